```python
import math, functools
import jax
import jax.numpy as jnp
from jax import lax
import numpy as np

D_MODEL = 1024
BATCH = 8
SEQ = 2048
DEPTH = 4

CTX_LEN = 256
GRID_W = 64

GLA_HEADS = 4
GLA_DK = 64
GLA_DV = 128
GLA_RANK = 16
GLA_NORMALIZER = 16.0
GLA_CHUNK = 16
ML_HEADS = 4
ML_DQK = 64
ML_DV = 128
ML_CHUNK = 64
M2_HEADS = 8
M2_HEADDIM = 64
M2_GROUPS = 2
M2_DSTATE = 128
M2_CHUNK = 64
DT_MIN = 0.001
DT_MAX = 0.1
CONV_W = 7

GLA_WIDTH = GLA_HEADS * GLA_DV
ML_WIDTH = ML_HEADS * ML_DV
M2_WIDTH = M2_HEADS * M2_HEADDIM
MIX_WIDTH = GLA_WIDTH + ML_WIDTH + M2_WIDTH

GLA_IN = 2 * GLA_HEADS * GLA_DK + 2 * GLA_WIDTH + 2 * GLA_RANK
ML_IN = 2 * ML_HEADS * ML_DQK + 2 * ML_WIDTH + 4 * ML_HEADS
M2_CONV_DIM = M2_WIDTH + 2 * M2_GROUPS * M2_DSTATE
M2_IN = M2_WIDTH + M2_CONV_DIM + 2 * M2_HEADS
IN_WIDTH = GLA_IN + ML_IN + M2_IN

N_EXPERTS = 32
TOP_K = 4
D_EXPERT = D_MODEL
SWIGLU_LIMIT = 7.0
SWIGLU_ALPHA = 1.702
MOE_BLOCK = 128

EPS = 1e-6
M_INIT = -1e30

kernel_name = "hybrid_gla_mlstm_ssd_moe_dit"


def rmsnorm(x, g):
    xf = x.astype(jnp.float32)
    y = xf * lax.rsqrt(jnp.mean(xf * xf, axis=-1, keepdims=True) + EPS)
    return (y * g.astype(jnp.float32)).astype(x.dtype)


def modulate(h, shift, scale):
    return h * (1 + scale) + shift


def conv_centred(u, w, b):
    ch = u.shape[-1]
    y = lax.conv_general_dilated(u, w[:, None, :].astype(u.dtype), window_strides=(1,),
                                 padding=[(CONV_W // 2, CONV_W // 2)],
                                 dimension_numbers=('NWC', 'WIO', 'NWC'), feature_group_count=ch)
    return y + b.astype(u.dtype)


def to_colmajor(u, rows):
    b_, t, ch = u.shape
    return u.reshape(b_, rows, GRID_W, ch).transpose(0, 2, 1, 3).reshape(b_, t, ch)


def from_colmajor(u, rows):
    b_, t, ch = u.shape
    return u.reshape(b_, GRID_W, rows, ch).transpose(0, 2, 1, 3).reshape(b_, t, ch)


def prefix_scan(scan_fn, ctx_args, lat_args, init, reverse):
    if reverse:
        ctx_args = [jnp.flip(t, axis=1) for t in ctx_args]
        lat_args = [jnp.flip(t, axis=1) for t in lat_args]
    o_c, state = scan_fn(*ctx_args, init)
    o_l, _ = scan_fn(*lat_args, state)
    if reverse:
        o_c, o_l = jnp.flip(o_c, axis=1), jnp.flip(o_l, axis=1)
    return o_c, o_l


def gla_scan(q, k, v, log_a, s0):
    dtype = v.dtype
    b_, t, h, dk = q.shape
    dv = v.shape[-1]
    L = GLA_CHUNK
    n = t // L
    f32 = jnp.float32
    q, k, la = (z.astype(f32).reshape(b_, n, L, h, dk) for z in (q, k, log_a))
    v = v.astype(f32).reshape(b_, n, L, h, dv)
    cum = jnp.cumsum(la, axis=2)
    cum_last = cum[:, :, -1]
    lower = jnp.tril(jnp.ones((L, L), bool))
    diff = cum[:, :, :, None] - cum[:, :, None, :]
    decay = jnp.exp(jnp.where(lower[:, :, None, None], diff, -jnp.inf))
    attn = jnp.einsum('bnihk,bnjhk,bnijhk->bnhij', q, k, decay)
    o = jnp.einsum('bnhij,bnjhv->bnihv', attn, v)
    k_end = k * jnp.exp(cum_last[:, :, None] - cum)
    d_state = jnp.einsum('bnjhk,bnjhv->nbhkv', k_end, v)

    def step(s, inp):
        a_c, ds_c = inp
        return a_c[..., None] * s + ds_c, s

    s_final, s_starts = lax.scan(step, s0, (jnp.exp(cum_last).transpose(1, 0, 2, 3), d_state))
    o = o + jnp.einsum('bnihk,nbhkv->bnihv', q * jnp.exp(cum), s_starts)
    return o.reshape(b_, t, h, dv).astype(dtype), s_final


def mlstm_scan(q, k, v, i_pre, log_f, state):
    dtype = v.dtype
    b_, t, h, dk = q.shape
    dv = v.shape[-1]
    L = ML_CHUNK
    n = t // L
    f32 = jnp.float32
    q, k = (z.astype(f32).reshape(b_, n, L, h, dk) for z in (q, k))
    v = v.astype(f32).reshape(b_, n, L, h, dv)
    ig, lf = (z.astype(f32).reshape(b_, n, L, h) for z in (i_pre, log_f))
    F = jnp.cumsum(lf, axis=2)
    F_last = F[:, :, -1]
    g_end = F_last[:, :, None] - F + ig
    m_loc = jnp.max(g_end, axis=2)
    w_end = jnp.exp(g_end - m_loc[:, :, None])
    dC = jnp.einsum('bnjh,bnjhk,bnjhv->nbhkv', w_end, k, v)
    dn = jnp.einsum('bnjh,bnjhk->nbhk', w_end, k)

    def step(carry, inp):
        C, nn, m = carry
        fl, ml, dC_c, dn_c = inp
        m_new = jnp.maximum(fl + m, ml)
        a = jnp.exp(fl + m - m_new)
        s = jnp.exp(ml - m_new)
        C_new = a[..., None, None] * C + s[..., None, None] * dC_c
        n_new = a[..., None] * nn + s[..., None] * dn_c
        return (C_new, n_new, m_new), (C, nn, m)

    final, (Cs, ns, ms) = lax.scan(step, state, (F_last.transpose(1, 0, 2), m_loc.transpose(1, 0, 2), dC, dn))
    lower = jnp.tril(jnp.ones((L, L), bool))
    Dlog = F[:, :, :, None] - F[:, :, None, :] + ig[:, :, None, :]
    Dlog = jnp.where(lower[:, :, None], Dlog, -jnp.inf)
    inter = F + ms.transpose(1, 0, 2)[:, :, None]
    m_row = jnp.maximum(inter, jnp.max(Dlog, axis=3))
    s_ij = jnp.einsum('bnihk,bnjhk->bnijh', q, k) * jnp.exp(Dlog - m_row[:, :, :, None])
    w_inter = jnp.exp(inter - m_row)
    num = jnp.einsum('bnijh,bnjhv->bnihv', s_ij, v) + w_inter[..., None] * jnp.einsum('bnihk,nbhkv->bnihv', q, Cs)
    den = jnp.sum(s_ij, axis=3) + w_inter * jnp.einsum('bnihk,nbhk->bnih', q, ns)
    hh = num / jnp.maximum(jnp.abs(den), jnp.exp(-m_row))[..., None]
    return hh.reshape(b_, t, h, dv).astype(dtype), final


def ssd_scan(A, x, dt, Bm, Cm, h0):
    dtype = x.dtype
    b_, t, h, p = x.shape
    rep = h // Bm.shape[2]
    ds = Bm.shape[-1]
    L = M2_CHUNK
    n = t // L
    f32 = jnp.float32
    Bh = jnp.repeat(Bm, rep, axis=2).astype(f32).reshape(b_, n, L, h, ds)
    Ch = jnp.repeat(Cm, rep, axis=2).astype(f32).reshape(b_, n, L, h, ds)
    dtf = dt.astype(f32).reshape(b_, n, L, h)
    xdt = x.astype(f32).reshape(b_, n, L, h, p) * dtf[..., None]
    cum = jnp.cumsum(dtf * A, axis=2)
    cum_last = cum[:, :, -1]
    lower = jnp.tril(jnp.ones((L, L), bool))
    seg = cum[:, :, :, None] - cum[:, :, None, :]
    decay = jnp.exp(jnp.where(lower[:, :, None], seg, -jnp.inf))
    sc = jnp.einsum('bnihs,bnjhs->bnijh', Ch, Bh) * decay
    y = jnp.einsum('bnijh,bnjhp->bnihp', sc, xdt)
    w_end = jnp.exp(cum_last[:, :, None] - cum)
    dH = jnp.einsum('bnjh,bnjhs,bnjhp->nbhps', w_end, Bh, xdt)

    def step(hs, inp):
        a_c, dh_c = inp
        return a_c[..., None, None] * hs + dh_c, hs

    h_final, h_starts = lax.scan(step, h0, (jnp.exp(cum_last).transpose(1, 0, 2), dH))
    y = y + jnp.einsum('bnihs,nbhps->bnihp', Ch * jnp.exp(cum)[..., None], h_starts)
    return y.reshape(b_, t, h, p).astype(dtype), h_final


def gla_mixer(u, uc, w_gate2, b_gate, norm_g):
    qk = GLA_HEADS * GLA_DK

    def prep(t):
        b_, n_t, _ = t.shape
        q = t[..., :qk].reshape(b_, n_t, GLA_HEADS, GLA_DK) * GLA_DK ** -0.5
        k = t[..., qk:2 * qk].reshape(b_, n_t, GLA_HEADS, GLA_DK)
        v = t[..., 2 * qk:2 * qk + GLA_WIDTH].reshape(b_, n_t, GLA_HEADS, GLA_DV)
        g = t[..., 2 * qk + GLA_WIDTH:2 * qk + 2 * GLA_WIDTH]
        lr = t[..., 2 * qk + 2 * GLA_WIDTH:].reshape(b_, n_t, 2, GLA_RANK)
        gk = jnp.einsum('btdr,drk->btdk', lr, w_gate2) + b_gate
        la = (jax.nn.log_sigmoid(gk.astype(jnp.float32)) / GLA_NORMALIZER).reshape(b_, n_t, 2, GLA_HEADS, GLA_DK)
        return q, k, v, g, la[:, :, 0], la[:, :, 1]

    q, k, v, g, la_f, la_b = prep(u)
    qc, kc, vc, gc, lac_f, lac_b = prep(uc)
    s0 = jnp.zeros((u.shape[0], GLA_HEADS, GLA_DK, GLA_DV), jnp.float32)
    oc_f, ol_f = prefix_scan(gla_scan, (qc, kc, vc, lac_f), (q, k, v, la_f), s0, False)
    oc_b, ol_b = prefix_scan(gla_scan, (qc, kc, vc, lac_b), (q, k, v, la_b), s0, True)

    def finish(o, gate):
        o = rmsnorm(o, norm_g)
        return (o.reshape(gate.shape) * jax.nn.silu(gate)).astype(u.dtype)

    return finish(ol_f + ol_b, g), finish(oc_f + oc_b, gc)


def mlstm_mixer(u, uc, conv_w, conv_b, b_i, b_f, norm_g):
    qk = ML_HEADS * ML_DQK

    def prep(t):
        b_, n_t, _ = t.shape
        qk_act = jax.nn.silu(conv_centred(t[..., :2 * qk], conv_w, conv_b))
        q = qk_act[..., :qk].reshape(b_, n_t, ML_HEADS, ML_DQK)
        k = qk_act[..., qk:].reshape(b_, n_t, ML_HEADS, ML_DQK) * ML_DQK ** -0.5
        v = t[..., 2 * qk:2 * qk + ML_WIDTH].reshape(b_, n_t, ML_HEADS, ML_DV)
        o = t[..., 2 * qk + ML_WIDTH:2 * qk + 2 * ML_WIDTH]
        gates = t[..., 2 * qk + 2 * ML_WIDTH:].reshape(b_, n_t, 4, ML_HEADS).astype(jnp.float32)
        i_f = gates[:, :, 0] + b_i[0]
        i_b = gates[:, :, 1] + b_i[1]
        lf_f = jax.nn.log_sigmoid(gates[:, :, 2] + b_f[0])
        lf_b = jax.nn.log_sigmoid(gates[:, :, 3] + b_f[1])
        return q, k, v, o, i_f, i_b, lf_f, lf_b

    q, k, v, o, i_f, i_b, lf_f, lf_b = prep(u)
    qc, kc, vc, oc, ic_f, ic_b, lfc_f, lfc_b = prep(uc)
    b_ = u.shape[0]
    init = (jnp.zeros((b_, ML_HEADS, ML_DQK, ML_DV), jnp.float32),
            jnp.zeros((b_, ML_HEADS, ML_DQK), jnp.float32),
            jnp.full((b_, ML_HEADS), M_INIT, jnp.float32))
    hc_f, hl_f = prefix_scan(mlstm_scan, (qc, kc, vc, ic_f, lfc_f), (q, k, v, i_f, lf_f), init, False)
    hc_b, hl_b = prefix_scan(mlstm_scan, (qc, kc, vc, ic_b, lfc_b), (q, k, v, i_b, lf_b), init, True)
    g_heads = norm_g.reshape(ML_HEADS, ML_DV)

    def finish(hh, ogate):
        hh = rmsnorm(hh, g_heads)
        return (hh.reshape(ogate.shape) * jax.nn.sigmoid(ogate)).astype(u.dtype)

    return finish(hl_f + hl_b, o), finish(hc_f + hc_b, oc)


def mamba2_mixer(u, uc, rows, conv_w, conv_b, dt_bias, A_log, D_skip, norm_g):
    gs = M2_GROUPS * M2_DSTATE

    def prep(t):
        b_, n_t, _ = t.shape
        z = t[..., :M2_WIDTH]
        xbc = jax.nn.silu(conv_centred(t[..., M2_WIDTH:M2_WIDTH + M2_CONV_DIM], conv_w, conv_b))
        xs = xbc[..., :M2_WIDTH].reshape(b_, n_t, M2_HEADS, M2_HEADDIM)
        Bm = xbc[..., M2_WIDTH:M2_WIDTH + gs].reshape(b_, n_t, M2_GROUPS, M2_DSTATE)
        Cm = xbc[..., M2_WIDTH + gs:].reshape(b_, n_t, M2_GROUPS, M2_DSTATE)
        dt_raw = t[..., M2_WIDTH + M2_CONV_DIM:].reshape(b_, n_t, 2, M2_HEADS).astype(jnp.float32)
        dt = jax.nn.softplus(dt_raw + dt_bias)
        return z, xs, Bm, Cm, dt[:, :, 0], dt[:, :, 1]

    z, xs, Bm, Cm, dt_f, dt_b = prep(to_colmajor(u, rows))
    zc, xsc, Bc, Cc, dtc_f, dtc_b = prep(uc)
    A = -jnp.exp(A_log.astype(jnp.float32))
    h0 = jnp.zeros((u.shape[0], M2_HEADS, M2_HEADDIM, M2_DSTATE), jnp.float32)
    yc_f, yl_f = prefix_scan(functools.partial(ssd_scan, A[0]), (xsc, dtc_f, Bc, Cc), (xs, dt_f, Bm, Cm), h0, False)
    yc_b, yl_b = prefix_scan(functools.partial(ssd_scan, A[1]), (xsc, dtc_b, Bc, Cc), (xs, dt_b, Bm, Cm), h0, True)
    g_groups = norm_g.reshape(M2_GROUPS, M2_WIDTH // M2_GROUPS)

    def finish(y, xv, zg):
        b_, n_t = y.shape[:2]
        y = (y + D_skip[:, None] * xv).reshape(b_, n_t, M2_WIDTH) * jax.nn.silu(zg)
        y = rmsnorm(y.reshape(b_, n_t, M2_GROUPS, M2_WIDTH // M2_GROUPS), g_groups)
        return y.reshape(b_, n_t, M2_WIDTH).astype(u.dtype)

    return from_colmajor(finish(yl_f + yl_b, xs, z), rows), finish(yc_f + yc_b, xsc, zc)


def swiglu_clamped(gu):
    glu = jnp.minimum(gu[..., :D_EXPERT], SWIGLU_LIMIT)
    lin = jnp.clip(gu[..., D_EXPERT:], -SWIGLU_LIMIT, SWIGLU_LIMIT)
    return glu * jax.nn.sigmoid(SWIGLU_ALPHA * glu) * (lin + 1)


def moe(h, router_w, router_b, w_gu, b_gu, w_dn, b_dn):
    n_tok, d = h.shape
    logits = (h @ router_w).astype(jnp.float32) + router_b.astype(jnp.float32)
    top_v, top_e = lax.top_k(logits, TOP_K)
    gates = jax.nn.softmax(top_v, axis=-1)
    n_assign = n_tok * TOP_K
    flat_e = top_e.reshape(-1).astype(jnp.int32)
    order = jnp.argsort(flat_e)
    sorted_e = flat_e[order]
    counts = jnp.bincount(flat_e, length=N_EXPERTS).astype(jnp.int32)
    padded = (counts + MOE_BLOCK - 1) // MOE_BLOCK * MOE_BLOCK
    pad_end = jnp.cumsum(padded)
    pad_start = pad_end - padded
    start = jnp.cumsum(counts) - counts
    slot = jnp.arange(n_assign, dtype=jnp.int32)
    row_sorted = (pad_start[sorted_e] + slot - start[sorted_e]).astype(jnp.int32)
    dest = jnp.zeros((n_assign,), jnp.int32).at[order].set(row_sorted)
    n_blocks = -(-n_assign // MOE_BLOCK) + N_EXPERTS
    n_rows = n_blocks * MOE_BLOCK
    row_tok = jnp.zeros((n_rows,), jnp.int32).at[dest].set(slot // TOP_K)
    block_start = jnp.arange(n_blocks, dtype=jnp.int32) * MOE_BLOCK
    block_e = jnp.minimum(jnp.searchsorted(pad_end, block_start, side='right'), N_EXPERTS - 1)
    xb = h[row_tok].reshape(n_blocks, MOE_BLOCK, d)

    def expert_block(args):
        xblk, e = args
        gu = xblk @ w_gu[e] + b_gu[e]
        return swiglu_clamped(gu) @ w_dn[e] + b_dn[e]

    yb = lax.map(expert_block, (xb, block_e)).reshape(n_rows, d)
    y = yb[dest].reshape(n_tok, TOP_K, d)
    return jnp.einsum('tk,tkd->td', gates.astype(y.dtype), y)


def hybrid_layer(x, xc, c_act, cc_act, rows, last,
                 w_mod, b_mod, norm1_g, w_in,
                 gla_w_gate2, gla_b_gate, gla_norm_g,
                 ml_conv_w, ml_conv_b, ml_b_i, ml_b_f, ml_norm_g,
                 m2_conv_w, m2_conv_b, m2_dt_bias, m2_A_log, m2_D, m2_norm_g,
                 w_out, norm2_g, router_w, router_b, moe_w_gu, moe_b_gu, moe_w_dn, moe_b_dn):
    b_, s, d = x.shape
    mod = (c_act @ w_mod + b_mod).reshape(b_, 6, 1, d)
    modc = (cc_act @ w_mod + b_mod).reshape(6, d)
    h = modulate(rmsnorm(x, norm1_g), mod[:, 0], mod[:, 1])
    hc = modulate(rmsnorm(xc, norm1_g), modc[0], modc[1])
    u = h @ w_in
    uc = hc @ w_in
    a0, a1 = GLA_IN, GLA_IN + ML_IN
    o_gla, oc_gla = gla_mixer(u[..., :a0], uc[..., :a0], gla_w_gate2, gla_b_gate, gla_norm_g)
    o_ml, oc_ml = mlstm_mixer(u[..., a0:a1], uc[..., a0:a1], ml_conv_w, ml_conv_b, ml_b_i, ml_b_f, ml_norm_g)
    o_m2, oc_m2 = mamba2_mixer(u[..., a1:], uc[..., a1:], rows, m2_conv_w, m2_conv_b, m2_dt_bias, m2_A_log, m2_D, m2_norm_g)
    x = x + mod[:, 2] * (jnp.concatenate([o_gla, o_ml, o_m2], axis=-1) @ w_out)
    h2 = modulate(rmsnorm(x, norm2_g), mod[:, 3], mod[:, 4])
    if last:
        y = moe(h2.reshape(-1, d), router_w, router_b, moe_w_gu, moe_b_gu, moe_w_dn, moe_b_dn)
        return x + mod[:, 5] * y.reshape(b_, s, d), xc
    xc = xc + modc[2] * (jnp.concatenate([oc_gla, oc_ml, oc_m2], axis=-1) @ w_out)
    h2c = modulate(rmsnorm(xc, norm2_g), modc[3], modc[4])
    tokens = jnp.concatenate([h2.reshape(-1, d), h2c.reshape(-1, d)], axis=0)
    y = moe(tokens, router_w, router_b, moe_w_gu, moe_b_gu, moe_w_dn, moe_b_dn)
    n_lat = b_ * s
    x = x + mod[:, 5] * y[:n_lat].reshape(b_, s, d)
    xc = xc + modc[5] * y[n_lat:].reshape(xc.shape)
    return x, xc


def setup_inputs(seed: int = 0) -> dict:
    key = jax.random.key(seed)
    ks = jax.random.split(key, 32)
    L = DEPTH
    D = D_MODEL

    def nrm(k, shape, s):
        return jax.random.normal(k, shape, jnp.float32) * s

    dt0 = jnp.exp(jax.random.uniform(ks[18], (L, 2, M2_HEADS), jnp.float32, math.log(DT_MIN), math.log(DT_MAX)))
    return {
        'x': nrm(ks[0], (BATCH, SEQ, D), 1.0),
        'c': nrm(ks[1], (BATCH, D), 1.0),
        'ctx': nrm(ks[2], (BATCH, CTX_LEN, D), 1.0),
        'c_ctx': nrm(ks[3], (D,), 1.0),
        'w_mod': nrm(ks[4], (L, D, 6 * D), 0.5 * D ** -0.5),
        'b_mod': nrm(ks[5], (L, 6 * D), 0.02),
        'norm1_g': 1.0 + nrm(ks[6], (L, D), 0.02),
        'w_in': nrm(ks[7], (L, D, IN_WIDTH), D ** -0.5),
        'gla_w_gate2': nrm(ks[8], (L, 2, GLA_RANK, GLA_HEADS * GLA_DK), GLA_RANK ** -0.5),
        'gla_b_gate': nrm(ks[9], (L, 2, GLA_HEADS * GLA_DK), 0.1),
        'gla_norm_g': 1.0 + nrm(ks[10], (L, GLA_DV), 0.02),
        'ml_conv_w': nrm(ks[11], (L, CONV_W, 2 * ML_HEADS * ML_DQK), CONV_W ** -0.5),
        'ml_conv_b': nrm(ks[12], (L, 2 * ML_HEADS * ML_DQK), 0.02),
        'ml_b_i': nrm(ks[13], (L, 2, ML_HEADS), 0.1),
        'ml_b_f': jnp.linspace(3.0, 6.0, ML_HEADS, dtype=jnp.float32) + nrm(ks[14], (L, 2, ML_HEADS), 0.1),
        'ml_norm_g': 1.0 + nrm(ks[15], (L, ML_WIDTH), 0.02),
        'm2_conv_w': nrm(ks[16], (L, CONV_W, M2_CONV_DIM), CONV_W ** -0.5),
        'm2_conv_b': nrm(ks[17], (L, M2_CONV_DIM), 0.02),
        'm2_dt_bias': dt0 + jnp.log(-jnp.expm1(-dt0)),
        'm2_A_log': jnp.log(jax.random.uniform(ks[19], (L, 2, M2_HEADS), jnp.float32, 1.0, 16.0)),
        'm2_D': 1.0 + nrm(ks[20], (L, M2_HEADS), 0.1),
        'm2_norm_g': 1.0 + nrm(ks[21], (L, M2_WIDTH), 0.02),
        'w_out': nrm(ks[22], (L, MIX_WIDTH, D), MIX_WIDTH ** -0.5),
        'norm2_g': 1.0 + nrm(ks[23], (L, D), 0.02),
        'router_w': nrm(ks[24], (L, D, N_EXPERTS), D ** -0.5),
        'router_b': nrm(ks[25], (L, N_EXPERTS), 0.01),
        'moe_w_gu': nrm(ks[26], (L, N_EXPERTS, D, 2 * D_EXPERT), D ** -0.5),
        'moe_b_gu': nrm(ks[27], (L, N_EXPERTS, 2 * D_EXPERT), 0.01),
        'moe_w_dn': nrm(ks[28], (L, N_EXPERTS, D_EXPERT, D), D_EXPERT ** -0.5),
        'moe_b_dn': nrm(ks[29], (L, N_EXPERTS, D), 0.01),
        'final_norm_g': 1.0 + nrm(ks[30], (D,), 0.02),
    }


def reference(x, c, ctx, c_ctx, w_mod, b_mod, norm1_g, w_in,
              gla_w_gate2, gla_b_gate, gla_norm_g,
              ml_conv_w, ml_conv_b, ml_b_i, ml_b_f, ml_norm_g,
              m2_conv_w, m2_conv_b, m2_dt_bias, m2_A_log, m2_D, m2_norm_g,
              w_out, norm2_g, router_w, router_b, moe_w_gu, moe_b_gu, moe_w_dn, moe_b_dn,
              final_norm_g):
    rows = x.shape[1] // GRID_W
    c_act = jax.nn.silu(c)
    cc_act = jax.nn.silu(c_ctx)
    xc = ctx
    for l in range(DEPTH):
        x, xc = hybrid_layer(x, xc, c_act, cc_act, rows, l == DEPTH - 1,
                             w_mod[l], b_mod[l], norm1_g[l], w_in[l],
                             gla_w_gate2[l], gla_b_gate[l], gla_norm_g[l],
                             ml_conv_w[l], ml_conv_b[l], ml_b_i[l], ml_b_f[l], ml_norm_g[l],
                             m2_conv_w[l], m2_conv_b[l], m2_dt_bias[l], m2_A_log[l], m2_D[l], m2_norm_g[l],
                             w_out[l], norm2_g[l], router_w[l], router_b[l],
                             moe_w_gu[l], moe_b_gu[l], moe_w_dn[l], moe_b_dn[l])
    return rmsnorm(x, final_norm_g)
```

```python
import functools
import math

import numpy as np
import jax
import jax.numpy as jnp
from jax import lax
from jax.experimental import pallas as pl
from jax.experimental.pallas import tpu as pltpu

F32 = jnp.float32
BF16 = jnp.bfloat16

D_MODEL = 1024
GRID_W = 64
GLA_HEADS, GLA_DK, GLA_DV, GLA_RANK = 4, 64, 128, 16
GLA_NORMALIZER = 16.0
ML_HEADS, ML_DQK, ML_DV = 4, 64, 128
M2_HEADS, M2_HEADDIM, M2_GROUPS, M2_DSTATE = 8, 64, 2, 128
CONV_W = 7
CONV_R = CONV_W // 2
N_EXPERTS, TOP_K, D_EXPERT = 32, 4, 1024
SWIGLU_LIMIT, SWIGLU_ALPHA = 7.0, 1.702
EPS = 1e-6
M_INIT = -1e30

GLA_WIDTH = GLA_HEADS * GLA_DV
ML_WIDTH = ML_HEADS * ML_DV
M2_WIDTH = M2_HEADS * M2_HEADDIM
MIX_WIDTH = GLA_WIDTH + ML_WIDTH + M2_WIDTH
GLA_IN = 2 * GLA_HEADS * GLA_DK + 2 * GLA_WIDTH + 2 * GLA_RANK
ML_IN = 2 * ML_HEADS * ML_DQK + 2 * ML_WIDTH + 4 * ML_HEADS
M2_CONV_DIM = M2_WIDTH + 2 * M2_GROUPS * M2_DSTATE
M2_IN = M2_WIDTH + M2_CONV_DIM + 2 * M2_HEADS
IN_WIDTH = GLA_IN + ML_IN + M2_IN

LANES = 128
SUBLANES = 8
CHUNK = 128
TOK_TILE = 256
MOE_TILE = 256
VMEM_LIMIT = 52 * 1024 * 1024

GLA_HEAD_COLS = 3 * LANES
GLA_PACK = GLA_HEADS * GLA_HEAD_COLS + LANES
ML_HEAD_COLS = 4 * LANES
ML_PACK = ML_HEADS * ML_HEAD_COLS
M2_GROUP_HEADS = M2_HEADS // M2_GROUPS
M2_GROUP_X = M2_GROUP_HEADS * M2_HEADDIM
M2_PAIRS = M2_GROUP_HEADS // 2
M2_DT_COLS = 2 * M2_PAIRS * LANES
M2_GROUP_COLS = 2 * M2_GROUP_X + 2 * M2_DSTATE + M2_DT_COLS
M2_PACK = M2_GROUPS * M2_GROUP_COLS
IN_PACK = GLA_PACK + ML_PACK + M2_PACK
ML_GATE_REP = LANES // 4


def _in_proj_column_map():
    cols = []
    qk = GLA_HEADS * GLA_DK
    for h in range(GLA_HEADS):
        cols += list(range(h * GLA_DK, (h + 1) * GLA_DK))
        cols += list(range(qk + h * GLA_DK, qk + (h + 1) * GLA_DK))
        cols += list(range(2 * qk + h * GLA_DV, 2 * qk + (h + 1) * GLA_DV))
        cols += list(range(2 * qk + GLA_WIDTH + h * GLA_DV, 2 * qk + GLA_WIDTH + (h + 1) * GLA_DV))
    lr0 = 2 * qk + 2 * GLA_WIDTH
    cols += list(range(lr0, lr0 + 2 * GLA_RANK)) + [-1] * (LANES - 2 * GLA_RANK)
    a0 = GLA_IN
    qk = ML_HEADS * ML_DQK
    g0 = a0 + 2 * qk + 2 * ML_WIDTH
    for h in range(ML_HEADS):
        cols += list(range(a0 + h * ML_DQK, a0 + (h + 1) * ML_DQK))
        cols += list(range(a0 + qk + h * ML_DQK, a0 + qk + (h + 1) * ML_DQK))
        cols += list(range(a0 + 2 * qk + h * ML_DV, a0 + 2 * qk + (h + 1) * ML_DV))
        cols += list(range(a0 + 2 * qk + ML_WIDTH + h * ML_DV, a0 + 2 * qk + ML_WIDTH + (h + 1) * ML_DV))
        for gate in range(4):
            cols += [g0 + gate * ML_HEADS + h] * ML_GATE_REP
    a1 = GLA_IN + ML_IN
    x0 = a1 + M2_WIDTH
    dt0 = a1 + M2_WIDTH + M2_CONV_DIM
    for g in range(M2_GROUPS):
        cols += list(range(a1 + g * M2_GROUP_X, a1 + (g + 1) * M2_GROUP_X))
        cols += list(range(x0 + g * M2_GROUP_X, x0 + (g + 1) * M2_GROUP_X))
        cols += list(range(x0 + M2_WIDTH + g * M2_DSTATE, x0 + M2_WIDTH + (g + 1) * M2_DSTATE))
        cols += list(range(x0 + M2_WIDTH + M2_GROUPS * M2_DSTATE + g * M2_DSTATE,
                           x0 + M2_WIDTH + M2_GROUPS * M2_DSTATE + (g + 1) * M2_DSTATE))
        for d in range(2):
            for h in range(M2_GROUP_HEADS):
                cols += [dt0 + d * M2_HEADS + g * M2_GROUP_HEADS + h] * M2_HEADDIM
    cols = np.asarray(cols, np.int32)
    assert cols.shape == (IN_PACK,)
    return cols


_IN_COLS = _in_proj_column_map()


def _bdot(a, b):
    return jnp.dot(a.astype(BF16), b.astype(BF16), preferred_element_type=F32)


def _bdot_nt(a, b):
    return lax.dot_general(a.astype(BF16), b.astype(BF16), (((1,), (1,)), ((), ())),
                           preferred_element_type=F32)


def _bdot_tn(a, b):
    return lax.dot_general(a.astype(BF16), b.astype(BF16), (((0,), (0,)), ((), ())),
                           preferred_element_type=F32)


def _split2(a):
    hi = a.astype(BF16)
    lo = (a - hi.astype(F32)).astype(BF16)
    return hi, lo


def _split3(a):
    hi = a.astype(BF16)
    r = a - hi.astype(F32)
    mid = r.astype(BF16)
    lo = (r - mid.astype(F32)).astype(BF16)
    return hi, mid, lo


def _hdot(a, b):
    ah, al = _split2(a)
    bh, bl = _split2(b)
    d = functools.partial(jnp.dot, preferred_element_type=F32)
    return d(ah, bh) + (d(ah, bl) + d(al, bh))


def _tri_dot(tri, a):
    hi, mid, lo = _split3(a)
    d = functools.partial(jnp.dot, preferred_element_type=F32)
    return d(tri, hi) + (d(tri, mid) + d(tri, lo))


def _rms(x):
    return x * lax.rsqrt(jnp.mean(x * x, axis=-1, keepdims=True) + EPS)


def _sigmoid(x):
    return 1.0 / (1.0 + jnp.exp(-x))


def _silu(x):
    return x * _sigmoid(x)


def _log_sigmoid(x):
    return jnp.minimum(x, 0.0) - jnp.log(1.0 + jnp.exp(-jnp.abs(x)))


def _softplus(x):
    return jnp.maximum(x, 0.0) + jnp.log(1.0 + jnp.exp(-jnp.abs(x)))


def _lane_iota(shape):
    return lax.broadcasted_iota(jnp.int32, shape, len(shape) - 1)


def _row_iota(shape):
    return lax.broadcasted_iota(jnp.int32, shape, len(shape) - 2)


def _lane_rep(x, lo, width):
    lane = _lane_iota(x.shape)
    y = jnp.where((lane >= lo) & (lane < lo + width), x, 0.0)
    w = width
    while w < LANES:
        y = y + pltpu.roll(y, w, axis=1)
        w *= 2
    return y


def _swap_halves(x):
    return pltpu.roll(x, LANES // 2, axis=1)


def _tri_incl(n):
    r = lax.broadcasted_iota(jnp.int32, (n, n), 0)
    c = lax.broadcasted_iota(jnp.int32, (n, n), 1)
    return (r >= c).astype(BF16)


def _mod_kernel(c_ref, w_ref, b_ref, o_ref):
    o_ref[...] = _hdot(_silu(c_ref[...]), w_ref[...]) + b_ref[...]


def _mod_table(c_rows, w_mod, b_mod):
    n_layers, d, n6 = w_mod.shape
    r = c_rows.shape[0]
    tn = 1536
    return pl.pallas_call(
        _mod_kernel,
        grid=(n_layers, n6 // tn),
        in_specs=[pl.BlockSpec((r, d), lambda l, j: (0, 0)),
                  pl.BlockSpec((None, d, tn), lambda l, j: (l, 0, j)),
                  pl.BlockSpec((None, 1, tn), lambda l, j: (l, 0, j))],
        out_specs=pl.BlockSpec((None, r, tn), lambda l, j: (l, 0, j)),
        out_shape=jax.ShapeDtypeStruct((n_layers, r, n6), F32),
        compiler_params=pltpu.CompilerParams(dimension_semantics=("arbitrary", "arbitrary"),
                                             vmem_limit_bytes=VMEM_LIMIT),
        name="adaln_mod",
    )(c_rows, w_mod, b_mod.reshape(n_layers, 1, n6))


def _inproj_kernel(x_ref, sh_ref, sc_ref, g_ref, w_ref, ogla_ref, oml_ref, om2_ref):
    h = _rms(x_ref[...]) * g_ref[...]
    h = (h * (1.0 + sc_ref[...]) + sh_ref[...]).astype(BF16)
    ogla_ref[...] = jnp.dot(h, w_ref[:, :GLA_PACK], preferred_element_type=F32)
    oml_ref[...] = jnp.dot(h, w_ref[:, GLA_PACK:GLA_PACK + ML_PACK], preferred_element_type=F32)
    m2 = jnp.dot(h, w_ref[:, GLA_PACK + ML_PACK:], preferred_element_type=F32)
    for s in range(M2_PACK // LANES):
        om2_ref[s] = m2[:, s * LANES:(s + 1) * LANES]


def _mod_spec(which, n_lat_tiles, n_batch):
    def imap(b, t):
        row = jnp.where(t < n_lat_tiles, b, n_batch)
        return (row * 6 + which, 0, 0)
    return pl.BlockSpec((None, 1, D_MODEL), imap)


def _in_proj(xa, mod3, norm_g, w_pack, n_lat):
    nb, t, d = xa.shape
    grid = (nb, t // TOK_TILE)
    n_lat_tiles = n_lat // TOK_TILE
    tok = lambda width: pl.BlockSpec((None, TOK_TILE, width), lambda b, i: (b, i, 0))
    return pl.pallas_call(
        _inproj_kernel,
        grid=grid,
        in_specs=[tok(d), _mod_spec(0, n_lat_tiles, nb), _mod_spec(1, n_lat_tiles, nb),
                  pl.BlockSpec((1, d), lambda b, i: (0, 0)),
                  pl.BlockSpec((d, IN_PACK), lambda b, i: (0, 0))],
        out_specs=[tok(GLA_PACK), tok(ML_PACK),
                   pl.BlockSpec((None, M2_PACK // LANES, TOK_TILE, LANES), lambda b, i: (b, 0, i, 0))],
        out_shape=[jax.ShapeDtypeStruct((nb, t, GLA_PACK), F32),
                   jax.ShapeDtypeStruct((nb, t, ML_PACK), F32),
                   jax.ShapeDtypeStruct((nb, M2_PACK // LANES, t, LANES), F32)],
        compiler_params=pltpu.CompilerParams(dimension_semantics=("arbitrary", "arbitrary"),
                                             vmem_limit_bytes=VMEM_LIMIT),
        name="in_proj",
    )(xa, mod3, mod3, norm_g.reshape(1, d), w_pack)


def _chunk_rows(c):
    return pl.ds(pl.multiple_of(c * CHUNK, CHUNK), CHUNK)


def _row_bcast(row):
    return jnp.broadcast_to(row, (SUBLANES, row.shape[-1]))


def _gla_kernel(u_ref, lr_ref, w2_ref, b2_ref, ng_ref, o_ref,
                oin_ref, q2_ref, ds_ref, a_ref, stf_ref, stb_ref, *, n_lat_chunks):
    n = u_ref.shape[0] // CHUNK
    tri = _tri_incl(CHUNK)
    fwd = _lane_iota((CHUNK, LANES)) < LANES // 2
    row = _row_iota((CHUNK, CHUNK))
    col = _lane_iota((CHUNK, CHUNK))
    w2 = w2_ref[...]
    b2 = b2_ref[...]

    def local(c, carry):
        rows = _chunk_rows(c)
        qk = u_ref[rows, 0:LANES]
        v = u_ref[rows, LANES:2 * LANES]
        la = _log_sigmoid(_hdot(lr_ref[rows, :], w2) + b2) * (1.0 / GLA_NORMALIZER)
        p = _tri_dot(tri, la)
        tot = p[CHUNK - 1:CHUNK, :]
        cum = jnp.where(fwd, p, tot - p + la)
        mid = cum[CHUNK // 2:CHUNK // 2 + 1, :]
        sw = _swap_halves(qk)
        qq = jnp.where(fwd, qk, sw) * (GLA_DK ** -0.5)
        kk = jnp.where(fwd, sw, qk)
        qe = qq * jnp.exp(cum - mid)
        ke = kk * jnp.exp(mid - cum)
        af = _bdot_nt(jnp.where(fwd, qe, 0.0), ke)
        ab = _bdot_nt(jnp.where(fwd, 0.0, qe), ke)
        attn = jnp.where(row >= col, af, 0.0) + jnp.where(col >= row, ab, 0.0)
        oin_ref[rows, :] = _bdot(attn, v)
        q2_ref[rows, :] = qq * jnp.exp(cum)
        ds_ref[c] = _bdot_tn(v, kk * jnp.exp(tot - cum))
        a_ref[c] = _row_bcast(jnp.exp(tot))
        return carry

    lax.fori_loop(0, n, local, 0)

    def scan(s, st):
        f = lax.rem(s + n_lat_chunks, n)
        g = n - 1 - s
        stf_ref[f] = st
        stb_ref[g] = st
        a = jnp.where(fwd[0:1], a_ref[f][0:1], a_ref[g][0:1])
        return st * a + jnp.where(fwd, ds_ref[f], ds_ref[g])

    lax.fori_loop(0, n, scan, jnp.zeros((GLA_DV, LANES), F32))

    def finish(c, carry):
        rows = _chunk_rows(c)
        st = jnp.where(fwd, stf_ref[c], stb_ref[c])
        o = oin_ref[rows, :] + _bdot_nt(q2_ref[rows, :], st)
        o = _rms(o) * ng_ref[...]
        o_ref[rows, :] = (o * _silu(u_ref[rows, 2 * LANES:3 * LANES])).astype(o_ref.dtype)
        return carry

    lax.fori_loop(0, n, finish, 0)


def _gla_mixer(u, w2, b2, ng, n_lat):
    nb, t, _ = u.shape
    n = t // CHUNK
    return pl.pallas_call(
        functools.partial(_gla_kernel, n_lat_chunks=n_lat // CHUNK),
        grid=(nb, GLA_HEADS),
        in_specs=[pl.BlockSpec((None, t, GLA_HEAD_COLS), lambda b, h: (b, 0, h)),
                  pl.BlockSpec((None, t, LANES), lambda b, h: (b, 0, GLA_HEADS * GLA_HEAD_COLS // LANES)),
                  pl.BlockSpec((None, LANES, LANES), lambda b, h: (h, 0, 0)),
                  pl.BlockSpec((None, 1, LANES), lambda b, h: (h, 0, 0)),
                  pl.BlockSpec((1, LANES), lambda b, h: (0, 0))],
        out_specs=pl.BlockSpec((None, t, GLA_DV), lambda b, h: (b, 0, h)),
        out_shape=jax.ShapeDtypeStruct((nb, t, GLA_WIDTH), BF16),
        scratch_shapes=[pltpu.VMEM((t, LANES), F32), pltpu.VMEM((t, LANES), F32),
                        pltpu.VMEM((n, GLA_DV, LANES), F32), pltpu.VMEM((n, SUBLANES, LANES), F32),
                        pltpu.VMEM((n, GLA_DV, LANES), F32), pltpu.VMEM((n, GLA_DV, LANES), F32)],
        compiler_params=pltpu.CompilerParams(dimension_semantics=("arbitrary", "arbitrary"),
                                             vmem_limit_bytes=VMEM_LIMIT),
        name="gla_mixer",
    )(u, u, w2, b2, ng)


def _zero_pads(pad_ref, n_lat, t):
    z = jnp.zeros((SUBLANES, pad_ref.shape[1]), F32)
    pad_ref[0:SUBLANES, :] = z
    pad_ref[SUBLANES + n_lat:2 * SUBLANES + n_lat, :] = z
    pad_ref[2 * SUBLANES + t:3 * SUBLANES + t, :] = z


def _pad_base(c, n_lat_chunks):
    return c * CHUNK + (SUBLANES if c < n_lat_chunks else 2 * SUBLANES)


def _conv_silu(pad_ref, w_ref, b_ref, out_ref, n, n_lat_chunks):
    width = pad_ref.shape[1]
    for c in range(n):
        base = _pad_base(c, n_lat_chunks)
        for l0 in range(0, width, LANES):
            acc = None
            for j in range(CONV_W):
                term = w_ref[j:j + 1, l0:l0 + LANES] * pad_ref[base + j - CONV_R:base + j - CONV_R + CHUNK,
                                                               l0:l0 + LANES]
                acc = term if acc is None else acc + term
            out_ref[c * CHUNK:(c + 1) * CHUNK, l0:l0 + LANES] = _silu(acc + b_ref[:, l0:l0 + LANES])


def _mlstm_kernel(u_ref, cw_ref, cb_ref, gb_ref, ng_ref, o_ref,
                  pad_ref, qk_ref, fc_ref, rc_ref, dc_ref, tot_ref, mloc_ref, stm_ref, *, n_lat_chunks):
    t = u_ref.shape[0]
    n = t // CHUNK
    n_lat = n_lat_chunks * CHUNK
    tri = _tri_incl(CHUNK)
    lane = _lane_iota((CHUNK, LANES))
    hi_half = lane >= LANES // 2
    row = _row_iota((CHUNK, CHUNK))
    col = _lane_iota((CHUNK, CHUNK))
    masks = (row >= col, col >= row)
    ones = jnp.ones((CHUNK, LANES), F32)

    _zero_pads(pad_ref, n_lat, t)
    pad_ref[SUBLANES:SUBLANES + n_lat, :] = u_ref[0:n_lat, 0:LANES]
    pad_ref[2 * SUBLANES + n_lat:2 * SUBLANES + t, :] = u_ref[n_lat:t, 0:LANES]
    _conv_silu(pad_ref, cw_ref, cb_ref, qk_ref, n, n_lat_chunks)

    def khat_of(qk):
        return jnp.where(hi_half, qk, 0.0) * (ML_DQK ** -0.5)

    def local(c, carry):
        rows = _chunk_rows(c)
        g = u_ref[rows, 3 * LANES:4 * LANES] + gb_ref[...]
        g = jnp.where(hi_half, _log_sigmoid(g), g)
        p = _tri_dot(tri, g)
        tot = p[CHUNK - 1:CHUNK, :]
        f_dir = (_lane_rep(p, 2 * ML_GATE_REP, ML_GATE_REP),
                 _lane_rep(tot - p + g, 3 * ML_GATE_REP, ML_GATE_REP))
        i_dir = (_lane_rep(g, 0, ML_GATE_REP), _lane_rep(g, ML_GATE_REP, ML_GATE_REP))
        khat = khat_of(qk_ref[rows, :])
        vaug = jnp.concatenate([u_ref[rows, LANES:2 * LANES], ones], axis=1)
        for d in range(2):
            fc, ig = f_dir[d], i_dir[d]
            tt = fc[CHUNK - 1:CHUNK, :] if d == 0 else fc[0:1, :]
            gend = tt - fc + ig
            mloc = jnp.max(gend, axis=0, keepdims=True)
            dc_ref[d, c] = _bdot_tn(khat * jnp.exp(gend - mloc), vaug)
            tot_ref[d, c] = _row_bcast(tt)
            mloc_ref[d, c] = _row_bcast(mloc)
            fc_ref[d, rows, :] = fc
            rc_ref[d, rows, :] = ig - fc
        return carry

    lax.fori_loop(0, n, local, 0)

    def scan(s, carry):
        new = []
        for d, idx in ((0, lax.rem(s + n_lat_chunks, n)), (1, n - 1 - s)):
            cst, m = carry[d]
            inc = dc_ref[d, idx]
            dc_ref[d, idx] = cst
            stm_ref[d, idx] = _row_bcast(m)
            tt = tot_ref[d, idx][0:1]
            ml = mloc_ref[d, idx][0:1]
            m_new = jnp.maximum(tt + m, ml)
            a = jnp.exp(tt + m - m_new)[:, 0:1]
            sc = jnp.exp(ml - m_new)[:, 0:1]
            new.append((a * cst + sc * inc, m_new))
        return tuple(new)

    init = (jnp.zeros((LANES, 2 * LANES), F32), jnp.full((1, LANES), M_INIT, F32))
    lax.fori_loop(0, n, scan, (init, init))

    def finish(c, carry):
        rows = _chunk_rows(c)
        qk = qk_ref[rows, :]
        qhat = jnp.where(hi_half, _swap_halves(qk), 0.0)
        s_qk = _bdot_nt(qhat, khat_of(qk))
        vaug = jnp.concatenate([u_ref[rows, LANES:2 * LANES], ones], axis=1)
        h = None
        for d in range(2):
            fc = fc_ref[d, rows, :]
            dlog = jnp.where(masks[d], fc + rc_ref[d, rows, :].T, -jnp.inf)
            inter = fc + stm_ref[d, c][0:1]
            m_row = jnp.maximum(inter, jnp.max(dlog, axis=1, keepdims=True))
            w_inter = jnp.exp(inter - m_row)
            nd = (_bdot(s_qk * jnp.exp(dlog - m_row), vaug)
                  + jnp.concatenate([w_inter, w_inter], axis=1) * _bdot(qhat, dc_ref[d, c]))
            hd = nd[:, 0:LANES] / jnp.maximum(jnp.abs(nd[:, LANES:]), jnp.exp(-m_row))
            h = hd if h is None else h + hd
        h = _rms(h) * ng_ref[...]
        o_ref[rows, :] = (h * _sigmoid(u_ref[rows, 2 * LANES:3 * LANES])).astype(o_ref.dtype)
        return carry

    lax.fori_loop(0, n, finish, 0)


def _mlstm_mixer(u, cw, cb, gb, ng, n_lat):
    nb, t, _ = u.shape
    n = t // CHUNK
    head = lambda rows: pl.BlockSpec((None, rows, LANES), lambda b, h: (h, 0, 0))
    return pl.pallas_call(
        functools.partial(_mlstm_kernel, n_lat_chunks=n_lat // CHUNK),
        grid=(nb, ML_HEADS),
        in_specs=[pl.BlockSpec((None, t, ML_HEAD_COLS), lambda b, h: (b, 0, h)),
                  head(SUBLANES), head(1), head(1), head(1)],
        out_specs=pl.BlockSpec((None, t, ML_DV), lambda b, h: (b, 0, h)),
        out_shape=jax.ShapeDtypeStruct((nb, t, ML_WIDTH), BF16),
        scratch_shapes=[pltpu.VMEM((t + 3 * SUBLANES, LANES), F32), pltpu.VMEM((t, LANES), F32),
                        pltpu.VMEM((2, t, LANES), F32), pltpu.VMEM((2, t, LANES), F32),
                        pltpu.VMEM((2, n, LANES, 2 * LANES), F32),
                        pltpu.VMEM((2, n, SUBLANES, LANES), F32), pltpu.VMEM((2, n, SUBLANES, LANES), F32),
                        pltpu.VMEM((2, n, SUBLANES, LANES), F32)],
        compiler_params=pltpu.CompilerParams(dimension_semantics=("arbitrary", "arbitrary"),
                                             vmem_limit_bytes=VMEM_LIMIT),
        name="mlstm_mixer",
    )(u, cw, cb, gb, ng)


M2_CONV_COLS = M2_GROUP_X + 2 * M2_DSTATE
M2_GROUP_SLABS = M2_GROUP_COLS // LANES
M2_Z_SLABS = M2_GROUP_X // LANES
M2_CONV_SLABS = M2_CONV_COLS // LANES
M2_DT_SLABS = M2_DT_COLS // LANES


def _ssd_kernel(u_ref, cw_ref, cb_ref, dtb_ref, alog_ref, dsk_ref, ng_ref, o_ref,
                pad_ref, xc_ref, dt_ref, dh_ref, a_ref, y_ref, *, n_lat_chunks):
    t = u_ref.shape[1]
    n = t // CHUNK
    n_lat = n_lat_chunks * CHUNK
    grid_rows = n_lat // GRID_W
    n_state = 2 * M2_PAIRS
    conv0 = M2_Z_SLABS
    dt0 = M2_Z_SLABS + M2_CONV_SLABS
    tri = _tri_incl(CHUNK)
    lo_half = _lane_iota((CHUNK, LANES)) < LANES // 2
    row = _row_iota((CHUNK, CHUNK))
    col = _lane_iota((CHUNK, CHUNK))
    masks = (row >= col, col >= row)
    fwd_cols = _lane_iota((CHUNK, M2_DT_COLS)) < M2_DT_COLS // 2
    a_row = -jnp.exp(alog_ref[...])

    _zero_pads(pad_ref, n_lat, t)
    for w in range(GRID_W):
        src = pl.ds(w, grid_rows, stride=GRID_W)
        dst = slice(w * grid_rows, (w + 1) * grid_rows)
        for s in range(M2_CONV_SLABS):
            pad_ref[SUBLANES + w * grid_rows:SUBLANES + (w + 1) * grid_rows, s * LANES:(s + 1) * LANES] = (
                u_ref[conv0 + s, src, :])
        for s in range(M2_DT_SLABS):
            dt_ref[dst, s * LANES:(s + 1) * LANES] = u_ref[dt0 + s, src, :]
    for s in range(M2_CONV_SLABS):
        pad_ref[2 * SUBLANES + n_lat:2 * SUBLANES + t, s * LANES:(s + 1) * LANES] = u_ref[conv0 + s, n_lat:t, :]
    for s in range(M2_DT_SLABS):
        dt_ref[n_lat:t, s * LANES:(s + 1) * LANES] = u_ref[dt0 + s, n_lat:t, :]
    _conv_silu(pad_ref, cw_ref, cb_ref, xc_ref, n, n_lat_chunks)

    def decay_terms(rows):
        dt = dt_ref[rows, :]
        da = dt * a_row
        p = _tri_dot(tri, da)
        tot = p[CHUNK - 1:CHUNK, :]
        return dt, jnp.where(fwd_cols, p, tot - p + da), tot

    def local(c, carry):
        rows = _chunk_rows(c)
        dt_ref[rows, :] = _softplus(dt_ref[rows, :] + dtb_ref[...])
        dt, cum, tot = decay_terms(rows)
        x = xc_ref[rows, 0:M2_GROUP_X]
        bm = xc_ref[rows, M2_GROUP_X:M2_GROUP_X + M2_DSTATE]
        cm = xc_ref[rows, M2_GROUP_X + M2_DSTATE:]
        g = _bdot_nt(cm, bm)
        y = [None] * M2_PAIRS
        for d in range(2):
            for p in range(M2_PAIRS):
                k = d * M2_PAIRS + p
                sl = slice(k * LANES, (k + 1) * LANES)
                fp = cum[:, sl]
                tt = tot[:, sl]
                xdt = x[:, p * LANES:(p + 1) * LANES] * dt[:, sl]
                dh_ref[c, k] = _bdot_tn(bm, jnp.exp(tt - fp) * xdt)
                a_ref[c, k] = _row_bcast(jnp.exp(tt))
                sw = _swap_halves(fp)
                halves = []
                for fh in (jnp.where(lo_half, fp, sw), jnp.where(lo_half, sw, fp)):
                    dec = jnp.exp(jnp.where(masks[d], fh - fh.T, -jnp.inf))
                    halves.append(_bdot(g * dec, xdt))
                yp = jnp.where(lo_half, halves[0], halves[1])
                y[p] = yp if y[p] is None else y[p] + yp
        for p in range(M2_PAIRS):
            y_ref[p, rows, :] = y[p]
        return carry

    lax.fori_loop(0, n, local, 0)

    def scan(s, carry):
        f = lax.rem(s + n_lat_chunks, n)
        g = n - 1 - s
        new = []
        for k in range(n_state):
            idx = f if k < M2_PAIRS else g
            inc = dh_ref[idx, k]
            dh_ref[idx, k] = carry[k]
            new.append(carry[k] * a_ref[idx, k][0:1] + inc)
        return tuple(new)

    lax.fori_loop(0, n, scan, tuple(jnp.zeros((M2_DSTATE, LANES), F32) for _ in range(n_state)))

    def finish(c, carry):
        rows = _chunk_rows(c)
        _, cum, _ = decay_terms(rows)
        cm = xc_ref[rows, M2_GROUP_X + M2_DSTATE:]
        y = [y_ref[p, rows, :] for p in range(M2_PAIRS)]
        for k in range(n_state):
            p = k % M2_PAIRS
            y[p] = y[p] + jnp.exp(cum[:, k * LANES:(k + 1) * LANES]) * _bdot(cm, dh_ref[c, k])
        for p in range(M2_PAIRS):
            lanes = slice(p * LANES, (p + 1) * LANES)
            y_ref[p, rows, :] = y[p] + dsk_ref[:, lanes] * xc_ref[rows, lanes]
        return carry

    lax.fori_loop(0, n, finish, 0)

    def gate_norm(y_rows, z_rows):
        y = jnp.concatenate([y_ref[p, y_rows, :] for p in range(M2_PAIRS)], axis=1)
        z = jnp.concatenate([u_ref[s, z_rows, :] for s in range(M2_Z_SLABS)], axis=1)
        return (_rms(y * _silu(z)) * ng_ref[...]).astype(o_ref.dtype)

    for r in range(grid_rows):
        dst = slice(r * GRID_W, (r + 1) * GRID_W)
        o_ref[dst, :] = gate_norm(pl.ds(r, GRID_W, stride=grid_rows), dst)
    o_ref[n_lat:t, :] = gate_norm(slice(n_lat, t), slice(n_lat, t))


def _ssd_mixer(u, cw, cb, dtb, alog, dsk, ng, n_lat):
    nb, _, t, _ = u.shape
    n = t // CHUNK
    grp = lambda rows, width: pl.BlockSpec((None, rows, width), lambda b, g: (g, 0, 0))
    return pl.pallas_call(
        functools.partial(_ssd_kernel, n_lat_chunks=n_lat // CHUNK),
        grid=(nb, M2_GROUPS),
        in_specs=[pl.BlockSpec((None, M2_GROUP_SLABS, t, LANES), lambda b, g: (b, g, 0, 0),
                               pipeline_mode=pl.Buffered(1)),
                  grp(SUBLANES, M2_CONV_COLS), grp(1, M2_CONV_COLS), grp(1, M2_DT_COLS), grp(1, M2_DT_COLS),
                  grp(1, M2_GROUP_X), grp(1, M2_GROUP_X)],
        out_specs=pl.BlockSpec((None, t, M2_GROUP_X), lambda b, g: (b, 0, g)),
        out_shape=jax.ShapeDtypeStruct((nb, t, M2_WIDTH), BF16),
        scratch_shapes=[pltpu.VMEM((t + 3 * SUBLANES, M2_CONV_COLS), F32), pltpu.VMEM((t, M2_CONV_COLS), F32),
                        pltpu.VMEM((t, M2_DT_COLS), F32),
                        pltpu.VMEM((n, 2 * M2_PAIRS, M2_DSTATE, LANES), F32),
                        pltpu.VMEM((n, 2 * M2_PAIRS, SUBLANES, LANES), F32),
                        pltpu.VMEM((M2_PAIRS, t, LANES), F32)],
        compiler_params=pltpu.CompilerParams(dimension_semantics=("arbitrary", "arbitrary"),
                                             vmem_limit_bytes=VMEM_LIMIT),
        name="ssd_mixer",
    )(u, cw, cb, dtb, alog, dsk, ng)


ROW_SLABS = D_MODEL // LANES


def _slab(c, n_rows):
    return pl.ds(c, n_rows, stride=ROW_SLABS)


def _outproj_kernel(x_ref, og_ref, om_ref, os_ref, w_ref, gate_ref, sh_ref, sc_ref, g2_ref, rw_ref, rb_ref,
                    xo_ref, h2_ref, lg_ref):
    mix = (jnp.dot(og_ref[...], w_ref[0:GLA_WIDTH, :], preferred_element_type=F32)
           + jnp.dot(om_ref[...], w_ref[GLA_WIDTH:GLA_WIDTH + ML_WIDTH, :], preferred_element_type=F32)
           + jnp.dot(os_ref[...], w_ref[GLA_WIDTH + ML_WIDTH:, :], preferred_element_type=F32))
    x = x_ref[...] + gate_ref[...] * mix
    xo_ref[...] = x
    h2 = (_rms(x) * g2_ref[...]) * (1.0 + sc_ref[...]) + sh_ref[...]
    for c in range(ROW_SLABS):
        h2_ref[_slab(c, TOK_TILE), :] = h2[:, c * LANES:(c + 1) * LANES]
    lg_ref[...] = _hdot(h2, rw_ref[...]) + rb_ref[...]


def _out_proj(xa, o_gla, o_ml, o_m2, w_out, mod3, norm_g, rw, rb, n_lat):
    nb, t, d = xa.shape
    nt = t // TOK_TILE
    n_lat_tiles = n_lat // TOK_TILE
    tok = lambda width: pl.BlockSpec((None, TOK_TILE, width), lambda b, i: (b, i, 0))
    const = lambda r, c: pl.BlockSpec((r, c), lambda b, i: (0, 0))
    mod = lambda which: _mod_spec(which, n_lat_tiles, nb)
    return pl.pallas_call(
        _outproj_kernel,
        grid=(nb, nt),
        in_specs=[tok(d), tok(GLA_WIDTH), tok(ML_WIDTH), tok(M2_WIDTH), const(MIX_WIDTH, d),
                  mod(2), mod(3), mod(4), const(1, d), const(d, LANES), const(1, LANES)],
        out_specs=[tok(d),
                   pl.BlockSpec((TOK_TILE * ROW_SLABS, LANES), lambda b, i: (b * nt + i, 0)),
                   pl.BlockSpec((TOK_TILE, LANES), lambda b, i: (b * nt + i, 0))],
        out_shape=[jax.ShapeDtypeStruct((nb, t, d), F32),
                   jax.ShapeDtypeStruct((nb * t * ROW_SLABS, LANES), F32),
                   jax.ShapeDtypeStruct((nb * t, LANES), F32)],
        compiler_params=pltpu.CompilerParams(dimension_semantics=("arbitrary", "arbitrary"),
                                             vmem_limit_bytes=VMEM_LIMIT),
        name="out_proj",
    )(xa, o_gla, o_ml, o_m2, w_out, mod3, mod3, mod3, norm_g.reshape(1, d), rw, rb)


ROUTE_TILE = 256


def _route_kernel(lg_ref, e_ref, gt_ref, rk_ref, cnt_ref, base_ref):
    @pl.when(pl.program_id(0) == 0)
    def _():
        base_ref[...] = jnp.zeros_like(base_ref)

    lane = _lane_iota((ROUTE_TILE, LANES))
    work = lg_ref[...]
    vals, idxs = [], []
    for _ in range(TOP_K):
        m = jnp.max(work, axis=1, keepdims=True)
        idx = jnp.min(jnp.where(work == m, lane, LANES), axis=1, keepdims=True)
        vals.append(m)
        idxs.append(idx)
        work = jnp.where(lane == idx, -jnp.inf, work)
    ex = [jnp.exp(v - vals[0]) for v in vals]
    inv = 1.0 / (ex[0] + ex[1] + ex[2] + ex[3])
    r = _row_iota((ROUTE_TILE, ROUTE_TILE))
    c = _lane_iota((ROUTE_TILE, ROUTE_TILE))
    earlier = (r > c).astype(BF16)
    base = base_ref[0:1, :]
    e_out = jnp.zeros((ROUTE_TILE, LANES), jnp.int32)
    g_out = jnp.zeros((ROUTE_TILE, LANES), F32)
    r_out = jnp.zeros((ROUTE_TILE, LANES), F32)
    for k in range(TOP_K):
        onehot = (lane == idxs[k]).astype(F32)
        within = jnp.dot(earlier, onehot.astype(BF16), preferred_element_type=F32)
        rank = jnp.sum((base + within) * onehot, axis=1, keepdims=True)
        base = base + jnp.sum(onehot, axis=0, keepdims=True)
        e_out = jnp.where(lane == k, idxs[k], e_out)
        g_out = jnp.where(lane == k, ex[k] * inv, g_out)
        r_out = jnp.where(lane == k, rank, r_out)
    base_ref[...] = _row_bcast(base)
    cnt_ref[...] = _row_bcast(base)
    e_ref[...] = e_out
    gt_ref[...] = g_out
    rk_ref[...] = r_out


def _route(logits):
    n_tok = logits.shape[0]
    tile = pl.BlockSpec((ROUTE_TILE, LANES), lambda i: (i, 0))
    return pl.pallas_call(
        _route_kernel,
        grid=(n_tok // ROUTE_TILE,),
        in_specs=[tile],
        out_specs=[tile, tile, tile, pl.BlockSpec((SUBLANES, LANES), lambda i: (0, 0))],
        out_shape=[jax.ShapeDtypeStruct((n_tok, LANES), jnp.int32),
                   jax.ShapeDtypeStruct((n_tok, LANES), F32),
                   jax.ShapeDtypeStruct((n_tok, LANES), F32),
                   jax.ShapeDtypeStruct((SUBLANES, LANES), F32)],
        scratch_shapes=[pltpu.VMEM((SUBLANES, LANES), F32)],
        compiler_params=pltpu.CompilerParams(dimension_semantics=("arbitrary",)),
        name="moe_route",
    )(logits)


DISPATCH_TILE = 512


def _row_slab(r):
    return pl.ds(pl.multiple_of(r * ROW_SLABS, ROW_SLABS), ROW_SLABS)


def _dispatch_kernel(dest_ref, h_hbm, xb_in, xb_hbm, sem):
    del xb_in
    tok0 = pl.program_id(0) * DISPATCH_TILE

    def copy(t, k):
        return pltpu.make_async_copy(h_hbm.at[_row_slab(tok0 + t)],
                                     xb_hbm.at[_row_slab(dest_ref[0, t * TOP_K + k])], sem)

    def issue(t, carry):
        for k in range(TOP_K):
            copy(t, k).start()
        return carry

    def drain(t, carry):
        for k in range(TOP_K):
            copy(t, k).wait()
        return carry

    lax.fori_loop(0, DISPATCH_TILE, issue, 0)
    lax.fori_loop(0, DISPATCH_TILE, drain, 0)


def _dispatch(dest, h2, xb):
    n_tiles = dest.shape[0] // (DISPATCH_TILE * TOP_K)
    return pl.pallas_call(
        _dispatch_kernel,
        grid=(n_tiles,),
        in_specs=[pl.BlockSpec((None, 1, DISPATCH_TILE * TOP_K), lambda i: (i, 0, 0), memory_space=pltpu.SMEM),
                  pl.BlockSpec(memory_space=pl.ANY), pl.BlockSpec(memory_space=pl.ANY)],
        out_specs=pl.BlockSpec(memory_space=pl.ANY),
        out_shape=jax.ShapeDtypeStruct(xb.shape, xb.dtype),
        scratch_shapes=[pltpu.SemaphoreType.DMA(())],
        input_output_aliases={2: 0},
        compiler_params=pltpu.CompilerParams(dimension_semantics=("arbitrary",)),
        name="moe_dispatch",
    )(dest.reshape(n_tiles, 1, DISPATCH_TILE * TOP_K), h2, xb)


def _expert_kernel(be_ref, nv_ref, x_ref, wgu_ref, bgu_ref, wdn_ref, bdn_ref, yb_in, y_ref,
                   wgu_bf, wdn_bf, act_ref):
    del yb_in
    i = pl.program_id(0)
    valid = i < nv_ref[0]
    fresh = jnp.logical_or(i == 0, be_ref[i] != be_ref[jnp.maximum(i - 1, 0)])

    @pl.when(jnp.logical_and(valid, fresh))
    def _():
        for r0 in range(0, D_MODEL, LANES):
            wgu_bf[r0:r0 + LANES, :] = wgu_ref[r0:r0 + LANES, :].astype(BF16)
            wdn_bf[r0:r0 + LANES, :] = wdn_ref[r0:r0 + LANES, :].astype(BF16)

    @pl.when(valid)
    def _():
        x = jnp.concatenate([x_ref[_slab(c, MOE_TILE), :] for c in range(ROW_SLABS)], axis=1).astype(BF16)
        half = D_EXPERT // 2
        for c0 in range(0, D_EXPERT, half):
            glu = jnp.dot(x, wgu_bf[:, c0:c0 + half], preferred_element_type=F32) + bgu_ref[:, c0:c0 + half]
            lin = (jnp.dot(x, wgu_bf[:, D_EXPERT + c0:D_EXPERT + c0 + half], preferred_element_type=F32)
                   + bgu_ref[:, D_EXPERT + c0:D_EXPERT + c0 + half])
            glu = jnp.minimum(glu, SWIGLU_LIMIT)
            lin = jnp.clip(lin, -SWIGLU_LIMIT, SWIGLU_LIMIT)
            act_ref[:, c0:c0 + half] = (glu * _sigmoid(SWIGLU_ALPHA * glu) * (lin + 1.0)).astype(BF16)
        y = jnp.dot(act_ref[...], wdn_bf[...], preferred_element_type=F32) + bdn_ref[...]
        for c in range(ROW_SLABS):
            y_ref[_slab(c, MOE_TILE), :] = y[:, c * LANES:(c + 1) * LANES]


def _experts(block_e, n_valid, xb, w_gu, b_gu, w_dn, b_dn, yb):
    n_blocks = block_e.shape[0]
    blk = lambda i, be, nv: jnp.minimum(i, nv[0] - 1)
    rows = pl.BlockSpec((MOE_TILE * ROW_SLABS, LANES), lambda i, be, nv: (blk(i, be, nv), 0))
    per_e = lambda r, c: pl.BlockSpec((None, r, c), lambda i, be, nv: (be[blk(i, be, nv)], 0, 0))
    return pl.pallas_call(
        _expert_kernel,
        grid_spec=pltpu.PrefetchScalarGridSpec(
            num_scalar_prefetch=2,
            grid=(n_blocks,),
            in_specs=[rows, per_e(D_MODEL, 2 * D_EXPERT), per_e(1, 2 * D_EXPERT),
                      per_e(D_EXPERT, D_MODEL), per_e(1, D_MODEL), pl.BlockSpec(memory_space=pl.ANY)],
            out_specs=rows,
            scratch_shapes=[pltpu.VMEM((D_MODEL, 2 * D_EXPERT), BF16), pltpu.VMEM((D_EXPERT, D_MODEL), BF16),
                            pltpu.VMEM((MOE_TILE, D_EXPERT), BF16)]),
        out_shape=jax.ShapeDtypeStruct(xb.shape, F32),
        input_output_aliases={7: 0},
        compiler_params=pltpu.CompilerParams(dimension_semantics=("arbitrary",), vmem_limit_bytes=VMEM_LIMIT),
        name="moe_experts",
    )(block_e, n_valid, xb, w_gu, b_gu.reshape(N_EXPERTS, 1, -1), w_dn, b_dn.reshape(N_EXPERTS, 1, -1), yb)


COMBINE_TILE = 128


def _combine_kernel(dest_ref, yb_hbm, x_ref, gt_ref, mg_ref, o_ref, buf_ref, sem):
    def copy(t, k):
        return pltpu.make_async_copy(yb_hbm.at[_row_slab(dest_ref[0, t * TOP_K + k])],
                                     buf_ref.at[_row_slab(k * COMBINE_TILE + t)], sem)

    def issue(t, carry):
        for k in range(TOP_K):
            copy(t, k).start()
        return carry

    def drain(t, carry):
        for k in range(TOP_K):
            copy(t, k).wait()
        return carry

    lax.fori_loop(0, COMBINE_TILE, issue, 0)
    lax.fori_loop(0, COMBINE_TILE, drain, 0)
    gates = gt_ref[...]
    acc = None
    for k in range(TOP_K):
        yk = jnp.concatenate([buf_ref[pl.ds(k * COMBINE_TILE * ROW_SLABS + c, COMBINE_TILE, stride=ROW_SLABS), :]
                              for c in range(ROW_SLABS)], axis=1)
        term = gates[:, k:k + 1] * yk
        acc = term if acc is None else acc + term
    o_ref[...] = x_ref[...] + mg_ref[...] * acc


def _combine(dest, yb, xa, gates, mod3, n_lat):
    nb, t, d = xa.shape
    nt = t // COMBINE_TILE
    n_lat_tiles = n_lat // COMBINE_TILE
    tok = pl.BlockSpec((None, COMBINE_TILE, d), lambda b, i: (b, i, 0))
    return pl.pallas_call(
        _combine_kernel,
        grid=(nb, nt),
        in_specs=[pl.BlockSpec((None, 1, COMBINE_TILE * TOP_K), lambda b, i: (b * nt + i, 0, 0),
                               memory_space=pltpu.SMEM),
                  pl.BlockSpec(memory_space=pl.ANY), tok,
                  pl.BlockSpec((COMBINE_TILE, LANES), lambda b, i: (b * nt + i, 0)),
                  _mod_spec(5, n_lat_tiles, nb)],
        out_specs=tok,
        out_shape=jax.ShapeDtypeStruct((nb, t, d), F32),
        scratch_shapes=[pltpu.VMEM((TOP_K * COMBINE_TILE * ROW_SLABS, LANES), F32),
                        pltpu.SemaphoreType.DMA(())],
        compiler_params=pltpu.CompilerParams(dimension_semantics=("arbitrary", "arbitrary"),
                                             vmem_limit_bytes=VMEM_LIMIT),
        name="moe_combine",
    )(dest.reshape(nb * nt, 1, COMBINE_TILE * TOP_K), yb, xa, gates, mod3)


def _final_norm_kernel(x_ref, g_ref, o_ref):
    o_ref[...] = _rms(x_ref[...]) * g_ref[...]


def _final_norm(xa, g, n_lat):
    nb, _, d = xa.shape
    tok = pl.BlockSpec((None, TOK_TILE, d), lambda b, i: (b, i, 0))
    return pl.pallas_call(
        _final_norm_kernel,
        grid=(nb, n_lat // TOK_TILE),
        in_specs=[tok, pl.BlockSpec((1, d), lambda b, i: (0, 0))],
        out_specs=tok,
        out_shape=jax.ShapeDtypeStruct((nb, n_lat, d), F32),
        compiler_params=pltpu.CompilerParams(dimension_semantics=("arbitrary", "arbitrary")),
        name="final_norm",
    )(xa, g.reshape(1, d))


def _pack_w_in(w_in):
    cols = jnp.asarray(np.maximum(_IN_COLS, 0))
    keep = jnp.asarray(_IN_COLS >= 0)
    return jnp.where(keep, jnp.take(w_in, cols, axis=-1), 0.0).astype(BF16)


def _pack_gla(w_gate2, b_gate):
    nl = w_gate2.shape[0]
    w = w_gate2.reshape(nl, 2, GLA_RANK, GLA_HEADS, GLA_DK).transpose(0, 3, 1, 2, 4)
    z = jnp.zeros((nl, GLA_HEADS, GLA_RANK, GLA_DK), F32)
    top = jnp.concatenate([w[:, :, 0], z], axis=-1)
    bot = jnp.concatenate([z, w[:, :, 1]], axis=-1)
    w2 = jnp.concatenate([top, bot, jnp.zeros((nl, GLA_HEADS, LANES - 2 * GLA_RANK, LANES), F32)], axis=2)
    b2 = b_gate.reshape(nl, 2, GLA_HEADS, GLA_DK).transpose(0, 2, 1, 3).reshape(nl, GLA_HEADS, 1, LANES)
    return w2, b2


def _pack_mlstm(conv_w, conv_b, b_i, b_f, norm_g):
    nl = conv_w.shape[0]
    cw = conv_w.reshape(nl, CONV_W, 2, ML_HEADS, ML_DQK).transpose(0, 3, 1, 2, 4).reshape(nl, ML_HEADS, CONV_W, LANES)
    cw = jnp.pad(cw, ((0, 0), (0, 0), (0, SUBLANES - CONV_W), (0, 0)))
    cb = conv_b.reshape(nl, 2, ML_HEADS, ML_DQK).transpose(0, 2, 1, 3).reshape(nl, ML_HEADS, 1, LANES)
    gates = jnp.stack([b_i[:, 0], b_i[:, 1], b_f[:, 0], b_f[:, 1]], axis=-1)
    gb = jnp.repeat(gates, ML_GATE_REP, axis=-1).reshape(nl, ML_HEADS, 1, LANES)
    return cw, cb, gb, norm_g.reshape(nl, ML_HEADS, 1, ML_DV)


def _pack_ssd(conv_w, conv_b, dt_bias, a_log, d_skip, norm_g):
    nl = conv_w.shape[0]
    bc = M2_GROUPS * M2_DSTATE

    def conv_cols(a):
        lead = a.shape[:-1]
        x = a[..., :M2_WIDTH].reshape(*lead, M2_GROUPS, M2_GROUP_X)
        b = a[..., M2_WIDTH:M2_WIDTH + bc].reshape(*lead, M2_GROUPS, M2_DSTATE)
        c = a[..., M2_WIDTH + bc:].reshape(*lead, M2_GROUPS, M2_DSTATE)
        return jnp.concatenate([x, b, c], axis=-1)

    cw = jnp.pad(conv_cols(conv_w).transpose(0, 2, 1, 3), ((0, 0), (0, 0), (0, SUBLANES - CONV_W), (0, 0)))
    cb = conv_cols(conv_b).reshape(nl, M2_GROUPS, 1, M2_CONV_COLS)

    def per_dir(a):
        a = a.reshape(nl, 2, M2_GROUPS, M2_GROUP_HEADS).transpose(0, 2, 1, 3)
        return jnp.repeat(a, M2_HEADDIM, axis=-1).reshape(nl, M2_GROUPS, 1, M2_DT_COLS)

    dsk = jnp.repeat(d_skip.reshape(nl, M2_GROUPS, M2_GROUP_HEADS), M2_HEADDIM, axis=-1)
    return (cw, cb, per_dir(dt_bias), per_dir(a_log), dsk.reshape(nl, M2_GROUPS, 1, M2_GROUP_X),
            norm_g.reshape(nl, M2_GROUPS, 1, M2_GROUP_X))


def _moe_plan(e_arr, rank_arr, counts_row, n_blocks):
    counts = counts_row[0, :N_EXPERTS].astype(jnp.int32)
    padded = (counts + MOE_TILE - 1) // MOE_TILE * MOE_TILE
    pad_end = jnp.cumsum(padded)
    pad_start = pad_end - padded
    e = e_arr[:, :TOP_K]
    dest = (jnp.take(pad_start, e) + rank_arr[:, :TOP_K].astype(jnp.int32)).reshape(-1)
    block_start = jnp.arange(n_blocks, dtype=jnp.int32) * MOE_TILE
    block_e = jnp.minimum(jnp.searchsorted(pad_end, block_start, side='right'), N_EXPERTS - 1).astype(jnp.int32)
    n_valid = (pad_end[-1:] // MOE_TILE).astype(jnp.int32)
    return dest, block_e, n_valid


def kernel(x, c, ctx, c_ctx, w_mod, b_mod, norm1_g, w_in, gla_w_gate2, gla_b_gate, gla_norm_g, ml_conv_w,
           ml_conv_b, ml_b_i, ml_b_f, ml_norm_g, m2_conv_w, m2_conv_b, m2_dt_bias, m2_A_log, m2_D, m2_norm_g,
           w_out, norm2_g, router_w, router_b, moe_w_gu, moe_b_gu, moe_w_dn, moe_b_dn, final_norm_g):
    nb, n_lat, d = x.shape
    n_ctx = ctx.shape[1]
    n_layers = w_mod.shape[0]
    t = n_lat + n_ctx
    assert d == D_MODEL and n_lat % (GRID_W * SUBLANES) == 0 and n_lat % TOK_TILE == 0 and n_ctx % TOK_TILE == 0
    assert (nb * t) % DISPATCH_TILE == 0

    mod_rows = -(-(nb + 1) // SUBLANES) * SUBLANES
    c_rows = jnp.concatenate([c, c_ctx[None], jnp.zeros((mod_rows - nb - 1, d), F32)], axis=0)
    mod = _mod_table(c_rows, w_mod, b_mod).reshape(n_layers, mod_rows * 6, 1, d)

    w_in_p = _pack_w_in(w_in)
    w_out_p = w_out.astype(BF16)
    gla_w2, gla_b2 = _pack_gla(gla_w_gate2, gla_b_gate)
    ml_cw, ml_cb, ml_gb, ml_ng = _pack_mlstm(ml_conv_w, ml_conv_b, ml_b_i, ml_b_f, ml_norm_g)
    m2_cw, m2_cb, m2_dtb, m2_alog, m2_dsk, m2_ng = _pack_ssd(m2_conv_w, m2_conv_b, m2_dt_bias, m2_A_log, m2_D,
                                                            m2_norm_g)
    rw = jnp.pad(router_w, ((0, 0), (0, 0), (0, LANES - N_EXPERTS)))
    rb = jnp.pad(router_b, ((0, 0), (0, LANES - N_EXPERTS)), constant_values=M_INIT).reshape(n_layers, 1, LANES)

    n_assign = nb * t * TOP_K
    n_blocks = n_assign // MOE_TILE + N_EXPERTS
    xb = jnp.zeros((n_blocks * MOE_TILE * ROW_SLABS, LANES), F32)
    yb = jnp.zeros_like(xb)

    xa = jnp.concatenate([x, ctx], axis=1)
    for l in range(n_layers):
        u_gla, u_ml, u_m2 = _in_proj(xa, mod[l], norm1_g[l], w_in_p[l], n_lat)
        o_gla = _gla_mixer(u_gla, gla_w2[l], gla_b2[l], gla_norm_g[l].reshape(1, GLA_DV), n_lat)
        o_ml = _mlstm_mixer(u_ml, ml_cw[l], ml_cb[l], ml_gb[l], ml_ng[l], n_lat)
        o_m2 = _ssd_mixer(u_m2, m2_cw[l], m2_cb[l], m2_dtb[l], m2_alog[l], m2_dsk[l], m2_ng[l], n_lat)
        xa, h2, logits = _out_proj(xa, o_gla, o_ml, o_m2, w_out_p[l], mod[l], norm2_g[l], rw[l], rb[l], n_lat)
        e_arr, gates, rank_arr, counts = _route(logits)
        dest, block_e, n_valid = _moe_plan(e_arr, rank_arr, counts, n_blocks)
        xb = _dispatch(dest, h2, xb)
        yb = _experts(block_e, n_valid, xb, moe_w_gu[l], moe_b_gu[l], moe_w_dn[l], moe_b_dn[l], yb)
        xa = _combine(dest, yb, xa, gates, mod[l], n_lat)
    return _final_norm(xa, final_norm_g, n_lat)
```

```python
import functools
import math

import numpy as np
import jax
import jax.numpy as jnp
from jax import lax
from jax.experimental import pallas as pl
from jax.experimental.pallas import tpu as pltpu

F32 = jnp.float32
BF16 = jnp.bfloat16

D_MODEL = 1024
GRID_W = 64
GLA_HEADS, GLA_DK, GLA_DV, GLA_RANK = 4, 64, 128, 16
GLA_NORMALIZER = 16.0
ML_HEADS, ML_DQK, ML_DV = 4, 64, 128
M2_HEADS, M2_HEADDIM, M2_GROUPS, M2_DSTATE = 8, 64, 2, 128
CONV_W = 7
CONV_R = CONV_W // 2
N_EXPERTS, TOP_K, D_EXPERT = 32, 4, 1024
SWIGLU_LIMIT, SWIGLU_ALPHA = 7.0, 1.702
EPS = 1e-6
M_INIT = -1e30

GLA_WIDTH = GLA_HEADS * GLA_DV
ML_WIDTH = ML_HEADS * ML_DV
M2_WIDTH = M2_HEADS * M2_HEADDIM
MIX_WIDTH = GLA_WIDTH + ML_WIDTH + M2_WIDTH
GLA_IN = 2 * GLA_HEADS * GLA_DK + 2 * GLA_WIDTH + 2 * GLA_RANK
ML_IN = 2 * ML_HEADS * ML_DQK + 2 * ML_WIDTH + 4 * ML_HEADS
M2_CONV_DIM = M2_WIDTH + 2 * M2_GROUPS * M2_DSTATE
M2_IN = M2_WIDTH + M2_CONV_DIM + 2 * M2_HEADS
IN_WIDTH = GLA_IN + ML_IN + M2_IN

LANES = 128
SUBLANES = 8
CHUNK = 128
TOK_TILE = 256
MOE_TILE = 256
VMEM_LIMIT = 52 * 1024 * 1024

GLA_HEAD_COLS = 3 * LANES
GLA_PACK = GLA_HEADS * GLA_HEAD_COLS + LANES
ML_HEAD_COLS = 4 * LANES
ML_PACK = ML_HEADS * ML_HEAD_COLS
M2_GROUP_HEADS = M2_HEADS // M2_GROUPS
M2_GROUP_X = M2_GROUP_HEADS * M2_HEADDIM
M2_PAIRS = M2_GROUP_HEADS // 2
M2_DT_COLS = 2 * M2_PAIRS * LANES
M2_GROUP_COLS = 2 * M2_GROUP_X + 2 * M2_DSTATE + M2_DT_COLS
M2_PACK = M2_GROUPS * M2_GROUP_COLS
IN_PACK = GLA_PACK + ML_PACK + M2_PACK
ML_GATE_REP = LANES // 4


def _in_proj_column_map():
    cols = []
    qk = GLA_HEADS * GLA_DK
    for h in range(GLA_HEADS):
        cols += list(range(h * GLA_DK, (h + 1) * GLA_DK))
        cols += list(range(qk + h * GLA_DK, qk + (h + 1) * GLA_DK))
        cols += list(range(2 * qk + h * GLA_DV, 2 * qk + (h + 1) * GLA_DV))
        cols += list(range(2 * qk + GLA_WIDTH + h * GLA_DV, 2 * qk + GLA_WIDTH + (h + 1) * GLA_DV))
    lr0 = 2 * qk + 2 * GLA_WIDTH
    cols += list(range(lr0, lr0 + 2 * GLA_RANK)) + [-1] * (LANES - 2 * GLA_RANK)
    a0 = GLA_IN
    qk = ML_HEADS * ML_DQK
    g0 = a0 + 2 * qk + 2 * ML_WIDTH
    for h in range(ML_HEADS):
        cols += list(range(a0 + h * ML_DQK, a0 + (h + 1) * ML_DQK))
        cols += list(range(a0 + qk + h * ML_DQK, a0 + qk + (h + 1) * ML_DQK))
        cols += list(range(a0 + 2 * qk + h * ML_DV, a0 + 2 * qk + (h + 1) * ML_DV))
        cols += list(range(a0 + 2 * qk + ML_WIDTH + h * ML_DV, a0 + 2 * qk + ML_WIDTH + (h + 1) * ML_DV))
        for gate in range(4):
            cols += [g0 + gate * ML_HEADS + h] * ML_GATE_REP
    a1 = GLA_IN + ML_IN
    x0 = a1 + M2_WIDTH
    dt0 = a1 + M2_WIDTH + M2_CONV_DIM
    for g in range(M2_GROUPS):
        cols += list(range(a1 + g * M2_GROUP_X, a1 + (g + 1) * M2_GROUP_X))
        cols += list(range(x0 + g * M2_GROUP_X, x0 + (g + 1) * M2_GROUP_X))
        cols += list(range(x0 + M2_WIDTH + g * M2_DSTATE, x0 + M2_WIDTH + (g + 1) * M2_DSTATE))
        cols += list(range(x0 + M2_WIDTH + M2_GROUPS * M2_DSTATE + g * M2_DSTATE,
                           x0 + M2_WIDTH + M2_GROUPS * M2_DSTATE + (g + 1) * M2_DSTATE))
        for d in range(2):
            for h in range(M2_GROUP_HEADS):
                cols += [dt0 + d * M2_HEADS + g * M2_GROUP_HEADS + h] * M2_HEADDIM
    cols = np.asarray(cols, np.int32)
    assert cols.shape == (IN_PACK,)
    return cols


_IN_COLS = _in_proj_column_map()


def _bdot(a, b):
    return jnp.dot(a.astype(BF16), b.astype(BF16), preferred_element_type=F32)


def _bdot_nt(a, b):
    return lax.dot_general(a.astype(BF16), b.astype(BF16), (((1,), (1,)), ((), ())),
                           preferred_element_type=F32)


def _bdot_tn(a, b):
    return lax.dot_general(a.astype(BF16), b.astype(BF16), (((0,), (0,)), ((), ())),
                           preferred_element_type=F32)


def _split2(a):
    hi = a.astype(BF16)
    lo = (a - hi.astype(F32)).astype(BF16)
    return hi, lo


def _split3(a):
    hi = a.astype(BF16)
    r = a - hi.astype(F32)
    mid = r.astype(BF16)
    lo = (r - mid.astype(F32)).astype(BF16)
    return hi, mid, lo


def _hdot(a, b):
    ah, al = _split2(a)
    bh, bl = _split2(b)
    d = functools.partial(jnp.dot, preferred_element_type=F32)
    return d(ah, bh) + (d(ah, bl) + d(al, bh))


def _tri_dot(tri, a):
    hi, mid, lo = _split3(a)
    d = functools.partial(jnp.dot, preferred_element_type=F32)
    return d(tri, hi) + (d(tri, mid) + d(tri, lo))


def _rms(x):
    return x * lax.rsqrt(jnp.mean(x * x, axis=-1, keepdims=True) + EPS)


def _sigmoid(x):
    return 1.0 / (1.0 + jnp.exp(-x))


def _silu(x):
    return x * _sigmoid(x)


def _log_sigmoid(x):
    return jnp.minimum(x, 0.0) - jnp.log(1.0 + jnp.exp(-jnp.abs(x)))


def _softplus(x):
    return jnp.maximum(x, 0.0) + jnp.log(1.0 + jnp.exp(-jnp.abs(x)))


def _lane_iota(shape):
    return lax.broadcasted_iota(jnp.int32, shape, len(shape) - 1)


def _row_iota(shape):
    return lax.broadcasted_iota(jnp.int32, shape, len(shape) - 2)


def _lane_rep(x, lo, width):
    lane = _lane_iota(x.shape)
    y = jnp.where((lane >= lo) & (lane < lo + width), x, 0.0)
    w = width
    while w < LANES:
        y = y + pltpu.roll(y, w, axis=1)
        w *= 2
    return y


def _swap_halves(x):
    return pltpu.roll(x, LANES // 2, axis=1)


def _tri_incl(n):
    r = lax.broadcasted_iota(jnp.int32, (n, n), 0)
    c = lax.broadcasted_iota(jnp.int32, (n, n), 1)
    return (r >= c).astype(BF16)


def _mod_kernel(c_ref, w_ref, b_ref, o_ref):
    o_ref[...] = _hdot(_silu(c_ref[...]), w_ref[...]) + b_ref[...]


def _mod_table(c_rows, w_mod, b_mod):
    n_layers, d, n6 = w_mod.shape
    r = c_rows.shape[0]
    tn = 1536
    return pl.pallas_call(
        _mod_kernel,
        grid=(n_layers, n6 // tn),
        in_specs=[pl.BlockSpec((r, d), lambda l, j: (0, 0)),
                  pl.BlockSpec((None, d, tn), lambda l, j: (l, 0, j)),
                  pl.BlockSpec((None, 1, tn), lambda l, j: (l, 0, j))],
        out_specs=pl.BlockSpec((None, r, tn), lambda l, j: (l, 0, j)),
        out_shape=jax.ShapeDtypeStruct((n_layers, r, n6), F32),
        compiler_params=pltpu.CompilerParams(dimension_semantics=("arbitrary", "arbitrary"),
                                             vmem_limit_bytes=VMEM_LIMIT),
        name="adaln_mod",
    )(c_rows, w_mod, b_mod.reshape(n_layers, 1, n6))


def _inproj_kernel(*refs, n_lat_tiles, widths, two_sources):
    if two_sources:
        xl_ref, xc_ref, sh_ref, sc_ref, g_ref, w_ref = refs[:6]
        outs = refs[6:]
        x = jnp.where(pl.program_id(1) < n_lat_tiles, xl_ref[...], xc_ref[...])
    else:
        x_ref, sh_ref, sc_ref, g_ref, w_ref = refs[:5]
        outs = refs[5:]
        x = x_ref[...]
    h = _rms(x) * g_ref[...]
    h = (h * (1.0 + sc_ref[...]) + sh_ref[...]).astype(BF16)
    c0 = 0
    for o_ref, width in zip(outs, widths):
        o_ref[...] = jnp.dot(h, w_ref[:, c0:c0 + width], preferred_element_type=F32)
        c0 += width


def _mod_spec(which, n_lat_tiles, n_batch):
    def imap(b, t):
        row = jnp.where(t < n_lat_tiles, b, n_batch)
        return (row * 6 + which, 0, 0)
    return pl.BlockSpec((None, 1, D_MODEL), imap)


def _in_proj(xa, x_lat, mod3, norm_g, w_pack, widths, n_lat):
    nb, t, d = xa.shape
    n_lat_tiles = n_lat // TOK_TILE
    tok = lambda width: pl.BlockSpec((None, TOK_TILE, width), lambda b, i: (b, i, 0))
    if x_lat is None:
        x_specs, x_args = [tok(d)], (xa,)
    else:
        x_specs = [pl.BlockSpec((None, TOK_TILE, d), lambda b, i: (b, jnp.minimum(i, n_lat_tiles - 1), 0)),
                   pl.BlockSpec((None, TOK_TILE, d), lambda b, i: (b, jnp.maximum(i, n_lat_tiles), 0))]
        x_args = (x_lat, xa)
    return pl.pallas_call(
        functools.partial(_inproj_kernel, n_lat_tiles=n_lat_tiles, widths=widths, two_sources=x_lat is not None),
        grid=(nb, t // TOK_TILE),
        in_specs=x_specs + [_mod_spec(0, n_lat_tiles, nb), _mod_spec(1, n_lat_tiles, nb),
                            pl.BlockSpec((1, d), lambda b, i: (0, 0)),
                            pl.BlockSpec((d, sum(widths)), lambda b, i: (0, 0))],
        out_specs=[tok(w) for w in widths],
        out_shape=[jax.ShapeDtypeStruct((nb, t, w), F32) for w in widths],
        compiler_params=pltpu.CompilerParams(dimension_semantics=("arbitrary", "arbitrary"),
                                             vmem_limit_bytes=VMEM_LIMIT),
        name="in_proj",
    )(*x_args, mod3, mod3, norm_g.reshape(1, d), w_pack)


def _chunk_rows(c):
    return pl.ds(pl.multiple_of(c * CHUNK, CHUNK), CHUNK)


def _row_bcast(row):
    return jnp.broadcast_to(row, (SUBLANES, row.shape[-1]))


CHUNK_UNROLL = 2


def _for_chunks(n, body):
    assert n % CHUNK_UNROLL == 0

    def step(i, carry):
        for j in range(CHUNK_UNROLL):
            body(i * CHUNK_UNROLL + j)
        return carry

    lax.fori_loop(0, n // CHUNK_UNROLL, step, 0)


def _gla_kernel(u_ref, lr_ref, w2_ref, b2_ref, ng_ref, o_ref,
                oin_ref, q2_ref, ds_ref, a_ref, stf_ref, stb_ref, *, n_lat_chunks):
    n = u_ref.shape[0] // CHUNK
    tri = _tri_incl(CHUNK)
    fwd = _lane_iota((CHUNK, LANES)) < LANES // 2
    row = _row_iota((CHUNK, CHUNK))
    col = _lane_iota((CHUNK, CHUNK))
    w2 = w2_ref[...]
    b2 = b2_ref[...]

    def local(c):
        rows = _chunk_rows(c)
        qk = u_ref[rows, 0:LANES]
        v = u_ref[rows, LANES:2 * LANES]
        la = _log_sigmoid(_hdot(lr_ref[rows, :], w2) + b2) * (1.0 / GLA_NORMALIZER)
        p = _tri_dot(tri, la)
        tot = p[CHUNK - 1:CHUNK, :]
        cum = jnp.where(fwd, p, tot - p + la)
        mid = cum[CHUNK // 2:CHUNK // 2 + 1, :]
        sw = _swap_halves(qk)
        qq = jnp.where(fwd, qk, sw) * (GLA_DK ** -0.5)
        kk = jnp.where(fwd, sw, qk)
        qe = qq * jnp.exp(cum - mid)
        ke = kk * jnp.exp(mid - cum)
        af = _bdot_nt(jnp.where(fwd, qe, 0.0), ke)
        ab = _bdot_nt(jnp.where(fwd, 0.0, qe), ke)
        attn = jnp.where(row >= col, af, 0.0) + jnp.where(col >= row, ab, 0.0)
        oin_ref[rows, :] = _bdot(attn, v)
        q2_ref[rows, :] = qq * jnp.exp(cum)
        ds_ref[c] = _bdot_tn(v, kk * jnp.exp(tot - cum))
        a_ref[c] = _row_bcast(jnp.exp(tot))

    _for_chunks(n, local)

    def scan(s, st):
        f = lax.rem(s + n_lat_chunks, n)
        g = n - 1 - s
        stf_ref[f] = st
        stb_ref[g] = st
        a = jnp.where(fwd[0:1], a_ref[f][0:1], a_ref[g][0:1])
        return st * a + jnp.where(fwd, ds_ref[f], ds_ref[g])

    lax.fori_loop(0, n, scan, jnp.zeros((GLA_DV, LANES), F32))

    def finish(c):
        rows = _chunk_rows(c)
        st = jnp.where(fwd, stf_ref[c], stb_ref[c])
        o = oin_ref[rows, :] + _bdot_nt(q2_ref[rows, :], st)
        o = _rms(o) * ng_ref[...]
        o_ref[rows, :] = (o * _silu(u_ref[rows, 2 * LANES:3 * LANES])).astype(o_ref.dtype)

    _for_chunks(n, finish)


def _gla_mixer(u, w2, b2, ng, n_lat):
    nb, t, _ = u.shape
    n = t // CHUNK
    return pl.pallas_call(
        functools.partial(_gla_kernel, n_lat_chunks=n_lat // CHUNK),
        grid=(nb, GLA_HEADS),
        in_specs=[pl.BlockSpec((None, t, GLA_HEAD_COLS), lambda b, h: (b, 0, h)),
                  pl.BlockSpec((None, t, LANES), lambda b, h: (b, 0, GLA_HEADS * GLA_HEAD_COLS // LANES)),
                  pl.BlockSpec((None, LANES, LANES), lambda b, h: (h, 0, 0)),
                  pl.BlockSpec((None, 1, LANES), lambda b, h: (h, 0, 0)),
                  pl.BlockSpec((1, LANES), lambda b, h: (0, 0))],
        out_specs=pl.BlockSpec((None, t, GLA_DV), lambda b, h: (b, 0, h)),
        out_shape=jax.ShapeDtypeStruct((nb, t, GLA_WIDTH), BF16),
        scratch_shapes=[pltpu.VMEM((t, LANES), F32), pltpu.VMEM((t, LANES), F32),
                        pltpu.VMEM((n, GLA_DV, LANES), F32), pltpu.VMEM((n, SUBLANES, LANES), F32),
                        pltpu.VMEM((n, GLA_DV, LANES), F32), pltpu.VMEM((n, GLA_DV, LANES), F32)],
        compiler_params=pltpu.CompilerParams(dimension_semantics=("arbitrary", "arbitrary"),
                                             vmem_limit_bytes=VMEM_LIMIT),
        name="gla_mixer",
    )(u, u, w2, b2, ng)


def _zero_pads(pad_ref, n_lat, t):
    z = jnp.zeros((SUBLANES, pad_ref.shape[1]), F32)
    pad_ref[0:SUBLANES, :] = z
    pad_ref[SUBLANES + n_lat:2 * SUBLANES + n_lat, :] = z
    pad_ref[2 * SUBLANES + t:3 * SUBLANES + t, :] = z


def _pad_base(c, n_lat_chunks):
    return c * CHUNK + (SUBLANES if c < n_lat_chunks else 2 * SUBLANES)


def _conv_silu(pad_ref, w_ref, b_ref, out_ref, n, n_lat_chunks):
    width = pad_ref.shape[1]
    for c in range(n):
        base = _pad_base(c, n_lat_chunks)
        for l0 in range(0, width, LANES):
            acc = None
            for j in range(CONV_W):
                term = w_ref[j:j + 1, l0:l0 + LANES] * pad_ref[base + j - CONV_R:base + j - CONV_R + CHUNK,
                                                               l0:l0 + LANES]
                acc = term if acc is None else acc + term
            out_ref[c * CHUNK:(c + 1) * CHUNK, l0:l0 + LANES] = _silu(acc + b_ref[:, l0:l0 + LANES])


def _mlstm_kernel(u_ref, cw_ref, cb_ref, gb_ref, ng_ref, o_ref,
                  pad_ref, qk_ref, fc_ref, rc_ref, dc_ref, tot_ref, mloc_ref, stm_ref, *, n_lat_chunks):
    t = u_ref.shape[0]
    n = t // CHUNK
    n_lat = n_lat_chunks * CHUNK
    tri = _tri_incl(CHUNK)
    lane = _lane_iota((CHUNK, LANES))
    hi_half = lane >= LANES // 2
    row = _row_iota((CHUNK, CHUNK))
    col = _lane_iota((CHUNK, CHUNK))
    masks = (row >= col, col >= row)
    ones = jnp.ones((CHUNK, LANES), F32)

    _zero_pads(pad_ref, n_lat, t)
    pad_ref[SUBLANES:SUBLANES + n_lat, :] = u_ref[0:n_lat, 0:LANES]
    pad_ref[2 * SUBLANES + n_lat:2 * SUBLANES + t, :] = u_ref[n_lat:t, 0:LANES]
    _conv_silu(pad_ref, cw_ref, cb_ref, qk_ref, n, n_lat_chunks)

    def khat_of(qk):
        return jnp.where(hi_half, qk, 0.0) * (ML_DQK ** -0.5)

    def local(c):
        rows = _chunk_rows(c)
        g = u_ref[rows, 3 * LANES:4 * LANES] + gb_ref[...]
        g = jnp.where(hi_half, _log_sigmoid(g), g)
        p = _tri_dot(tri, g)
        tot = p[CHUNK - 1:CHUNK, :]
        f_dir = (_lane_rep(p, 2 * ML_GATE_REP, ML_GATE_REP),
                 _lane_rep(tot - p + g, 3 * ML_GATE_REP, ML_GATE_REP))
        i_dir = (_lane_rep(g, 0, ML_GATE_REP), _lane_rep(g, ML_GATE_REP, ML_GATE_REP))
        khat = khat_of(qk_ref[rows, :])
        vaug = jnp.concatenate([u_ref[rows, LANES:2 * LANES], ones], axis=1)
        for d in range(2):
            fc, ig = f_dir[d], i_dir[d]
            tt = fc[CHUNK - 1:CHUNK, :] if d == 0 else fc[0:1, :]
            gend = tt - fc + ig
            mloc = jnp.max(gend, axis=0, keepdims=True)
            dc_ref[d, c] = _bdot_tn(khat * jnp.exp(gend - mloc), vaug)
            tot_ref[d, c] = _row_bcast(tt)
            mloc_ref[d, c] = _row_bcast(mloc)
            fc_ref[d, rows, :] = fc
            rc_ref[d, rows, :] = ig - fc

    _for_chunks(n, local)

    def scan(s, carry):
        new = []
        for d, idx in ((0, lax.rem(s + n_lat_chunks, n)), (1, n - 1 - s)):
            cst, m = carry[d]
            inc = dc_ref[d, idx]
            dc_ref[d, idx] = cst
            stm_ref[d, idx] = _row_bcast(m)
            tt = tot_ref[d, idx][0:1]
            ml = mloc_ref[d, idx][0:1]
            m_new = jnp.maximum(tt + m, ml)
            a = jnp.exp(tt + m - m_new)[:, 0:1]
            sc = jnp.exp(ml - m_new)[:, 0:1]
            new.append((a * cst + sc * inc, m_new))
        return tuple(new)

    init = (jnp.zeros((LANES, 2 * LANES), F32), jnp.full((1, LANES), M_INIT, F32))
    lax.fori_loop(0, n, scan, (init, init))

    def finish(c):
        rows = _chunk_rows(c)
        qk = qk_ref[rows, :]
        qhat = jnp.where(hi_half, _swap_halves(qk), 0.0)
        s_qk = _bdot_nt(qhat, khat_of(qk))
        vaug = jnp.concatenate([u_ref[rows, LANES:2 * LANES], ones], axis=1)
        h = None
        for d in range(2):
            fc = fc_ref[d, rows, :]
            dlog = jnp.where(masks[d], fc + rc_ref[d, rows, :].T, -jnp.inf)
            inter = fc + stm_ref[d, c][0:1]
            m_row = jnp.maximum(inter, jnp.max(dlog, axis=1, keepdims=True))
            w_inter = jnp.exp(inter - m_row)
            nd = (_bdot(s_qk * jnp.exp(dlog - m_row), vaug)
                  + jnp.concatenate([w_inter, w_inter], axis=1) * _bdot(qhat, dc_ref[d, c]))
            hd = nd[:, 0:LANES] / jnp.maximum(jnp.abs(nd[:, LANES:]), jnp.exp(-m_row))
            h = hd if h is None else h + hd
        h = _rms(h) * ng_ref[...]
        o_ref[rows, :] = (h * _sigmoid(u_ref[rows, 2 * LANES:3 * LANES])).astype(o_ref.dtype)

    _for_chunks(n, finish)


def _mlstm_mixer(u, cw, cb, gb, ng, n_lat):
    nb, t, _ = u.shape
    n = t // CHUNK
    head = lambda rows: pl.BlockSpec((None, rows, LANES), lambda b, h: (h, 0, 0))
    return pl.pallas_call(
        functools.partial(_mlstm_kernel, n_lat_chunks=n_lat // CHUNK),
        grid=(nb, ML_HEADS),
        in_specs=[pl.BlockSpec((None, t, ML_HEAD_COLS), lambda b, h: (b, 0, h)),
                  head(SUBLANES), head(1), head(1), head(1)],
        out_specs=pl.BlockSpec((None, t, ML_DV), lambda b, h: (b, 0, h)),
        out_shape=jax.ShapeDtypeStruct((nb, t, ML_WIDTH), BF16),
        scratch_shapes=[pltpu.VMEM((t + 3 * SUBLANES, LANES), F32), pltpu.VMEM((t, LANES), F32),
                        pltpu.VMEM((2, t, LANES), F32), pltpu.VMEM((2, t, LANES), F32),
                        pltpu.VMEM((2, n, LANES, 2 * LANES), F32),
                        pltpu.VMEM((2, n, SUBLANES, LANES), F32), pltpu.VMEM((2, n, SUBLANES, LANES), F32),
                        pltpu.VMEM((2, n, SUBLANES, LANES), F32)],
        compiler_params=pltpu.CompilerParams(dimension_semantics=("arbitrary", "arbitrary"),
                                             vmem_limit_bytes=VMEM_LIMIT),
        name="mlstm_mixer",
    )(u, cw, cb, gb, ng)


M2_CONV_COLS = M2_GROUP_X + 2 * M2_DSTATE
M2_X0 = M2_GROUP_X
M2_DT0 = M2_X0 + M2_CONV_COLS


def _ssd_kernel(u_ref, cw_ref, cb_ref, dtb_ref, alog_ref, dsk_ref, ng_ref, o_ref,
                pad_ref, xc_ref, dt_ref, dh_ref, a_ref, y_ref, *, n_lat_chunks):
    t = u_ref.shape[0]
    n = t // CHUNK
    n_lat = n_lat_chunks * CHUNK
    n_state = 2 * M2_PAIRS
    tri = _tri_incl(CHUNK)
    lo_half = _lane_iota((CHUNK, LANES)) < LANES // 2
    row = _row_iota((CHUNK, CHUNK))
    col = _lane_iota((CHUNK, CHUNK))
    masks = (row >= col, col >= row)
    fwd_cols = _lane_iota((CHUNK, M2_DT_COLS)) < M2_DT_COLS // 2
    a_row = -jnp.exp(alog_ref[...])

    _zero_pads(pad_ref, n_lat, t)
    pad_ref[SUBLANES:SUBLANES + n_lat, :] = u_ref[0:n_lat, M2_X0:M2_DT0]
    pad_ref[2 * SUBLANES + n_lat:2 * SUBLANES + t, :] = u_ref[n_lat:t, M2_X0:M2_DT0]
    _conv_silu(pad_ref, cw_ref, cb_ref, xc_ref, n, n_lat_chunks)

    def decay_terms(rows):
        dt = dt_ref[rows, :]
        da = dt * a_row
        p = _tri_dot(tri, da)
        tot = p[CHUNK - 1:CHUNK, :]
        return dt, jnp.where(fwd_cols, p, tot - p + da), tot

    def local(c):
        rows = _chunk_rows(c)
        dt_ref[rows, :] = _softplus(u_ref[rows, M2_DT0:] + dtb_ref[...])
        dt, cum, tot = decay_terms(rows)
        x = xc_ref[rows, 0:M2_GROUP_X]
        bm = xc_ref[rows, M2_GROUP_X:M2_GROUP_X + M2_DSTATE]
        cm = xc_ref[rows, M2_GROUP_X + M2_DSTATE:]
        g = _bdot_nt(cm, bm)
        y = [None] * M2_PAIRS
        for d in range(2):
            for p in range(M2_PAIRS):
                k = d * M2_PAIRS + p
                sl = slice(k * LANES, (k + 1) * LANES)
                fp = cum[:, sl]
                tt = tot[:, sl]
                xdt = x[:, p * LANES:(p + 1) * LANES] * dt[:, sl]
                dh_ref[c, k] = _bdot_tn(bm, jnp.exp(tt - fp) * xdt)
                a_ref[c, k] = _row_bcast(jnp.exp(tt))
                sw = _swap_halves(fp)
                halves = []
                for fh in (jnp.where(lo_half, fp, sw), jnp.where(lo_half, sw, fp)):
                    dec = jnp.exp(jnp.where(masks[d], fh - fh.T, -jnp.inf))
                    halves.append(_bdot(g * dec, xdt))
                yp = jnp.where(lo_half, halves[0], halves[1])
                y[p] = yp if y[p] is None else y[p] + yp
        for p in range(M2_PAIRS):
            y_ref[p, rows, :] = y[p]

    _for_chunks(n, local)

    def scan(s, carry):
        f = lax.rem(s + n_lat_chunks, n)
        g = n - 1 - s
        new = []
        for k in range(n_state):
            idx = f if k < M2_PAIRS else g
            inc = dh_ref[idx, k]
            dh_ref[idx, k] = carry[k]
            new.append(carry[k] * a_ref[idx, k][0:1] + inc)
        return tuple(new)

    lax.fori_loop(0, n, scan, tuple(jnp.zeros((M2_DSTATE, LANES), F32) for _ in range(n_state)))

    def finish(c):
        rows = _chunk_rows(c)
        _, cum, _ = decay_terms(rows)
        cm = xc_ref[rows, M2_GROUP_X + M2_DSTATE:]
        y = [y_ref[p, rows, :] for p in range(M2_PAIRS)]
        for k in range(n_state):
            p = k % M2_PAIRS
            y[p] = y[p] + jnp.exp(cum[:, k * LANES:(k + 1) * LANES]) * _bdot(cm, dh_ref[c, k])
        y = jnp.concatenate(y, axis=1) + dsk_ref[...] * xc_ref[rows, 0:M2_GROUP_X]
        y = _rms(y * _silu(u_ref[rows, 0:M2_X0])) * ng_ref[...]
        o_ref[rows, :] = y.astype(o_ref.dtype)

    _for_chunks(n, finish)


def _ssd_mixer(u, cw, cb, dtb, alog, dsk, ng, n_lat):
    nb, t, _ = u.shape
    n = t // CHUNK
    grp = lambda rows, width: pl.BlockSpec((None, rows, width), lambda b, g: (g, 0, 0))
    return pl.pallas_call(
        functools.partial(_ssd_kernel, n_lat_chunks=n_lat // CHUNK),
        grid=(nb, M2_GROUPS),
        in_specs=[pl.BlockSpec((None, t, M2_GROUP_COLS), lambda b, g: (b, 0, g)),
                  grp(SUBLANES, M2_CONV_COLS), grp(1, M2_CONV_COLS), grp(1, M2_DT_COLS), grp(1, M2_DT_COLS),
                  grp(1, M2_GROUP_X), grp(1, M2_GROUP_X)],
        out_specs=pl.BlockSpec((None, t, M2_GROUP_X), lambda b, g: (b, 0, g)),
        out_shape=jax.ShapeDtypeStruct((nb, t, M2_WIDTH), BF16),
        scratch_shapes=[pltpu.VMEM((t + 3 * SUBLANES, M2_CONV_COLS), F32), pltpu.VMEM((t, M2_CONV_COLS), F32),
                        pltpu.VMEM((t, M2_DT_COLS), F32),
                        pltpu.VMEM((n, 2 * M2_PAIRS, M2_DSTATE, LANES), F32),
                        pltpu.VMEM((n, 2 * M2_PAIRS, SUBLANES, LANES), F32),
                        pltpu.VMEM((M2_PAIRS, t, LANES), F32)],
        compiler_params=pltpu.CompilerParams(dimension_semantics=("arbitrary", "arbitrary"),
                                             vmem_limit_bytes=VMEM_LIMIT),
        name="ssd_mixer",
    )(u, cw, cb, dtb, alog, dsk, ng)


ROW_SLABS = D_MODEL // LANES


def _slab(c, n_rows):
    return pl.ds(c, n_rows, stride=ROW_SLABS)


def _outproj_kernel(x_ref, og_ref, om_ref, osl_ref, osc_ref, w_ref, gate_ref, sh_ref, sc_ref, g2_ref, rw_ref,
                    rb_ref, xo_ref, h2_ref, lg_ref, *, n_lat_tiles):
    o_ssd = jnp.where(pl.program_id(1) < n_lat_tiles, osl_ref[...], osc_ref[...])
    mix = (jnp.dot(og_ref[...], w_ref[0:GLA_WIDTH, :], preferred_element_type=F32)
           + jnp.dot(om_ref[...], w_ref[GLA_WIDTH:GLA_WIDTH + ML_WIDTH, :], preferred_element_type=F32)
           + jnp.dot(o_ssd, w_ref[GLA_WIDTH + ML_WIDTH:, :], preferred_element_type=F32))
    x = x_ref[...] + gate_ref[...] * mix
    xo_ref[...] = x
    h2 = (_rms(x) * g2_ref[...]) * (1.0 + sc_ref[...]) + sh_ref[...]
    for c in range(ROW_SLABS):
        h2_ref[_slab(c, TOK_TILE), :] = h2[:, c * LANES:(c + 1) * LANES]
    lg_ref[...] = _hdot(h2, rw_ref[...]) + rb_ref[...]


def _out_proj(xa, o_gla, o_ml, o_m2_lat, o_m2, w_out, mod3, norm_g, rw, rb, n_lat):
    nb, t, d = xa.shape
    nt = t // TOK_TILE
    n_lat_tiles = n_lat // TOK_TILE
    tok = lambda width: pl.BlockSpec((None, TOK_TILE, width), lambda b, i: (b, i, 0))
    const = lambda r, c: pl.BlockSpec((r, c), lambda b, i: (0, 0))
    mod = lambda which: _mod_spec(which, n_lat_tiles, nb)
    ssd_lat = pl.BlockSpec((None, TOK_TILE, M2_WIDTH), lambda b, i: (b, jnp.minimum(i, n_lat_tiles - 1), 0))
    ssd_ctx = pl.BlockSpec((None, TOK_TILE, M2_WIDTH), lambda b, i: (b, jnp.maximum(i, n_lat_tiles), 0))
    return pl.pallas_call(
        functools.partial(_outproj_kernel, n_lat_tiles=n_lat_tiles),
        grid=(nb, nt),
        in_specs=[tok(d), tok(GLA_WIDTH), tok(ML_WIDTH), ssd_lat, ssd_ctx, const(MIX_WIDTH, d),
                  mod(2), mod(3), mod(4), const(1, d), const(d, LANES), const(1, LANES)],
        out_specs=[tok(d),
                   pl.BlockSpec((TOK_TILE * ROW_SLABS, LANES), lambda b, i: (b * nt + i, 0)),
                   pl.BlockSpec((TOK_TILE, LANES), lambda b, i: (b * nt + i, 0))],
        out_shape=[jax.ShapeDtypeStruct((nb, t, d), F32),
                   jax.ShapeDtypeStruct((nb * t * ROW_SLABS, LANES), F32),
                   jax.ShapeDtypeStruct((nb * t, LANES), F32)],
        compiler_params=pltpu.CompilerParams(dimension_semantics=("arbitrary", "arbitrary"),
                                             vmem_limit_bytes=VMEM_LIMIT),
        name="out_proj",
    )(xa, o_gla, o_ml, o_m2_lat, o_m2, w_out, mod3, mod3, mod3, norm_g.reshape(1, d), rw, rb)


ROUTE_TILE = 256


def _route_kernel(lg_ref, e_ref, gt_ref, rk_ref, cnt_ref, base_ref):
    @pl.when(pl.program_id(0) == 0)
    def _():
        base_ref[...] = jnp.zeros_like(base_ref)

    lane = _lane_iota((ROUTE_TILE, LANES))
    work = lg_ref[...]
    vals, idxs = [], []
    for _ in range(TOP_K):
        m = jnp.max(work, axis=1, keepdims=True)
        idx = jnp.min(jnp.where(work == m, lane, LANES), axis=1, keepdims=True)
        vals.append(m)
        idxs.append(idx)
        work = jnp.where(lane == idx, -jnp.inf, work)
    ex = [jnp.exp(v - vals[0]) for v in vals]
    inv = 1.0 / (ex[0] + ex[1] + ex[2] + ex[3])
    r = _row_iota((ROUTE_TILE, ROUTE_TILE))
    c = _lane_iota((ROUTE_TILE, ROUTE_TILE))
    earlier = (r > c).astype(BF16)
    base = base_ref[0:1, :]
    e_out = jnp.zeros((ROUTE_TILE, LANES), jnp.int32)
    g_out = jnp.zeros((ROUTE_TILE, LANES), F32)
    r_out = jnp.zeros((ROUTE_TILE, LANES), F32)
    for k in range(TOP_K):
        onehot = (lane == idxs[k]).astype(F32)
        within = jnp.dot(earlier, onehot.astype(BF16), preferred_element_type=F32)
        rank = jnp.sum((base + within) * onehot, axis=1, keepdims=True)
        base = base + jnp.sum(onehot, axis=0, keepdims=True)
        e_out = jnp.where(lane == k, idxs[k], e_out)
        g_out = jnp.where(lane == k, ex[k] * inv, g_out)
        r_out = jnp.where(lane == k, rank, r_out)
    base_ref[...] = _row_bcast(base)
    cnt_ref[...] = _row_bcast(base)
    e_ref[...] = e_out
    gt_ref[...] = g_out
    rk_ref[...] = r_out


def _route(logits):
    n_tok = logits.shape[0]
    tile = pl.BlockSpec((ROUTE_TILE, LANES), lambda i: (i, 0))
    return pl.pallas_call(
        _route_kernel,
        grid=(n_tok // ROUTE_TILE,),
        in_specs=[tile],
        out_specs=[tile, tile, tile, pl.BlockSpec((SUBLANES, LANES), lambda i: (0, 0))],
        out_shape=[jax.ShapeDtypeStruct((n_tok, LANES), jnp.int32),
                   jax.ShapeDtypeStruct((n_tok, LANES), F32),
                   jax.ShapeDtypeStruct((n_tok, LANES), F32),
                   jax.ShapeDtypeStruct((SUBLANES, LANES), F32)],
        scratch_shapes=[pltpu.VMEM((SUBLANES, LANES), F32)],
        compiler_params=pltpu.CompilerParams(dimension_semantics=("arbitrary",)),
        name="moe_route",
    )(logits)


DISPATCH_TILE = 512


def _row_slab(r):
    return pl.ds(pl.multiple_of(r * ROW_SLABS, ROW_SLABS), ROW_SLABS)


def _dispatch_kernel(dest_ref, h_ref, xb_in, xb_hbm, sem):
    del xb_in

    def copy(t, k):
        return pltpu.make_async_copy(h_ref.at[_row_slab(t)],
                                     xb_hbm.at[_row_slab(dest_ref[0, t * TOP_K + k])], sem)

    def issue(t, carry):
        for k in range(TOP_K):
            copy(t, k).start()
        return carry

    def drain(t, carry):
        for k in range(TOP_K):
            copy(t, k).wait()
        return carry

    lax.fori_loop(0, DISPATCH_TILE, issue, 0)
    lax.fori_loop(0, DISPATCH_TILE, drain, 0)


def _dispatch(dest, h2, xb):
    n_tiles = dest.shape[0] // (DISPATCH_TILE * TOP_K)
    return pl.pallas_call(
        _dispatch_kernel,
        grid=(n_tiles,),
        in_specs=[pl.BlockSpec((None, 1, DISPATCH_TILE * TOP_K), lambda i: (i, 0, 0), memory_space=pltpu.SMEM),
                  pl.BlockSpec((DISPATCH_TILE * ROW_SLABS, LANES), lambda i: (i, 0)),
                  pl.BlockSpec(memory_space=pl.ANY)],
        out_specs=pl.BlockSpec(memory_space=pl.ANY),
        out_shape=jax.ShapeDtypeStruct(xb.shape, xb.dtype),
        scratch_shapes=[pltpu.SemaphoreType.DMA(())],
        input_output_aliases={2: 0},
        compiler_params=pltpu.CompilerParams(dimension_semantics=("arbitrary",)),
        name="moe_dispatch",
    )(dest.reshape(n_tiles, 1, DISPATCH_TILE * TOP_K), h2, xb)


def _expert_kernel(be_ref, nv_ref, x_ref, wgu_ref, bgu_ref, wdn_ref, bdn_ref, yb_in, y_ref,
                   wgu_bf, wdn_bf, act_ref):
    del yb_in
    i = pl.program_id(0)
    valid = i < nv_ref[0]
    fresh = jnp.logical_or(i == 0, be_ref[i] != be_ref[jnp.maximum(i - 1, 0)])

    @pl.when(jnp.logical_and(valid, fresh))
    def _():
        for r0 in range(0, D_MODEL, LANES):
            wgu_bf[r0:r0 + LANES, :] = wgu_ref[r0:r0 + LANES, :].astype(BF16)
            wdn_bf[r0:r0 + LANES, :] = wdn_ref[r0:r0 + LANES, :].astype(BF16)

    @pl.when(valid)
    def _():
        x = jnp.concatenate([x_ref[_slab(c, MOE_TILE), :] for c in range(ROW_SLABS)], axis=1).astype(BF16)
        half = D_EXPERT // 2
        for c0 in range(0, D_EXPERT, half):
            glu = jnp.dot(x, wgu_bf[:, c0:c0 + half], preferred_element_type=F32) + bgu_ref[:, c0:c0 + half]
            lin = (jnp.dot(x, wgu_bf[:, D_EXPERT + c0:D_EXPERT + c0 + half], preferred_element_type=F32)
                   + bgu_ref[:, D_EXPERT + c0:D_EXPERT + c0 + half])
            glu = jnp.minimum(glu, SWIGLU_LIMIT)
            lin = jnp.clip(lin, -SWIGLU_LIMIT, SWIGLU_LIMIT)
            act_ref[:, c0:c0 + half] = (glu * _sigmoid(SWIGLU_ALPHA * glu) * (lin + 1.0)).astype(BF16)
        y = jnp.dot(act_ref[...], wdn_bf[...], preferred_element_type=F32) + bdn_ref[...]
        for c in range(ROW_SLABS):
            y_ref[_slab(c, MOE_TILE), :] = y[:, c * LANES:(c + 1) * LANES]


def _experts(block_e, n_valid, xb, w_gu, b_gu, w_dn, b_dn, yb):
    n_blocks = block_e.shape[0]
    blk = lambda i, be, nv: jnp.minimum(i, nv[0] - 1)
    rows = pl.BlockSpec((MOE_TILE * ROW_SLABS, LANES), lambda i, be, nv: (blk(i, be, nv), 0))
    per_e = lambda r, c: pl.BlockSpec((None, r, c), lambda i, be, nv: (be[blk(i, be, nv)], 0, 0))
    return pl.pallas_call(
        _expert_kernel,
        grid_spec=pltpu.PrefetchScalarGridSpec(
            num_scalar_prefetch=2,
            grid=(n_blocks,),
            in_specs=[rows, per_e(D_MODEL, 2 * D_EXPERT), per_e(1, 2 * D_EXPERT),
                      per_e(D_EXPERT, D_MODEL), per_e(1, D_MODEL), pl.BlockSpec(memory_space=pl.ANY)],
            out_specs=rows,
            scratch_shapes=[pltpu.VMEM((D_MODEL, 2 * D_EXPERT), BF16), pltpu.VMEM((D_EXPERT, D_MODEL), BF16),
                            pltpu.VMEM((MOE_TILE, D_EXPERT), BF16)]),
        out_shape=jax.ShapeDtypeStruct(xb.shape, F32),
        input_output_aliases={7: 0},
        compiler_params=pltpu.CompilerParams(dimension_semantics=("arbitrary",), vmem_limit_bytes=VMEM_LIMIT),
        name="moe_experts",
    )(block_e, n_valid, xb, w_gu, b_gu.reshape(N_EXPERTS, 1, -1), w_dn, b_dn.reshape(N_EXPERTS, 1, -1), yb)


COMBINE_TILE = 128


def _combine_kernel(dest_ref, yb_hbm, x_ref, gt_ref, mg_ref, o_ref, buf_ref, sem):
    def copy(t, k):
        return pltpu.make_async_copy(yb_hbm.at[_row_slab(dest_ref[0, t * TOP_K + k])],
                                     buf_ref.at[_row_slab(k * COMBINE_TILE + t)], sem)

    def issue(t, carry):
        for k in range(TOP_K):
            copy(t, k).start()
        return carry

    def drain(t, carry):
        for k in range(TOP_K):
            copy(t, k).wait()
        return carry

    lax.fori_loop(0, COMBINE_TILE, issue, 0)
    lax.fori_loop(0, COMBINE_TILE, drain, 0)
    gates = gt_ref[...]
    acc = None
    for k in range(TOP_K):
        yk = jnp.concatenate([buf_ref[pl.ds(k * COMBINE_TILE * ROW_SLABS + c, COMBINE_TILE, stride=ROW_SLABS), :]
                              for c in range(ROW_SLABS)], axis=1)
        term = gates[:, k:k + 1] * yk
        acc = term if acc is None else acc + term
    o_ref[...] = x_ref[...] + mg_ref[...] * acc


def _combine(dest, yb, xa, gates, mod3, n_lat):
    nb, t, d = xa.shape
    nt = t // COMBINE_TILE
    n_lat_tiles = n_lat // COMBINE_TILE
    tok = pl.BlockSpec((None, COMBINE_TILE, d), lambda b, i: (b, i, 0))
    return pl.pallas_call(
        _combine_kernel,
        grid=(nb, nt),
        in_specs=[pl.BlockSpec((None, 1, COMBINE_TILE * TOP_K), lambda b, i: (b * nt + i, 0, 0),
                               memory_space=pltpu.SMEM),
                  pl.BlockSpec(memory_space=pl.ANY), tok,
                  pl.BlockSpec((COMBINE_TILE, LANES), lambda b, i: (b * nt + i, 0)),
                  _mod_spec(5, n_lat_tiles, nb)],
        out_specs=tok,
        out_shape=jax.ShapeDtypeStruct((nb, t, d), F32),
        scratch_shapes=[pltpu.VMEM((TOP_K * COMBINE_TILE * ROW_SLABS, LANES), F32),
                        pltpu.SemaphoreType.DMA(())],
        compiler_params=pltpu.CompilerParams(dimension_semantics=("arbitrary", "arbitrary"),
                                             vmem_limit_bytes=VMEM_LIMIT),
        name="moe_combine",
    )(dest.reshape(nb * nt, 1, COMBINE_TILE * TOP_K), yb, xa, gates, mod3)


def _final_norm_kernel(x_ref, g_ref, o_ref):
    o_ref[...] = _rms(x_ref[...]) * g_ref[...]


def _final_norm(xa, g, n_lat):
    nb, _, d = xa.shape
    tok = pl.BlockSpec((None, TOK_TILE, d), lambda b, i: (b, i, 0))
    return pl.pallas_call(
        _final_norm_kernel,
        grid=(nb, n_lat // TOK_TILE),
        in_specs=[tok, pl.BlockSpec((1, d), lambda b, i: (0, 0))],
        out_specs=tok,
        out_shape=jax.ShapeDtypeStruct((nb, n_lat, d), F32),
        compiler_params=pltpu.CompilerParams(dimension_semantics=("arbitrary", "arbitrary")),
        name="final_norm",
    )(xa, g.reshape(1, d))


def _column_runs(cols):
    runs = []
    i = 0
    while i < len(cols):
        c = int(cols[i])
        j = i + 1
        if c < 0:
            while j < len(cols) and cols[j] < 0:
                j += 1
            runs.append(("zero", 0, j - i))
        elif j < len(cols) and cols[j] == c:
            while j < len(cols) and cols[j] == c:
                j += 1
            runs.append(("rep", c, j - i))
        else:
            while j < len(cols) and cols[j] == cols[j - 1] + 1:
                j += 1
            runs.append(("range", c, j - i))
        i = j
    return runs


def _pack_w_in(w_in, cols):
    lead = w_in.shape[:-1]
    parts = []
    for kind, c, n in _column_runs(cols):
        if kind == "zero":
            parts.append(jnp.zeros(lead + (n,), BF16))
        elif kind == "rep":
            parts.append(jnp.broadcast_to(w_in[..., c:c + 1].astype(BF16), lead + (n,)))
        else:
            parts.append(w_in[..., c:c + n].astype(BF16))
    return jnp.concatenate(parts, axis=-1)


def _pack_gla(w_gate2, b_gate):
    nl = w_gate2.shape[0]
    w = w_gate2.reshape(nl, 2, GLA_RANK, GLA_HEADS, GLA_DK).transpose(0, 3, 1, 2, 4)
    z = jnp.zeros((nl, GLA_HEADS, GLA_RANK, GLA_DK), F32)
    top = jnp.concatenate([w[:, :, 0], z], axis=-1)
    bot = jnp.concatenate([z, w[:, :, 1]], axis=-1)
    w2 = jnp.concatenate([top, bot, jnp.zeros((nl, GLA_HEADS, LANES - 2 * GLA_RANK, LANES), F32)], axis=2)
    b2 = b_gate.reshape(nl, 2, GLA_HEADS, GLA_DK).transpose(0, 2, 1, 3).reshape(nl, GLA_HEADS, 1, LANES)
    return w2, b2


def _pack_mlstm(conv_w, conv_b, b_i, b_f, norm_g):
    nl = conv_w.shape[0]
    cw = conv_w.reshape(nl, CONV_W, 2, ML_HEADS, ML_DQK).transpose(0, 3, 1, 2, 4).reshape(nl, ML_HEADS, CONV_W, LANES)
    cw = jnp.pad(cw, ((0, 0), (0, 0), (0, SUBLANES - CONV_W), (0, 0)))
    cb = conv_b.reshape(nl, 2, ML_HEADS, ML_DQK).transpose(0, 2, 1, 3).reshape(nl, ML_HEADS, 1, LANES)
    gates = jnp.stack([b_i[:, 0], b_i[:, 1], b_f[:, 0], b_f[:, 1]], axis=-1)
    gb = jnp.repeat(gates, ML_GATE_REP, axis=-1).reshape(nl, ML_HEADS, 1, LANES)
    return cw, cb, gb, norm_g.reshape(nl, ML_HEADS, 1, ML_DV)


def _pack_ssd(conv_w, conv_b, dt_bias, a_log, d_skip, norm_g):
    nl = conv_w.shape[0]
    bc = M2_GROUPS * M2_DSTATE

    def conv_cols(a):
        lead = a.shape[:-1]
        x = a[..., :M2_WIDTH].reshape(*lead, M2_GROUPS, M2_GROUP_X)
        b = a[..., M2_WIDTH:M2_WIDTH + bc].reshape(*lead, M2_GROUPS, M2_DSTATE)
        c = a[..., M2_WIDTH + bc:].reshape(*lead, M2_GROUPS, M2_DSTATE)
        return jnp.concatenate([x, b, c], axis=-1)

    cw = jnp.pad(conv_cols(conv_w).transpose(0, 2, 1, 3), ((0, 0), (0, 0), (0, SUBLANES - CONV_W), (0, 0)))
    cb = conv_cols(conv_b).reshape(nl, M2_GROUPS, 1, M2_CONV_COLS)

    def per_dir(a):
        a = a.reshape(nl, 2, M2_GROUPS, M2_GROUP_HEADS).transpose(0, 2, 1, 3)
        return jnp.repeat(a, M2_HEADDIM, axis=-1).reshape(nl, M2_GROUPS, 1, M2_DT_COLS)

    dsk = jnp.repeat(d_skip.reshape(nl, M2_GROUPS, M2_GROUP_HEADS), M2_HEADDIM, axis=-1)
    return (cw, cb, per_dir(dt_bias), per_dir(a_log), dsk.reshape(nl, M2_GROUPS, 1, M2_GROUP_X),
            norm_g.reshape(nl, M2_GROUPS, 1, M2_GROUP_X))


def _moe_plan(e_arr, rank_arr, counts_row, n_blocks):
    counts = counts_row[0, :N_EXPERTS].astype(jnp.int32)
    padded = (counts + MOE_TILE - 1) // MOE_TILE * MOE_TILE
    pad_end = jnp.cumsum(padded)
    pad_start = pad_end - padded
    e = e_arr[:, :TOP_K]
    dest = (jnp.take(pad_start, e) + rank_arr[:, :TOP_K].astype(jnp.int32)).reshape(-1)
    block_start = jnp.arange(n_blocks, dtype=jnp.int32) * MOE_TILE
    block_e = jnp.sum((pad_end[None, :] <= block_start[:, None]).astype(jnp.int32), axis=1)
    block_e = jnp.minimum(block_e, N_EXPERTS - 1)
    n_valid = (pad_end[-1:] // MOE_TILE).astype(jnp.int32)
    return dest, block_e, n_valid


def kernel(x, c, ctx, c_ctx, w_mod, b_mod, norm1_g, w_in, gla_w_gate2, gla_b_gate, gla_norm_g, ml_conv_w,
           ml_conv_b, ml_b_i, ml_b_f, ml_norm_g, m2_conv_w, m2_conv_b, m2_dt_bias, m2_A_log, m2_D, m2_norm_g,
           w_out, norm2_g, router_w, router_b, moe_w_gu, moe_b_gu, moe_w_dn, moe_b_dn, final_norm_g):
    nb, n_lat, d = x.shape
    n_ctx = ctx.shape[1]
    n_layers = w_mod.shape[0]
    t = n_lat + n_ctx
    assert d == D_MODEL and n_lat % (GRID_W * SUBLANES) == 0 and n_lat % TOK_TILE == 0 and n_ctx % TOK_TILE == 0
    assert (nb * t) % DISPATCH_TILE == 0

    mod_rows = -(-(nb + 1) // SUBLANES) * SUBLANES
    c_rows = jnp.concatenate([c, c_ctx[None], jnp.zeros((mod_rows - nb - 1, d), F32)], axis=0)
    mod = _mod_table(c_rows, w_mod, b_mod).reshape(n_layers, mod_rows * 6, 1, d)

    w_in_a = _pack_w_in(w_in, _IN_COLS[:GLA_PACK + ML_PACK])
    w_in_b = _pack_w_in(w_in, _IN_COLS[GLA_PACK + ML_PACK:])
    w_out_p = w_out.astype(BF16)
    grid_rows = n_lat // GRID_W
    gla_w2, gla_b2 = _pack_gla(gla_w_gate2, gla_b_gate)
    ml_cw, ml_cb, ml_gb, ml_ng = _pack_mlstm(ml_conv_w, ml_conv_b, ml_b_i, ml_b_f, ml_norm_g)
    m2_cw, m2_cb, m2_dtb, m2_alog, m2_dsk, m2_ng = _pack_ssd(m2_conv_w, m2_conv_b, m2_dt_bias, m2_A_log, m2_D,
                                                            m2_norm_g)
    rw = jnp.pad(router_w, ((0, 0), (0, 0), (0, LANES - N_EXPERTS)))
    rb = jnp.pad(router_b, ((0, 0), (0, LANES - N_EXPERTS)), constant_values=M_INIT).reshape(n_layers, 1, LANES)

    n_assign = nb * t * TOP_K
    n_blocks = n_assign // MOE_TILE + N_EXPERTS
    xb = jnp.zeros((n_blocks * MOE_TILE * ROW_SLABS, LANES), F32)
    yb = jnp.zeros_like(xb)

    xa = jnp.concatenate([x, ctx], axis=1)
    for l in range(n_layers):
        u_gla, u_ml = _in_proj(xa, None, mod[l], norm1_g[l], w_in_a[l], (GLA_PACK, ML_PACK), n_lat)
        x_cm = xa[:, :n_lat].reshape(nb, grid_rows, GRID_W, d).transpose(0, 2, 1, 3).reshape(nb, n_lat, d)
        u_m2, = _in_proj(xa, x_cm, mod[l], norm1_g[l], w_in_b[l], (M2_PACK,), n_lat)
        o_gla = _gla_mixer(u_gla, gla_w2[l], gla_b2[l], gla_norm_g[l].reshape(1, GLA_DV), n_lat)
        o_ml = _mlstm_mixer(u_ml, ml_cw[l], ml_cb[l], ml_gb[l], ml_ng[l], n_lat)
        o_m2 = _ssd_mixer(u_m2, m2_cw[l], m2_cb[l], m2_dtb[l], m2_alog[l], m2_dsk[l], m2_ng[l], n_lat)
        o_m2_lat = (o_m2[:, :n_lat].reshape(nb, GRID_W, grid_rows, M2_WIDTH).transpose(0, 2, 1, 3)
                    .reshape(nb, n_lat, M2_WIDTH))
        xa, h2, logits = _out_proj(xa, o_gla, o_ml, o_m2_lat, o_m2, w_out_p[l], mod[l], norm2_g[l], rw[l], rb[l],
                                   n_lat)
        e_arr, gates, rank_arr, counts = _route(logits)
        dest, block_e, n_valid = _moe_plan(e_arr, rank_arr, counts, n_blocks)
        xb = _dispatch(dest, h2, xb)
        yb = _experts(block_e, n_valid, xb, moe_w_gu[l], moe_b_gu[l], moe_w_dn[l], moe_b_dn[l], yb)
        xa = _combine(dest, yb, xa, gates, mod[l], n_lat)
    return _final_norm(xa, final_norm_g, n_lat)
```

```python
import functools
import math

import numpy as np
import jax
import jax.numpy as jnp
from jax import lax
from jax.experimental import pallas as pl
from jax.experimental.pallas import tpu as pltpu

F32 = jnp.float32
BF16 = jnp.bfloat16

D_MODEL = 1024
GRID_W = 64
GLA_HEADS, GLA_DK, GLA_DV, GLA_RANK = 4, 64, 128, 16
GLA_NORMALIZER = 16.0
ML_HEADS, ML_DQK, ML_DV = 4, 64, 128
M2_HEADS, M2_HEADDIM, M2_GROUPS, M2_DSTATE = 8, 64, 2, 128
CONV_W = 7
CONV_R = CONV_W // 2
N_EXPERTS, TOP_K, D_EXPERT = 32, 4, 1024
SWIGLU_LIMIT, SWIGLU_ALPHA = 7.0, 1.702
EPS = 1e-6
M_INIT = -1e30

GLA_WIDTH = GLA_HEADS * GLA_DV
ML_WIDTH = ML_HEADS * ML_DV
M2_WIDTH = M2_HEADS * M2_HEADDIM
MIX_WIDTH = GLA_WIDTH + ML_WIDTH + M2_WIDTH
GLA_IN = 2 * GLA_HEADS * GLA_DK + 2 * GLA_WIDTH + 2 * GLA_RANK
ML_IN = 2 * ML_HEADS * ML_DQK + 2 * ML_WIDTH + 4 * ML_HEADS
M2_CONV_DIM = M2_WIDTH + 2 * M2_GROUPS * M2_DSTATE
M2_IN = M2_WIDTH + M2_CONV_DIM + 2 * M2_HEADS
IN_WIDTH = GLA_IN + ML_IN + M2_IN

LANES = 128
SUBLANES = 8
CHUNK = 128
TOK_TILE = 256
MOE_TILE = 512
VMEM_LIMIT = 52 * 1024 * 1024

GLA_HEAD_COLS = 3 * LANES
GLA_PACK = GLA_HEADS * GLA_HEAD_COLS + LANES
ML_HEAD_COLS = 4 * LANES
ML_PACK = ML_HEADS * ML_HEAD_COLS
M2_GROUP_HEADS = M2_HEADS // M2_GROUPS
M2_GROUP_X = M2_GROUP_HEADS * M2_HEADDIM
M2_PAIRS = M2_GROUP_HEADS // 2
M2_DT_COLS = 2 * M2_PAIRS * LANES
M2_GROUP_COLS = 2 * M2_GROUP_X + 2 * M2_DSTATE + M2_DT_COLS
M2_PACK = M2_GROUPS * M2_GROUP_COLS
IN_PACK = GLA_PACK + ML_PACK + M2_PACK
ML_GATE_REP = LANES // 4


def _in_proj_column_map():
    cols = []
    qk = GLA_HEADS * GLA_DK
    for h in range(GLA_HEADS):
        cols += list(range(h * GLA_DK, (h + 1) * GLA_DK))
        cols += list(range(qk + h * GLA_DK, qk + (h + 1) * GLA_DK))
        cols += list(range(2 * qk + h * GLA_DV, 2 * qk + (h + 1) * GLA_DV))
        cols += list(range(2 * qk + GLA_WIDTH + h * GLA_DV, 2 * qk + GLA_WIDTH + (h + 1) * GLA_DV))
    lr0 = 2 * qk + 2 * GLA_WIDTH
    cols += list(range(lr0, lr0 + 2 * GLA_RANK)) + [-1] * (LANES - 2 * GLA_RANK)
    a0 = GLA_IN
    qk = ML_HEADS * ML_DQK
    g0 = a0 + 2 * qk + 2 * ML_WIDTH
    for h in range(ML_HEADS):
        cols += list(range(a0 + h * ML_DQK, a0 + (h + 1) * ML_DQK))
        cols += list(range(a0 + qk + h * ML_DQK, a0 + qk + (h + 1) * ML_DQK))
        cols += list(range(a0 + 2 * qk + h * ML_DV, a0 + 2 * qk + (h + 1) * ML_DV))
        cols += list(range(a0 + 2 * qk + ML_WIDTH + h * ML_DV, a0 + 2 * qk + ML_WIDTH + (h + 1) * ML_DV))
        for gate in range(4):
            cols += [g0 + gate * ML_HEADS + h] * ML_GATE_REP
    a1 = GLA_IN + ML_IN
    x0 = a1 + M2_WIDTH
    dt0 = a1 + M2_WIDTH + M2_CONV_DIM
    for g in range(M2_GROUPS):
        cols += list(range(a1 + g * M2_GROUP_X, a1 + (g + 1) * M2_GROUP_X))
        cols += list(range(x0 + g * M2_GROUP_X, x0 + (g + 1) * M2_GROUP_X))
        cols += list(range(x0 + M2_WIDTH + g * M2_DSTATE, x0 + M2_WIDTH + (g + 1) * M2_DSTATE))
        cols += list(range(x0 + M2_WIDTH + M2_GROUPS * M2_DSTATE + g * M2_DSTATE,
                           x0 + M2_WIDTH + M2_GROUPS * M2_DSTATE + (g + 1) * M2_DSTATE))
        for d in range(2):
            for h in range(M2_GROUP_HEADS):
                cols += [dt0 + d * M2_HEADS + g * M2_GROUP_HEADS + h] * M2_HEADDIM
    cols = np.asarray(cols, np.int32)
    assert cols.shape == (IN_PACK,)
    return cols


_IN_COLS = _in_proj_column_map()


def _bdot(a, b):
    return jnp.dot(a.astype(BF16), b.astype(BF16), preferred_element_type=F32)


def _bdot_nt(a, b):
    return lax.dot_general(a.astype(BF16), b.astype(BF16), (((1,), (1,)), ((), ())),
                           preferred_element_type=F32)


def _bdot_tn(a, b):
    return lax.dot_general(a.astype(BF16), b.astype(BF16), (((0,), (0,)), ((), ())),
                           preferred_element_type=F32)


def _split2(a):
    hi = a.astype(BF16)
    lo = (a - hi.astype(F32)).astype(BF16)
    return hi, lo


def _split3(a):
    hi = a.astype(BF16)
    r = a - hi.astype(F32)
    mid = r.astype(BF16)
    lo = (r - mid.astype(F32)).astype(BF16)
    return hi, mid, lo


def _hdot(a, b):
    ah, al = _split2(a)
    bh, bl = _split2(b)
    d = functools.partial(jnp.dot, preferred_element_type=F32)
    return d(ah, bh) + (d(ah, bl) + d(al, bh))


def _tri_dot(tri, a):
    hi, mid, lo = _split3(a)
    d = functools.partial(jnp.dot, preferred_element_type=F32)
    return d(tri, hi) + (d(tri, mid) + d(tri, lo))


def _rms(x):
    return x * lax.rsqrt(jnp.mean(x * x, axis=-1, keepdims=True) + EPS)


def _sigmoid(x):
    return 1.0 / (1.0 + jnp.exp(-x))


def _silu(x):
    return x * _sigmoid(x)


def _log_sigmoid(x):
    return jnp.minimum(x, 0.0) - jnp.log(1.0 + jnp.exp(-jnp.abs(x)))


def _softplus(x):
    return jnp.maximum(x, 0.0) + jnp.log(1.0 + jnp.exp(-jnp.abs(x)))


def _lane_iota(shape):
    return lax.broadcasted_iota(jnp.int32, shape, len(shape) - 1)


def _row_iota(shape):
    return lax.broadcasted_iota(jnp.int32, shape, len(shape) - 2)


def _lane_rep(x, lo, width):
    lane = _lane_iota(x.shape)
    y = jnp.where((lane >= lo) & (lane < lo + width), x, 0.0)
    w = width
    while w < LANES:
        y = y + pltpu.roll(y, w, axis=1)
        w *= 2
    return y


def _swap_halves(x):
    return pltpu.roll(x, LANES // 2, axis=1)


def _tri_incl(n):
    r = lax.broadcasted_iota(jnp.int32, (n, n), 0)
    c = lax.broadcasted_iota(jnp.int32, (n, n), 1)
    return (r >= c).astype(BF16)


def _mod_kernel(c_ref, w_ref, b_ref, o_ref):
    o_ref[...] = _hdot(_silu(c_ref[...]), w_ref[...]) + b_ref[...]


def _mod_table(c_rows, w_mod, b_mod):
    n_layers, d, n6 = w_mod.shape
    r = c_rows.shape[0]
    tn = 1536
    return pl.pallas_call(
        _mod_kernel,
        grid=(n_layers, n6 // tn),
        in_specs=[pl.BlockSpec((r, d), lambda l, j: (0, 0)),
                  pl.BlockSpec((None, d, tn), lambda l, j: (l, 0, j)),
                  pl.BlockSpec((None, 1, tn), lambda l, j: (l, 0, j))],
        out_specs=pl.BlockSpec((None, r, tn), lambda l, j: (l, 0, j)),
        out_shape=jax.ShapeDtypeStruct((n_layers, r, n6), F32),
        compiler_params=pltpu.CompilerParams(dimension_semantics=("arbitrary", "arbitrary"),
                                             vmem_limit_bytes=VMEM_LIMIT),
        name="adaln_mod",
    )(c_rows, w_mod, b_mod.reshape(n_layers, 1, n6))


def _inproj_kernel(*refs, n_lat_tiles, widths, two_sources):
    if two_sources:
        xl_ref, xc_ref, sh_ref, sc_ref, g_ref, w_ref = refs[:6]
        outs = refs[6:]
        x = jnp.where(pl.program_id(1) < n_lat_tiles, xl_ref[...], xc_ref[...])
    else:
        x_ref, sh_ref, sc_ref, g_ref, w_ref = refs[:5]
        outs = refs[5:]
        x = x_ref[...]
    h = _rms(x) * g_ref[...]
    h = (h * (1.0 + sc_ref[...]) + sh_ref[...]).astype(BF16)
    c0 = 0
    for o_ref, width in zip(outs, widths):
        o_ref[...] = jnp.dot(h, w_ref[:, c0:c0 + width], preferred_element_type=F32)
        c0 += width


def _mod_spec(which, n_lat_tiles, n_batch):
    def imap(b, t):
        row = jnp.where(t < n_lat_tiles, b, n_batch)
        return (row * 6 + which, 0, 0)
    return pl.BlockSpec((None, 1, D_MODEL), imap)


def _in_proj(xa, x_lat, mod3, norm_g, w_pack, widths, n_lat):
    nb, t, d = xa.shape
    n_lat_tiles = n_lat // TOK_TILE
    tok = lambda width: pl.BlockSpec((None, TOK_TILE, width), lambda b, i: (b, i, 0))
    if x_lat is None:
        x_specs, x_args = [tok(d)], (xa,)
    else:
        x_specs = [pl.BlockSpec((None, TOK_TILE, d), lambda b, i: (b, jnp.minimum(i, n_lat_tiles - 1), 0)),
                   pl.BlockSpec((None, TOK_TILE, d), lambda b, i: (b, jnp.maximum(i, n_lat_tiles), 0))]
        x_args = (x_lat, xa)
    return pl.pallas_call(
        functools.partial(_inproj_kernel, n_lat_tiles=n_lat_tiles, widths=widths, two_sources=x_lat is not None),
        grid=(nb, t // TOK_TILE),
        in_specs=x_specs + [_mod_spec(0, n_lat_tiles, nb), _mod_spec(1, n_lat_tiles, nb),
                            pl.BlockSpec((1, d), lambda b, i: (0, 0)),
                            pl.BlockSpec((d, sum(widths)), lambda b, i: (0, 0))],
        out_specs=[tok(w) for w in widths],
        out_shape=[jax.ShapeDtypeStruct((nb, t, w), F32) for w in widths],
        compiler_params=pltpu.CompilerParams(dimension_semantics=("arbitrary", "arbitrary"),
                                             vmem_limit_bytes=VMEM_LIMIT),
        name="in_proj",
    )(*x_args, mod3, mod3, norm_g.reshape(1, d), w_pack)


def _chunk_rows(c):
    return pl.ds(pl.multiple_of(c * CHUNK, CHUNK), CHUNK)


def _row_bcast(row):
    return jnp.broadcast_to(row, (SUBLANES, row.shape[-1]))


CHUNK_UNROLL = 2


def _for_chunks(n, body):
    assert n % CHUNK_UNROLL == 0

    def step(i, carry):
        for j in range(CHUNK_UNROLL):
            body(i * CHUNK_UNROLL + j)
        return carry

    lax.fori_loop(0, n // CHUNK_UNROLL, step, 0)


CUMSUM_GROUP = 6


def _chunk_cumsums(n, tri, load, emit):
    assert n % CUMSUM_GROUP == 0
    for c0 in range(0, n, CUMSUM_GROUP):
        xs = [load(c) for c in range(c0, c0 + CUMSUM_GROUP)]
        width = xs[0].shape[1]
        p = _tri_dot(tri, jnp.concatenate(xs, axis=1))
        for j in range(CUMSUM_GROUP):
            emit(c0 + j, xs[j], p[:, j * width:(j + 1) * width])


def _gla_kernel(u_ref, lr_ref, w2_ref, b2_ref, ng_ref, o_ref,
                cum_ref, oin_ref, q2_ref, ds_ref, a_ref, stf_ref, stb_ref, *, n_lat_chunks):
    t = u_ref.shape[0]
    n = t // CHUNK
    tri = _tri_incl(CHUNK)
    fwd = _lane_iota((CHUNK, LANES)) < LANES // 2
    row = _row_iota((CHUNK, CHUNK))
    col = _lane_iota((CHUNK, CHUNK))
    w2 = w2_ref[...]
    b2 = b2_ref[...]

    def log_decay(i, carry):
        rows = pl.ds(pl.multiple_of(i * TOK_TILE, TOK_TILE), TOK_TILE)
        cum_ref[rows, :] = _log_sigmoid(_hdot(lr_ref[rows, :], w2) + b2) * (1.0 / GLA_NORMALIZER)
        return carry

    lax.fori_loop(0, t // TOK_TILE, log_decay, 0)

    def emit_cum(c, la, p):
        tot = p[CHUNK - 1:CHUNK, :]
        cum_ref[c * CHUNK:(c + 1) * CHUNK, :] = jnp.where(fwd, p, tot - p + la)
        a_ref[c] = _row_bcast(jnp.exp(tot))

    _chunk_cumsums(n, tri, lambda c: cum_ref[c * CHUNK:(c + 1) * CHUNK, :], emit_cum)

    def local(c):
        rows = _chunk_rows(c)
        qk = u_ref[rows, 0:LANES]
        v = u_ref[rows, LANES:2 * LANES]
        cum = cum_ref[rows, :]
        tot = jnp.where(fwd[0:1], cum[CHUNK - 1:CHUNK, :], cum[0:1, :])
        mid = cum[CHUNK // 2:CHUNK // 2 + 1, :]
        sw = _swap_halves(qk)
        qq = jnp.where(fwd, qk, sw) * (GLA_DK ** -0.5)
        kk = jnp.where(fwd, sw, qk)
        qe = qq * jnp.exp(cum - mid)
        ke = kk * jnp.exp(mid - cum)
        af = _bdot_nt(jnp.where(fwd, qe, 0.0), ke)
        ab = _bdot_nt(jnp.where(fwd, 0.0, qe), ke)
        attn = jnp.where(row >= col, af, 0.0) + jnp.where(col >= row, ab, 0.0)
        oin_ref[rows, :] = _bdot(attn, v)
        q2_ref[rows, :] = qq * jnp.exp(cum)
        ds_ref[c] = _bdot_tn(v, kk * jnp.exp(tot - cum))

    _for_chunks(n, local)

    def scan(s, st):
        f = lax.rem(s + n_lat_chunks, n)
        g = n - 1 - s
        stf_ref[f] = st
        stb_ref[g] = st
        a = jnp.where(fwd[0:1], a_ref[f][0:1], a_ref[g][0:1])
        return st * a + jnp.where(fwd, ds_ref[f], ds_ref[g])

    lax.fori_loop(0, n, scan, jnp.zeros((GLA_DV, LANES), F32))

    def finish(c):
        rows = _chunk_rows(c)
        st = jnp.where(fwd, stf_ref[c], stb_ref[c])
        o = oin_ref[rows, :] + _bdot_nt(q2_ref[rows, :], st)
        o = _rms(o) * ng_ref[...]
        o_ref[rows, :] = (o * _silu(u_ref[rows, 2 * LANES:3 * LANES])).astype(o_ref.dtype)

    _for_chunks(n, finish)


def _gla_mixer(u, w2, b2, ng, n_lat):
    nb, t, _ = u.shape
    n = t // CHUNK
    return pl.pallas_call(
        functools.partial(_gla_kernel, n_lat_chunks=n_lat // CHUNK),
        grid=(nb, GLA_HEADS),
        in_specs=[pl.BlockSpec((None, t, GLA_HEAD_COLS), lambda b, h: (b, 0, h)),
                  pl.BlockSpec((None, t, LANES), lambda b, h: (b, 0, GLA_HEADS * GLA_HEAD_COLS // LANES)),
                  pl.BlockSpec((None, LANES, LANES), lambda b, h: (h, 0, 0)),
                  pl.BlockSpec((None, 1, LANES), lambda b, h: (h, 0, 0)),
                  pl.BlockSpec((1, LANES), lambda b, h: (0, 0))],
        out_specs=pl.BlockSpec((None, t, GLA_DV), lambda b, h: (b, 0, h)),
        out_shape=jax.ShapeDtypeStruct((nb, t, GLA_WIDTH), BF16),
        scratch_shapes=[pltpu.VMEM((t, LANES), F32), pltpu.VMEM((t, LANES), F32), pltpu.VMEM((t, LANES), F32),
                        pltpu.VMEM((n, GLA_DV, LANES), F32), pltpu.VMEM((n, SUBLANES, LANES), F32),
                        pltpu.VMEM((n, GLA_DV, LANES), F32), pltpu.VMEM((n, GLA_DV, LANES), F32)],
        compiler_params=pltpu.CompilerParams(dimension_semantics=("arbitrary", "arbitrary"),
                                             vmem_limit_bytes=VMEM_LIMIT),
        name="gla_mixer",
    )(u, u, w2, b2, ng)


def _zero_pads(pad_ref, n_lat, t):
    z = jnp.zeros((SUBLANES, pad_ref.shape[1]), F32)
    pad_ref[0:SUBLANES, :] = z
    pad_ref[SUBLANES + n_lat:2 * SUBLANES + n_lat, :] = z
    pad_ref[2 * SUBLANES + t:3 * SUBLANES + t, :] = z


def _pad_base(c, n_lat_chunks):
    return c * CHUNK + (SUBLANES if c < n_lat_chunks else 2 * SUBLANES)


def _conv_silu(pad_ref, w_ref, b_ref, out_ref, n, n_lat_chunks):
    width = pad_ref.shape[1]
    for c in range(n):
        base = _pad_base(c, n_lat_chunks)
        for l0 in range(0, width, LANES):
            acc = None
            for j in range(CONV_W):
                term = w_ref[j:j + 1, l0:l0 + LANES] * pad_ref[base + j - CONV_R:base + j - CONV_R + CHUNK,
                                                               l0:l0 + LANES]
                acc = term if acc is None else acc + term
            out_ref[c * CHUNK:(c + 1) * CHUNK, l0:l0 + LANES] = _silu(acc + b_ref[:, l0:l0 + LANES])


def _mlstm_kernel(u_ref, cw_ref, cb_ref, gb_ref, ng_ref, o_ref,
                  pad_ref, qk_ref, fc_ref, rc_ref, dc_ref, tot_ref, mloc_ref, stm_ref, *, n_lat_chunks):
    t = u_ref.shape[0]
    n = t // CHUNK
    n_lat = n_lat_chunks * CHUNK
    tri = _tri_incl(CHUNK)
    lane = _lane_iota((CHUNK, LANES))
    hi_half = lane >= LANES // 2
    row = _row_iota((CHUNK, CHUNK))
    col = _lane_iota((CHUNK, CHUNK))
    masks = (row >= col, col >= row)
    ones = jnp.ones((CHUNK, LANES), F32)

    _zero_pads(pad_ref, n_lat, t)
    pad_ref[SUBLANES:SUBLANES + n_lat, :] = u_ref[0:n_lat, 0:LANES]
    pad_ref[2 * SUBLANES + n_lat:2 * SUBLANES + t, :] = u_ref[n_lat:t, 0:LANES]
    _conv_silu(pad_ref, cw_ref, cb_ref, qk_ref, n, n_lat_chunks)

    def khat_of(qk):
        return jnp.where(hi_half, qk, 0.0) * (ML_DQK ** -0.5)

    sel_r = lax.broadcasted_iota(jnp.int32, (LANES, 4 * LANES), 0)
    sel_c = lax.broadcasted_iota(jnp.int32, (LANES, 4 * LANES), 1)
    spread = (sel_r == (sel_c // LANES) * ML_GATE_REP).astype(BF16)

    def load_gates(c):
        rows = slice(c * CHUNK, (c + 1) * CHUNK)
        g = u_ref[rows, 3 * LANES:4 * LANES] + gb_ref[...]
        g = jnp.where(hi_half, _log_sigmoid(g), g)
        hi, mid, lo = _split3(g)
        d = functools.partial(jnp.dot, preferred_element_type=F32)
        wide = d(hi, spread) + (d(mid, spread) + d(lo, spread))
        rc_ref[0, rows, :] = wide[:, 0:LANES]
        rc_ref[1, rows, :] = wide[:, LANES:2 * LANES]
        return wide[:, 2 * LANES:]

    def emit_gates(c, lf, p):
        rows = slice(c * CHUNK, (c + 1) * CHUNK)
        tot_b = p[CHUNK - 1:CHUNK, LANES:]
        f_dir = (p[:, 0:LANES], tot_b - p[:, LANES:] + lf[:, LANES:])
        for d in range(2):
            fc_ref[d, rows, :] = f_dir[d]
            rc_ref[d, rows, :] = rc_ref[d, rows, :] - f_dir[d]

    _chunk_cumsums(n, tri, load_gates, emit_gates)

    def local(c):
        rows = _chunk_rows(c)
        khat = khat_of(qk_ref[rows, :])
        vaug = jnp.concatenate([u_ref[rows, LANES:2 * LANES], ones], axis=1)
        for d in range(2):
            fc = fc_ref[d, rows, :]
            tt = fc[CHUNK - 1:CHUNK, :] if d == 0 else fc[0:1, :]
            gend = tt + rc_ref[d, rows, :]
            mloc = jnp.max(gend, axis=0, keepdims=True)
            dc_ref[d, c] = _bdot_tn(khat * jnp.exp(gend - mloc), vaug)
            tot_ref[d, c] = _row_bcast(tt)
            mloc_ref[d, c] = _row_bcast(mloc)

    _for_chunks(n, local)

    def scan(s, carry):
        new = []
        for d, idx in ((0, lax.rem(s + n_lat_chunks, n)), (1, n - 1 - s)):
            cst, m = carry[d]
            inc = dc_ref[d, idx]
            dc_ref[d, idx] = cst
            stm_ref[d, idx] = _row_bcast(m)
            tt = tot_ref[d, idx][0:1]
            ml = mloc_ref[d, idx][0:1]
            m_new = jnp.maximum(tt + m, ml)
            a = jnp.exp(tt + m - m_new)[:, 0:1]
            sc = jnp.exp(ml - m_new)[:, 0:1]
            new.append((a * cst + sc * inc, m_new))
        return tuple(new)

    init = (jnp.zeros((LANES, 2 * LANES), F32), jnp.full((1, LANES), M_INIT, F32))
    lax.fori_loop(0, n, scan, (init, init))

    def finish(c):
        rows = _chunk_rows(c)
        qk = qk_ref[rows, :]
        qhat = jnp.where(hi_half, _swap_halves(qk), 0.0)
        s_qk = _bdot_nt(qhat, khat_of(qk))
        vaug = jnp.concatenate([u_ref[rows, LANES:2 * LANES], ones], axis=1)
        h = None
        for d in range(2):
            fc = fc_ref[d, rows, :]
            dlog = jnp.where(masks[d], fc + rc_ref[d, rows, :].T, -jnp.inf)
            inter = fc + stm_ref[d, c][0:1]
            m_row = jnp.maximum(inter, jnp.max(dlog, axis=1, keepdims=True))
            w_inter = jnp.exp(inter - m_row)
            nd = (_bdot(s_qk * jnp.exp(dlog - m_row), vaug)
                  + jnp.concatenate([w_inter, w_inter], axis=1) * _bdot(qhat, dc_ref[d, c]))
            hd = nd[:, 0:LANES] / jnp.maximum(jnp.abs(nd[:, LANES:]), jnp.exp(-m_row))
            h = hd if h is None else h + hd
        h = _rms(h) * ng_ref[...]
        o_ref[rows, :] = (h * _sigmoid(u_ref[rows, 2 * LANES:3 * LANES])).astype(o_ref.dtype)

    _for_chunks(n, finish)


def _mlstm_mixer(u, cw, cb, gb, ng, n_lat):
    nb, t, _ = u.shape
    n = t // CHUNK
    head = lambda rows: pl.BlockSpec((None, rows, LANES), lambda b, h: (h, 0, 0))
    return pl.pallas_call(
        functools.partial(_mlstm_kernel, n_lat_chunks=n_lat // CHUNK),
        grid=(nb, ML_HEADS),
        in_specs=[pl.BlockSpec((None, t, ML_HEAD_COLS), lambda b, h: (b, 0, h)),
                  head(SUBLANES), head(1), head(1), head(1)],
        out_specs=pl.BlockSpec((None, t, ML_DV), lambda b, h: (b, 0, h)),
        out_shape=jax.ShapeDtypeStruct((nb, t, ML_WIDTH), BF16),
        scratch_shapes=[pltpu.VMEM((t + 3 * SUBLANES, LANES), F32), pltpu.VMEM((t, LANES), F32),
                        pltpu.VMEM((2, t, LANES), F32), pltpu.VMEM((2, t, LANES), F32),
                        pltpu.VMEM((2, n, LANES, 2 * LANES), F32),
                        pltpu.VMEM((2, n, SUBLANES, LANES), F32), pltpu.VMEM((2, n, SUBLANES, LANES), F32),
                        pltpu.VMEM((2, n, SUBLANES, LANES), F32)],
        compiler_params=pltpu.CompilerParams(dimension_semantics=("arbitrary", "arbitrary"),
                                             vmem_limit_bytes=VMEM_LIMIT),
        name="mlstm_mixer",
    )(u, cw, cb, gb, ng)


M2_CONV_COLS = M2_GROUP_X + 2 * M2_DSTATE
M2_X0 = M2_GROUP_X
M2_DT0 = M2_X0 + M2_CONV_COLS


def _ssd_kernel(u_ref, cw_ref, cb_ref, dtb_ref, alog_ref, dsk_ref, ng_ref, o_ref,
                pad_ref, xc_ref, dt_ref, dh_ref, a_ref, y_ref, *, n_lat_chunks):
    t = u_ref.shape[0]
    n = t // CHUNK
    n_lat = n_lat_chunks * CHUNK
    n_state = 2 * M2_PAIRS
    tri = _tri_incl(CHUNK)
    lo_half = _lane_iota((CHUNK, LANES)) < LANES // 2
    row = _row_iota((CHUNK, CHUNK))
    col = _lane_iota((CHUNK, CHUNK))
    masks = (row >= col, col >= row)
    fwd_cols = _lane_iota((CHUNK, M2_DT_COLS)) < M2_DT_COLS // 2
    a_row = -jnp.exp(alog_ref[...])

    _zero_pads(pad_ref, n_lat, t)
    pad_ref[SUBLANES:SUBLANES + n_lat, :] = u_ref[0:n_lat, M2_X0:M2_DT0]
    pad_ref[2 * SUBLANES + n_lat:2 * SUBLANES + t, :] = u_ref[n_lat:t, M2_X0:M2_DT0]
    _conv_silu(pad_ref, cw_ref, cb_ref, xc_ref, n, n_lat_chunks)

    cum_ref = pad_ref

    def load_decay(c):
        rows = slice(c * CHUNK, (c + 1) * CHUNK)
        dt = _softplus(u_ref[rows, M2_DT0:] + dtb_ref[...])
        dt_ref[rows, :] = dt
        return dt * a_row

    def emit_decay(c, da, p):
        tot = p[CHUNK - 1:CHUNK, :]
        cum_ref[c * CHUNK:(c + 1) * CHUNK, :] = jnp.where(fwd_cols, p, tot - p + da)

    _chunk_cumsums(n, tri, load_decay, emit_decay)

    def local(c):
        rows = _chunk_rows(c)
        dt = dt_ref[rows, :]
        cum = cum_ref[rows, :]
        tot = jnp.where(fwd_cols[0:1], cum[CHUNK - 1:CHUNK, :], cum[0:1, :])
        x = xc_ref[rows, 0:M2_GROUP_X]
        bm = xc_ref[rows, M2_GROUP_X:M2_GROUP_X + M2_DSTATE]
        cm = xc_ref[rows, M2_GROUP_X + M2_DSTATE:]
        g = _bdot_nt(cm, bm)
        y = [None] * M2_PAIRS
        for d in range(2):
            for p in range(M2_PAIRS):
                k = d * M2_PAIRS + p
                sl = slice(k * LANES, (k + 1) * LANES)
                fp = cum[:, sl]
                tt = tot[:, sl]
                xdt = x[:, p * LANES:(p + 1) * LANES] * dt[:, sl]
                dh_ref[c, k] = _bdot_tn(bm, jnp.exp(tt - fp) * xdt)
                a_ref[c, k] = _row_bcast(jnp.exp(tt))
                sw = _swap_halves(fp)
                halves = []
                for fh in (jnp.where(lo_half, fp, sw), jnp.where(lo_half, sw, fp)):
                    dec = jnp.exp(jnp.where(masks[d], fh - fh.T, -jnp.inf))
                    halves.append(_bdot(g * dec, xdt))
                yp = jnp.where(lo_half, halves[0], halves[1])
                y[p] = yp if y[p] is None else y[p] + yp
        for p in range(M2_PAIRS):
            y_ref[p, rows, :] = y[p]

    _for_chunks(n, local)

    def scan(s, carry):
        f = lax.rem(s + n_lat_chunks, n)
        g = n - 1 - s
        new = []
        for k in range(n_state):
            idx = f if k < M2_PAIRS else g
            inc = dh_ref[idx, k]
            dh_ref[idx, k] = carry[k]
            new.append(carry[k] * a_ref[idx, k][0:1] + inc)
        return tuple(new)

    lax.fori_loop(0, n, scan, tuple(jnp.zeros((M2_DSTATE, LANES), F32) for _ in range(n_state)))

    def finish(c):
        rows = _chunk_rows(c)
        cum = cum_ref[rows, :]
        cm = xc_ref[rows, M2_GROUP_X + M2_DSTATE:]
        y = [y_ref[p, rows, :] for p in range(M2_PAIRS)]
        for k in range(n_state):
            p = k % M2_PAIRS
            y[p] = y[p] + jnp.exp(cum[:, k * LANES:(k + 1) * LANES]) * _bdot(cm, dh_ref[c, k])
        y = jnp.concatenate(y, axis=1) + dsk_ref[...] * xc_ref[rows, 0:M2_GROUP_X]
        y = _rms(y * _silu(u_ref[rows, 0:M2_X0])) * ng_ref[...]
        o_ref[rows, :] = y.astype(o_ref.dtype)

    _for_chunks(n, finish)


def _ssd_mixer(u, cw, cb, dtb, alog, dsk, ng, n_lat):
    nb, t, _ = u.shape
    n = t // CHUNK
    grp = lambda rows, width: pl.BlockSpec((None, rows, width), lambda b, g: (g, 0, 0))
    return pl.pallas_call(
        functools.partial(_ssd_kernel, n_lat_chunks=n_lat // CHUNK),
        grid=(nb, M2_GROUPS),
        in_specs=[pl.BlockSpec((None, t, M2_GROUP_COLS), lambda b, g: (b, 0, g)),
                  grp(SUBLANES, M2_CONV_COLS), grp(1, M2_CONV_COLS), grp(1, M2_DT_COLS), grp(1, M2_DT_COLS),
                  grp(1, M2_GROUP_X), grp(1, M2_GROUP_X)],
        out_specs=pl.BlockSpec((None, t, M2_GROUP_X), lambda b, g: (b, 0, g)),
        out_shape=jax.ShapeDtypeStruct((nb, t, M2_WIDTH), BF16),
        scratch_shapes=[pltpu.VMEM((t + 3 * SUBLANES, M2_CONV_COLS), F32), pltpu.VMEM((t, M2_CONV_COLS), F32),
                        pltpu.VMEM((t, M2_DT_COLS), F32),
                        pltpu.VMEM((n, 2 * M2_PAIRS, M2_DSTATE, LANES), F32),
                        pltpu.VMEM((n, 2 * M2_PAIRS, SUBLANES, LANES), F32),
                        pltpu.VMEM((M2_PAIRS, t, LANES), F32)],
        compiler_params=pltpu.CompilerParams(dimension_semantics=("arbitrary", "arbitrary"),
                                             vmem_limit_bytes=VMEM_LIMIT),
        name="ssd_mixer",
    )(u, cw, cb, dtb, alog, dsk, ng)


ROW_SLABS = D_MODEL // LANES


def _slab(c, n_rows):
    return pl.ds(c, n_rows, stride=ROW_SLABS)


def _outproj_kernel(x_ref, og_ref, om_ref, osl_ref, osc_ref, w_ref, gate_ref, sh_ref, sc_ref, g2_ref, rw_ref,
                    rb_ref, xo_ref, h2_ref, lg_ref, *, n_lat_tiles):
    o_ssd = jnp.where(pl.program_id(1) < n_lat_tiles, osl_ref[...], osc_ref[...])
    mix = (jnp.dot(og_ref[...], w_ref[0:GLA_WIDTH, :], preferred_element_type=F32)
           + jnp.dot(om_ref[...], w_ref[GLA_WIDTH:GLA_WIDTH + ML_WIDTH, :], preferred_element_type=F32)
           + jnp.dot(o_ssd, w_ref[GLA_WIDTH + ML_WIDTH:, :], preferred_element_type=F32))
    x = x_ref[...] + gate_ref[...] * mix
    xo_ref[...] = x
    h2 = (_rms(x) * g2_ref[...]) * (1.0 + sc_ref[...]) + sh_ref[...]
    for c in range(ROW_SLABS):
        h2_ref[_slab(c, TOK_TILE), :] = h2[:, c * LANES:(c + 1) * LANES]
    lg_ref[...] = _hdot(h2, rw_ref[...]) + rb_ref[...]


def _out_proj(xa, o_gla, o_ml, o_m2_lat, o_m2, w_out, mod3, norm_g, rw, rb, n_lat):
    nb, t, d = xa.shape
    nt = t // TOK_TILE
    n_lat_tiles = n_lat // TOK_TILE
    tok = lambda width: pl.BlockSpec((None, TOK_TILE, width), lambda b, i: (b, i, 0))
    const = lambda r, c: pl.BlockSpec((r, c), lambda b, i: (0, 0))
    mod = lambda which: _mod_spec(which, n_lat_tiles, nb)
    ssd_lat = pl.BlockSpec((None, TOK_TILE, M2_WIDTH), lambda b, i: (b, jnp.minimum(i, n_lat_tiles - 1), 0))
    ssd_ctx = pl.BlockSpec((None, TOK_TILE, M2_WIDTH), lambda b, i: (b, jnp.maximum(i, n_lat_tiles), 0))
    return pl.pallas_call(
        functools.partial(_outproj_kernel, n_lat_tiles=n_lat_tiles),
        grid=(nb, nt),
        in_specs=[tok(d), tok(GLA_WIDTH), tok(ML_WIDTH), ssd_lat, ssd_ctx, const(MIX_WIDTH, d),
                  mod(2), mod(3), mod(4), const(1, d), const(d, LANES), const(1, LANES)],
        out_specs=[tok(d),
                   pl.BlockSpec((TOK_TILE * ROW_SLABS, LANES), lambda b, i: (b * nt + i, 0)),
                   pl.BlockSpec((TOK_TILE, LANES), lambda b, i: (b * nt + i, 0))],
        out_shape=[jax.ShapeDtypeStruct((nb, t, d), F32),
                   jax.ShapeDtypeStruct((nb * t * ROW_SLABS, LANES), F32),
                   jax.ShapeDtypeStruct((nb * t, LANES), F32)],
        compiler_params=pltpu.CompilerParams(dimension_semantics=("arbitrary", "arbitrary"),
                                             vmem_limit_bytes=VMEM_LIMIT),
        name="out_proj",
    )(xa, o_gla, o_ml, o_m2_lat, o_m2, w_out, mod3, mod3, mod3, norm_g.reshape(1, d), rw, rb)


ROUTE_TILE = 256


def _route_kernel(lg_ref, e_ref, gt_ref, rk_ref, cnt_ref, base_ref):
    @pl.when(pl.program_id(0) == 0)
    def _():
        base_ref[...] = jnp.zeros_like(base_ref)

    lane = _lane_iota((ROUTE_TILE, LANES))
    work = lg_ref[...]
    vals, idxs = [], []
    for _ in range(TOP_K):
        m = jnp.max(work, axis=1, keepdims=True)
        idx = jnp.min(jnp.where(work == m, lane, LANES), axis=1, keepdims=True)
        vals.append(m)
        idxs.append(idx)
        work = jnp.where(lane == idx, -jnp.inf, work)
    ex = [jnp.exp(v - vals[0]) for v in vals]
    inv = 1.0 / (ex[0] + ex[1] + ex[2] + ex[3])
    r = _row_iota((ROUTE_TILE, ROUTE_TILE))
    c = _lane_iota((ROUTE_TILE, ROUTE_TILE))
    earlier = (r > c).astype(BF16)
    base = base_ref[0:1, :]
    e_out = jnp.zeros((ROUTE_TILE, LANES), jnp.int32)
    g_out = jnp.zeros((ROUTE_TILE, LANES), F32)
    r_out = jnp.zeros((ROUTE_TILE, LANES), F32)
    for k in range(TOP_K):
        onehot = (lane == idxs[k]).astype(F32)
        within = jnp.dot(earlier, onehot.astype(BF16), preferred_element_type=F32)
        rank = jnp.sum((base + within) * onehot, axis=1, keepdims=True)
        base = base + jnp.sum(onehot, axis=0, keepdims=True)
        e_out = jnp.where(lane == k, idxs[k], e_out)
        g_out = jnp.where(lane == k, ex[k] * inv, g_out)
        r_out = jnp.where(lane == k, rank, r_out)
    base_ref[...] = _row_bcast(base)
    cnt_ref[...] = _row_bcast(base)
    e_ref[...] = e_out
    gt_ref[...] = g_out
    rk_ref[...] = r_out


def _route(logits):
    n_tok = logits.shape[0]
    tile = pl.BlockSpec((ROUTE_TILE, LANES), lambda i: (i, 0))
    return pl.pallas_call(
        _route_kernel,
        grid=(n_tok // ROUTE_TILE,),
        in_specs=[tile],
        out_specs=[tile, tile, tile, pl.BlockSpec((SUBLANES, LANES), lambda i: (0, 0))],
        out_shape=[jax.ShapeDtypeStruct((n_tok, LANES), jnp.int32),
                   jax.ShapeDtypeStruct((n_tok, LANES), F32),
                   jax.ShapeDtypeStruct((n_tok, LANES), F32),
                   jax.ShapeDtypeStruct((SUBLANES, LANES), F32)],
        scratch_shapes=[pltpu.VMEM((SUBLANES, LANES), F32)],
        compiler_params=pltpu.CompilerParams(dimension_semantics=("arbitrary",)),
        name="moe_route",
    )(logits)


DISPATCH_TILE = 512


def _row_slab(r):
    return pl.ds(pl.multiple_of(r * ROW_SLABS, ROW_SLABS), ROW_SLABS)


WAIT_UNROLL = 32


def _drain_rows(src_ref, dst_ref, sem, n_rows):
    assert n_rows % WAIT_UNROLL == 0

    def body(i, carry):
        for _ in range(WAIT_UNROLL):
            pltpu.make_async_copy(src_ref.at[_row_slab(0)], dst_ref.at[_row_slab(0)], sem).wait()
        return carry

    lax.fori_loop(0, n_rows // WAIT_UNROLL, body, 0)


def _dispatch_kernel(dest_ref, h_ref, xb_in, xb_hbm, sem):
    del xb_in

    def issue(t, carry):
        for k in range(TOP_K):
            pltpu.make_async_copy(h_ref.at[_row_slab(t)],
                                  xb_hbm.at[_row_slab(dest_ref[0, t * TOP_K + k])], sem).start()
        return carry

    lax.fori_loop(0, DISPATCH_TILE, issue, 0)
    _drain_rows(h_ref, xb_hbm, sem, DISPATCH_TILE * TOP_K)


def _dispatch(dest, h2, xb):
    n_tiles = dest.shape[0] // (DISPATCH_TILE * TOP_K)
    return pl.pallas_call(
        _dispatch_kernel,
        grid=(n_tiles,),
        in_specs=[pl.BlockSpec((None, 1, DISPATCH_TILE * TOP_K), lambda i: (i, 0, 0), memory_space=pltpu.SMEM),
                  pl.BlockSpec((DISPATCH_TILE * ROW_SLABS, LANES), lambda i: (i, 0)),
                  pl.BlockSpec(memory_space=pl.ANY)],
        out_specs=pl.BlockSpec(memory_space=pl.ANY),
        out_shape=jax.ShapeDtypeStruct(xb.shape, xb.dtype),
        scratch_shapes=[pltpu.SemaphoreType.DMA(())],
        input_output_aliases={2: 0},
        compiler_params=pltpu.CompilerParams(dimension_semantics=("arbitrary",)),
        name="moe_dispatch",
    )(dest.reshape(n_tiles, 1, DISPATCH_TILE * TOP_K), h2, xb)


def _expert_kernel(be_ref, nv_ref, x_ref, wgu_ref, bgu_ref, wdn_ref, bdn_ref, yb_in, y_ref,
                   wgu_bf, wdn_bf, act_ref):
    del yb_in
    i = pl.program_id(0)
    valid = i < nv_ref[0]
    fresh = jnp.logical_or(i == 0, be_ref[i] != be_ref[jnp.maximum(i - 1, 0)])

    @pl.when(jnp.logical_and(valid, fresh))
    def _():
        for r0 in range(0, D_MODEL, LANES):
            wgu_bf[r0:r0 + LANES, :] = wgu_ref[r0:r0 + LANES, :].astype(BF16)
            wdn_bf[r0:r0 + LANES, :] = wdn_ref[r0:r0 + LANES, :].astype(BF16)

    @pl.when(valid)
    def _():
        x = jnp.concatenate([x_ref[_slab(c, MOE_TILE), :] for c in range(ROW_SLABS)], axis=1).astype(BF16)
        half = D_EXPERT // 2
        for c0 in range(0, D_EXPERT, half):
            glu = jnp.dot(x, wgu_bf[:, c0:c0 + half], preferred_element_type=F32) + bgu_ref[:, c0:c0 + half]
            lin = (jnp.dot(x, wgu_bf[:, D_EXPERT + c0:D_EXPERT + c0 + half], preferred_element_type=F32)
                   + bgu_ref[:, D_EXPERT + c0:D_EXPERT + c0 + half])
            glu = jnp.minimum(glu, SWIGLU_LIMIT)
            lin = jnp.clip(lin, -SWIGLU_LIMIT, SWIGLU_LIMIT)
            act_ref[:, c0:c0 + half] = (glu * _sigmoid(SWIGLU_ALPHA * glu) * (lin + 1.0)).astype(BF16)
        y = jnp.dot(act_ref[...], wdn_bf[...], preferred_element_type=F32) + bdn_ref[...]
        for c in range(ROW_SLABS):
            y_ref[_slab(c, MOE_TILE), :] = y[:, c * LANES:(c + 1) * LANES]


def _experts(block_e, n_valid, xb, w_gu, b_gu, w_dn, b_dn, yb):
    n_blocks = block_e.shape[0]
    blk = lambda i, be, nv: jnp.minimum(i, nv[0] - 1)
    rows = pl.BlockSpec((MOE_TILE * ROW_SLABS, LANES), lambda i, be, nv: (blk(i, be, nv), 0))
    per_e = lambda r, c: pl.BlockSpec((None, r, c), lambda i, be, nv: (be[blk(i, be, nv)], 0, 0))
    return pl.pallas_call(
        _expert_kernel,
        grid_spec=pltpu.PrefetchScalarGridSpec(
            num_scalar_prefetch=2,
            grid=(n_blocks,),
            in_specs=[rows, per_e(D_MODEL, 2 * D_EXPERT), per_e(1, 2 * D_EXPERT),
                      per_e(D_EXPERT, D_MODEL), per_e(1, D_MODEL), pl.BlockSpec(memory_space=pl.ANY)],
            out_specs=rows,
            scratch_shapes=[pltpu.VMEM((D_MODEL, 2 * D_EXPERT), BF16), pltpu.VMEM((D_EXPERT, D_MODEL), BF16),
                            pltpu.VMEM((MOE_TILE, D_EXPERT), BF16)]),
        out_shape=jax.ShapeDtypeStruct(xb.shape, F32),
        input_output_aliases={7: 0},
        compiler_params=pltpu.CompilerParams(dimension_semantics=("arbitrary",), vmem_limit_bytes=VMEM_LIMIT),
        name="moe_experts",
    )(block_e, n_valid, xb, w_gu, b_gu.reshape(N_EXPERTS, 1, -1), w_dn, b_dn.reshape(N_EXPERTS, 1, -1), yb)


COMBINE_TILE = 128


def _combine_kernel(dest_ref, dnext_ref, yb_hbm, x_ref, gt_ref, mg_ref, o_ref, buf_ref, sem, *, n_steps):
    step = pl.program_id(0) * pl.num_programs(1) + pl.program_id(1)
    slot = lax.rem(step, 2)

    def gather(d_ref, into):
        def body(t, carry):
            for k in range(TOP_K):
                pltpu.make_async_copy(yb_hbm.at[_row_slab(d_ref[0, t * TOP_K + k])],
                                      buf_ref.at[into, _row_slab(k * COMBINE_TILE + t)], sem.at[into]).start()
            return carry

        lax.fori_loop(0, COMBINE_TILE, body, 0)

    @pl.when(step == 0)
    def _():
        gather(dest_ref, 0)

    @pl.when(step + 1 < n_steps)
    def _():
        gather(dnext_ref, 1 - slot)

    _drain_rows(yb_hbm, buf_ref.at[slot], sem.at[slot], COMBINE_TILE * TOP_K)
    gates = gt_ref[...]
    acc = None
    for k in range(TOP_K):
        yk = jnp.concatenate(
            [buf_ref[slot, pl.ds(k * COMBINE_TILE * ROW_SLABS + c, COMBINE_TILE, stride=ROW_SLABS), :]
             for c in range(ROW_SLABS)], axis=1)
        term = gates[:, k:k + 1] * yk
        acc = term if acc is None else acc + term
    o_ref[...] = x_ref[...] + mg_ref[...] * acc


def _combine(dest, yb, xa, gates, mod3, n_lat):
    nb, t, d = xa.shape
    nt = t // COMBINE_TILE
    n_steps = nb * nt
    n_lat_tiles = n_lat // COMBINE_TILE
    tok = pl.BlockSpec((None, COMBINE_TILE, d), lambda b, i: (b, i, 0))
    rows_of = lambda step_of: pl.BlockSpec((None, 1, COMBINE_TILE * TOP_K),
                                           lambda b, i: (step_of(b * nt + i), 0, 0), memory_space=pltpu.SMEM)
    dest3 = dest.reshape(n_steps, 1, COMBINE_TILE * TOP_K)
    return pl.pallas_call(
        functools.partial(_combine_kernel, n_steps=n_steps),
        grid=(nb, nt),
        in_specs=[rows_of(lambda s: s), rows_of(lambda s: jnp.minimum(s + 1, n_steps - 1)),
                  pl.BlockSpec(memory_space=pl.ANY), tok,
                  pl.BlockSpec((COMBINE_TILE, LANES), lambda b, i: (b * nt + i, 0)),
                  _mod_spec(5, n_lat_tiles, nb)],
        out_specs=tok,
        out_shape=jax.ShapeDtypeStruct((nb, t, d), F32),
        scratch_shapes=[pltpu.VMEM((2, TOP_K * COMBINE_TILE * ROW_SLABS, LANES), F32),
                        pltpu.SemaphoreType.DMA((2,))],
        compiler_params=pltpu.CompilerParams(dimension_semantics=("arbitrary", "arbitrary"),
                                             vmem_limit_bytes=VMEM_LIMIT),
        name="moe_combine",
    )(dest3, dest3, yb, xa, gates, mod3)


def _final_norm_kernel(x_ref, g_ref, o_ref):
    o_ref[...] = _rms(x_ref[...]) * g_ref[...]


def _final_norm(xa, g, n_lat):
    nb, _, d = xa.shape
    tok = pl.BlockSpec((None, TOK_TILE, d), lambda b, i: (b, i, 0))
    return pl.pallas_call(
        _final_norm_kernel,
        grid=(nb, n_lat // TOK_TILE),
        in_specs=[tok, pl.BlockSpec((1, d), lambda b, i: (0, 0))],
        out_specs=tok,
        out_shape=jax.ShapeDtypeStruct((nb, n_lat, d), F32),
        compiler_params=pltpu.CompilerParams(dimension_semantics=("arbitrary", "arbitrary")),
        name="final_norm",
    )(xa, g.reshape(1, d))


def _pack_w_in(w_in_t, cols):
    nl, n_src, d = w_in_t.shape
    idx = np.where(cols >= 0, cols, nl * n_src)
    idx = np.where(cols >= 0, idx[None, :] + n_src * np.arange(nl)[:, None], nl * n_src).reshape(-1)
    rows = jnp.take(w_in_t.reshape(nl * n_src, d), jnp.asarray(idx, jnp.int32), axis=0, mode='fill', fill_value=0)
    return jnp.swapaxes(rows.reshape(nl, len(cols), d), 1, 2)


def _pack_gla(w_gate2, b_gate):
    nl = w_gate2.shape[0]
    w = w_gate2.reshape(nl, 2, GLA_RANK, GLA_HEADS, GLA_DK).transpose(0, 3, 1, 2, 4)
    z = jnp.zeros((nl, GLA_HEADS, GLA_RANK, GLA_DK), F32)
    top = jnp.concatenate([w[:, :, 0], z], axis=-1)
    bot = jnp.concatenate([z, w[:, :, 1]], axis=-1)
    w2 = jnp.concatenate([top, bot, jnp.zeros((nl, GLA_HEADS, LANES - 2 * GLA_RANK, LANES), F32)], axis=2)
    b2 = b_gate.reshape(nl, 2, GLA_HEADS, GLA_DK).transpose(0, 2, 1, 3).reshape(nl, GLA_HEADS, 1, LANES)
    return w2, b2


def _pack_mlstm(conv_w, conv_b, b_i, b_f, norm_g):
    nl = conv_w.shape[0]
    cw = conv_w.reshape(nl, CONV_W, 2, ML_HEADS, ML_DQK).transpose(0, 3, 1, 2, 4).reshape(nl, ML_HEADS, CONV_W, LANES)
    cw = jnp.pad(cw, ((0, 0), (0, 0), (0, SUBLANES - CONV_W), (0, 0)))
    cb = conv_b.reshape(nl, 2, ML_HEADS, ML_DQK).transpose(0, 2, 1, 3).reshape(nl, ML_HEADS, 1, LANES)
    gates = jnp.stack([b_i[:, 0], b_i[:, 1], b_f[:, 0], b_f[:, 1]], axis=-1)
    gb = jnp.repeat(gates, ML_GATE_REP, axis=-1).reshape(nl, ML_HEADS, 1, LANES)
    return cw, cb, gb, norm_g.reshape(nl, ML_HEADS, 1, ML_DV)


def _pack_ssd(conv_w, conv_b, dt_bias, a_log, d_skip, norm_g):
    nl = conv_w.shape[0]
    bc = M2_GROUPS * M2_DSTATE

    def conv_cols(a):
        lead = a.shape[:-1]
        x = a[..., :M2_WIDTH].reshape(*lead, M2_GROUPS, M2_GROUP_X)
        b = a[..., M2_WIDTH:M2_WIDTH + bc].reshape(*lead, M2_GROUPS, M2_DSTATE)
        c = a[..., M2_WIDTH + bc:].reshape(*lead, M2_GROUPS, M2_DSTATE)
        return jnp.concatenate([x, b, c], axis=-1)

    cw = jnp.pad(conv_cols(conv_w).transpose(0, 2, 1, 3), ((0, 0), (0, 0), (0, SUBLANES - CONV_W), (0, 0)))
    cb = conv_cols(conv_b).reshape(nl, M2_GROUPS, 1, M2_CONV_COLS)

    def per_dir(a):
        a = a.reshape(nl, 2, M2_GROUPS, M2_GROUP_HEADS).transpose(0, 2, 1, 3)
        return jnp.repeat(a, M2_HEADDIM, axis=-1).reshape(nl, M2_GROUPS, 1, M2_DT_COLS)

    dsk = jnp.repeat(d_skip.reshape(nl, M2_GROUPS, M2_GROUP_HEADS), M2_HEADDIM, axis=-1)
    return (cw, cb, per_dir(dt_bias), per_dir(a_log), dsk.reshape(nl, M2_GROUPS, 1, M2_GROUP_X),
            norm_g.reshape(nl, M2_GROUPS, 1, M2_GROUP_X))


def _moe_plan(e_arr, rank_arr, counts_row, n_blocks):
    counts = counts_row[0, :N_EXPERTS].astype(jnp.int32)
    padded = (counts + MOE_TILE - 1) // MOE_TILE * MOE_TILE
    pad_end = jnp.cumsum(padded)
    pad_start = pad_end - padded
    e = e_arr[:, :TOP_K]
    dest = (jnp.take(pad_start, e) + rank_arr[:, :TOP_K].astype(jnp.int32)).reshape(-1)
    block_start = jnp.arange(n_blocks, dtype=jnp.int32) * MOE_TILE
    block_e = jnp.sum((pad_end[None, :] <= block_start[:, None]).astype(jnp.int32), axis=1)
    block_e = jnp.minimum(block_e, N_EXPERTS - 1)
    n_valid = (pad_end[-1:] // MOE_TILE).astype(jnp.int32)
    return dest, block_e, n_valid


def kernel(x, c, ctx, c_ctx, w_mod, b_mod, norm1_g, w_in, gla_w_gate2, gla_b_gate, gla_norm_g, ml_conv_w,
           ml_conv_b, ml_b_i, ml_b_f, ml_norm_g, m2_conv_w, m2_conv_b, m2_dt_bias, m2_A_log, m2_D, m2_norm_g,
           w_out, norm2_g, router_w, router_b, moe_w_gu, moe_b_gu, moe_w_dn, moe_b_dn, final_norm_g):
    nb, n_lat, d = x.shape
    n_ctx = ctx.shape[1]
    n_layers = w_mod.shape[0]
    t = n_lat + n_ctx
    assert d == D_MODEL and n_lat % (GRID_W * SUBLANES) == 0 and n_lat % TOK_TILE == 0 and n_ctx % TOK_TILE == 0
    assert (nb * t) % DISPATCH_TILE == 0

    mod_rows = -(-(nb + 1) // SUBLANES) * SUBLANES
    c_rows = jnp.concatenate([c, c_ctx[None], jnp.zeros((mod_rows - nb - 1, d), F32)], axis=0)
    mod = _mod_table(c_rows, w_mod, b_mod).reshape(n_layers, mod_rows * 6, 1, d)

    w_in_t = jnp.swapaxes(w_in.astype(BF16), 1, 2)
    w_in_a = _pack_w_in(w_in_t, _IN_COLS[:GLA_PACK + ML_PACK])
    w_in_b = _pack_w_in(w_in_t, _IN_COLS[GLA_PACK + ML_PACK:])
    w_out_p = w_out.astype(BF16)
    grid_rows = n_lat // GRID_W
    gla_w2, gla_b2 = _pack_gla(gla_w_gate2, gla_b_gate)
    ml_cw, ml_cb, ml_gb, ml_ng = _pack_mlstm(ml_conv_w, ml_conv_b, ml_b_i, ml_b_f, ml_norm_g)
    m2_cw, m2_cb, m2_dtb, m2_alog, m2_dsk, m2_ng = _pack_ssd(m2_conv_w, m2_conv_b, m2_dt_bias, m2_A_log, m2_D,
                                                            m2_norm_g)
    rw = jnp.pad(router_w, ((0, 0), (0, 0), (0, LANES - N_EXPERTS)))
    rb = jnp.pad(router_b, ((0, 0), (0, LANES - N_EXPERTS)), constant_values=M_INIT).reshape(n_layers, 1, LANES)

    n_assign = nb * t * TOP_K
    n_blocks = n_assign // MOE_TILE + N_EXPERTS
    xb = jnp.zeros((n_blocks * MOE_TILE * ROW_SLABS, LANES), F32)
    yb = jnp.zeros_like(xb)

    xa = jnp.concatenate([x, ctx], axis=1)
    for l in range(n_layers):
        u_gla, u_ml = _in_proj(xa, None, mod[l], norm1_g[l], w_in_a[l], (GLA_PACK, ML_PACK), n_lat)
        x_cm = xa[:, :n_lat].reshape(nb, grid_rows, GRID_W, d).transpose(0, 2, 1, 3).reshape(nb, n_lat, d)
        u_m2, = _in_proj(xa, x_cm, mod[l], norm1_g[l], w_in_b[l], (M2_PACK,), n_lat)
        o_gla = _gla_mixer(u_gla, gla_w2[l], gla_b2[l], gla_norm_g[l].reshape(1, GLA_DV), n_lat)
        o_ml = _mlstm_mixer(u_ml, ml_cw[l], ml_cb[l], ml_gb[l], ml_ng[l], n_lat)
        o_m2 = _ssd_mixer(u_m2, m2_cw[l], m2_cb[l], m2_dtb[l], m2_alog[l], m2_dsk[l], m2_ng[l], n_lat)
        o_m2_lat = (o_m2[:, :n_lat].reshape(nb, GRID_W, grid_rows, M2_WIDTH).transpose(0, 2, 1, 3)
                    .reshape(nb, n_lat, M2_WIDTH))
        xa, h2, logits = _out_proj(xa, o_gla, o_ml, o_m2_lat, o_m2, w_out_p[l], mod[l], norm2_g[l], rw[l], rb[l],
                                   n_lat)
        e_arr, gates, rank_arr, counts = _route(logits)
        dest, block_e, n_valid = _moe_plan(e_arr, rank_arr, counts, n_blocks)
        xb = _dispatch(dest, h2, xb)
        yb = _experts(block_e, n_valid, xb, moe_w_gu[l], moe_b_gu[l], moe_w_dn[l], moe_b_dn[l], yb)
        xa = _combine(dest, yb, xa, gates, mod[l], n_lat)
    return _final_norm(xa, final_norm_g, n_lat)
```

```python
import functools
import math

import numpy as np
import jax
import jax.numpy as jnp
from jax import lax
from jax.experimental import pallas as pl
from jax.experimental.pallas import tpu as pltpu

F32 = jnp.float32
BF16 = jnp.bfloat16

D_MODEL = 1024
GRID_W = 64
GLA_HEADS, GLA_DK, GLA_DV, GLA_RANK = 4, 64, 128, 16
GLA_NORMALIZER = 16.0
ML_HEADS, ML_DQK, ML_DV = 4, 64, 128
M2_HEADS, M2_HEADDIM, M2_GROUPS, M2_DSTATE = 8, 64, 2, 128
CONV_W = 7
CONV_R = CONV_W // 2
N_EXPERTS, TOP_K, D_EXPERT = 32, 4, 1024
SWIGLU_LIMIT, SWIGLU_ALPHA = 7.0, 1.702
EPS = 1e-6
M_INIT = -1e30

GLA_WIDTH = GLA_HEADS * GLA_DV
ML_WIDTH = ML_HEADS * ML_DV
M2_WIDTH = M2_HEADS * M2_HEADDIM
MIX_WIDTH = GLA_WIDTH + ML_WIDTH + M2_WIDTH
GLA_IN = 2 * GLA_HEADS * GLA_DK + 2 * GLA_WIDTH + 2 * GLA_RANK
ML_IN = 2 * ML_HEADS * ML_DQK + 2 * ML_WIDTH + 4 * ML_HEADS
M2_CONV_DIM = M2_WIDTH + 2 * M2_GROUPS * M2_DSTATE
M2_IN = M2_WIDTH + M2_CONV_DIM + 2 * M2_HEADS
IN_WIDTH = GLA_IN + ML_IN + M2_IN

LANES = 128
SUBLANES = 8
CHUNK = 128
TOK_TILE = 256
MOE_TILE = 512
VMEM_LIMIT = 52 * 1024 * 1024

GLA_HEAD_COLS = 3 * LANES
GLA_PACK = GLA_HEADS * GLA_HEAD_COLS + LANES
ML_HEAD_COLS = 4 * LANES
ML_PACK = ML_HEADS * ML_HEAD_COLS
M2_GROUP_HEADS = M2_HEADS // M2_GROUPS
M2_GROUP_X = M2_GROUP_HEADS * M2_HEADDIM
M2_PAIRS = M2_GROUP_HEADS // 2
M2_DT_COLS = 2 * M2_PAIRS * LANES
M2_GROUP_COLS = 2 * M2_GROUP_X + 2 * M2_DSTATE + M2_DT_COLS
M2_PACK = M2_GROUPS * M2_GROUP_COLS
IN_PACK = GLA_PACK + ML_PACK + M2_PACK
ML_GATE_REP = LANES // 4


def _in_proj_column_map():
    cols = []
    qk = GLA_HEADS * GLA_DK
    for h in range(GLA_HEADS):
        cols += list(range(h * GLA_DK, (h + 1) * GLA_DK))
        cols += list(range(qk + h * GLA_DK, qk + (h + 1) * GLA_DK))
        cols += list(range(2 * qk + h * GLA_DV, 2 * qk + (h + 1) * GLA_DV))
        cols += list(range(2 * qk + GLA_WIDTH + h * GLA_DV, 2 * qk + GLA_WIDTH + (h + 1) * GLA_DV))
    lr0 = 2 * qk + 2 * GLA_WIDTH
    cols += list(range(lr0, lr0 + 2 * GLA_RANK)) + [-1] * (LANES - 2 * GLA_RANK)
    a0 = GLA_IN
    qk = ML_HEADS * ML_DQK
    g0 = a0 + 2 * qk + 2 * ML_WIDTH
    for h in range(ML_HEADS):
        cols += list(range(a0 + h * ML_DQK, a0 + (h + 1) * ML_DQK))
        cols += list(range(a0 + qk + h * ML_DQK, a0 + qk + (h + 1) * ML_DQK))
        cols += list(range(a0 + 2 * qk + h * ML_DV, a0 + 2 * qk + (h + 1) * ML_DV))
        cols += list(range(a0 + 2 * qk + ML_WIDTH + h * ML_DV, a0 + 2 * qk + ML_WIDTH + (h + 1) * ML_DV))
        for gate in range(4):
            cols += [g0 + gate * ML_HEADS + h] * ML_GATE_REP
    a1 = GLA_IN + ML_IN
    x0 = a1 + M2_WIDTH
    dt0 = a1 + M2_WIDTH + M2_CONV_DIM
    for g in range(M2_GROUPS):
        cols += list(range(a1 + g * M2_GROUP_X, a1 + (g + 1) * M2_GROUP_X))
        cols += list(range(x0 + g * M2_GROUP_X, x0 + (g + 1) * M2_GROUP_X))
        cols += list(range(x0 + M2_WIDTH + g * M2_DSTATE, x0 + M2_WIDTH + (g + 1) * M2_DSTATE))
        cols += list(range(x0 + M2_WIDTH + M2_GROUPS * M2_DSTATE + g * M2_DSTATE,
                           x0 + M2_WIDTH + M2_GROUPS * M2_DSTATE + (g + 1) * M2_DSTATE))
        for d in range(2):
            for h in range(M2_GROUP_HEADS):
                cols += [dt0 + d * M2_HEADS + g * M2_GROUP_HEADS + h] * M2_HEADDIM
    cols = np.asarray(cols, np.int32)
    assert cols.shape == (IN_PACK,)
    return cols


_IN_COLS = _in_proj_column_map()


def _bdot(a, b):
    return jnp.dot(a.astype(BF16), b.astype(BF16), preferred_element_type=F32)


def _bdot_nt(a, b):
    return lax.dot_general(a.astype(BF16), b.astype(BF16), (((1,), (1,)), ((), ())),
                           preferred_element_type=F32)


def _bdot_tn(a, b):
    return lax.dot_general(a.astype(BF16), b.astype(BF16), (((0,), (0,)), ((), ())),
                           preferred_element_type=F32)


def _split2(a):
    hi = a.astype(BF16)
    lo = (a - hi.astype(F32)).astype(BF16)
    return hi, lo


def _split3(a):
    hi = a.astype(BF16)
    r = a - hi.astype(F32)
    mid = r.astype(BF16)
    lo = (r - mid.astype(F32)).astype(BF16)
    return hi, mid, lo


def _hdot(a, b):
    ah, al = _split2(a)
    bh, bl = _split2(b)
    d = functools.partial(jnp.dot, preferred_element_type=F32)
    return d(ah, bh) + (d(ah, bl) + d(al, bh))


def _tri_dot(tri, a):
    hi, mid, lo = _split3(a)
    d = functools.partial(jnp.dot, preferred_element_type=F32)
    return d(tri, hi) + (d(tri, mid) + d(tri, lo))


def _rms(x):
    return x * lax.rsqrt(jnp.mean(x * x, axis=-1, keepdims=True) + EPS)


def _sigmoid(x):
    return 1.0 / (1.0 + jnp.exp(-x))


def _silu(x):
    return x * _sigmoid(x)


def _log_sigmoid(x):
    return jnp.minimum(x, 0.0) - jnp.log(1.0 + jnp.exp(-jnp.abs(x)))


def _softplus(x):
    return jnp.maximum(x, 0.0) + jnp.log(1.0 + jnp.exp(-jnp.abs(x)))


def _lane_iota(shape):
    return lax.broadcasted_iota(jnp.int32, shape, len(shape) - 1)


def _row_iota(shape):
    return lax.broadcasted_iota(jnp.int32, shape, len(shape) - 2)


def _lane_rep(x, lo, width):
    lane = _lane_iota(x.shape)
    y = jnp.where((lane >= lo) & (lane < lo + width), x, 0.0)
    w = width
    while w < LANES:
        y = y + pltpu.roll(y, w, axis=1)
        w *= 2
    return y


def _swap_halves(x):
    return pltpu.roll(x, LANES // 2, axis=1)


def _tri_incl(n):
    r = lax.broadcasted_iota(jnp.int32, (n, n), 0)
    c = lax.broadcasted_iota(jnp.int32, (n, n), 1)
    return (r >= c).astype(BF16)


def _mod_kernel(c_ref, w_ref, b_ref, o_ref):
    o_ref[...] = _hdot(_silu(c_ref[...]), w_ref[...]) + b_ref[...]


def _mod_table(c_rows, w_mod, b_mod):
    n_layers, d, n6 = w_mod.shape
    r = c_rows.shape[0]
    tn = 1536
    return pl.pallas_call(
        _mod_kernel,
        grid=(n_layers, n6 // tn),
        in_specs=[pl.BlockSpec((r, d), lambda l, j: (0, 0)),
                  pl.BlockSpec((None, d, tn), lambda l, j: (l, 0, j)),
                  pl.BlockSpec((None, 1, tn), lambda l, j: (l, 0, j))],
        out_specs=pl.BlockSpec((None, r, tn), lambda l, j: (l, 0, j)),
        out_shape=jax.ShapeDtypeStruct((n_layers, r, n6), F32),
        compiler_params=pltpu.CompilerParams(dimension_semantics=("arbitrary", "arbitrary"),
                                             vmem_limit_bytes=VMEM_LIMIT),
        name="adaln_mod",
    )(c_rows, w_mod, b_mod.reshape(n_layers, 1, n6))


def _inproj_kernel(*refs, n_lat_tiles, widths, two_sources):
    if two_sources:
        xl_ref, xc_ref, sh_ref, sc_ref, g_ref, w_ref = refs[:6]
        outs = refs[6:]
        x = jnp.where(pl.program_id(1) < n_lat_tiles, xl_ref[...], xc_ref[...])
    else:
        x_ref, sh_ref, sc_ref, g_ref, w_ref = refs[:5]
        outs = refs[5:]
        x = x_ref[...]
    h = _rms(x) * g_ref[...]
    h = (h * (1.0 + sc_ref[...]) + sh_ref[...]).astype(BF16)
    c0 = 0
    for o_ref, width in zip(outs, widths):
        o_ref[...] = jnp.dot(h, w_ref[:, c0:c0 + width], preferred_element_type=F32)
        c0 += width


def _mod_spec(which, n_lat_tiles, n_batch):
    def imap(b, t):
        row = jnp.where(t < n_lat_tiles, b, n_batch)
        return (row * 6 + which, 0, 0)
    return pl.BlockSpec((None, 1, D_MODEL), imap)


def _in_proj(xa, x_lat, mod3, norm_g, w_pack, widths, n_lat):
    nb, t, d = xa.shape
    n_lat_tiles = n_lat // TOK_TILE
    tok = lambda width: pl.BlockSpec((None, TOK_TILE, width), lambda b, i: (b, i, 0))
    if x_lat is None:
        x_specs, x_args = [tok(d)], (xa,)
    else:
        x_specs = [pl.BlockSpec((None, TOK_TILE, d), lambda b, i: (b, jnp.minimum(i, n_lat_tiles - 1), 0)),
                   pl.BlockSpec((None, TOK_TILE, d), lambda b, i: (b, jnp.maximum(i, n_lat_tiles), 0))]
        x_args = (x_lat, xa)
    return pl.pallas_call(
        functools.partial(_inproj_kernel, n_lat_tiles=n_lat_tiles, widths=widths, two_sources=x_lat is not None),
        grid=(nb, t // TOK_TILE),
        in_specs=x_specs + [_mod_spec(0, n_lat_tiles, nb), _mod_spec(1, n_lat_tiles, nb),
                            pl.BlockSpec((1, d), lambda b, i: (0, 0)),
                            pl.BlockSpec((d, sum(widths)), lambda b, i: (0, 0))],
        out_specs=[tok(w) for w in widths],
        out_shape=[jax.ShapeDtypeStruct((nb, t, w), F32) for w in widths],
        compiler_params=pltpu.CompilerParams(dimension_semantics=("arbitrary", "arbitrary"),
                                             vmem_limit_bytes=VMEM_LIMIT),
        name="in_proj",
    )(*x_args, mod3, mod3, norm_g.reshape(1, d), w_pack)


def _chunk_rows(c):
    return pl.ds(pl.multiple_of(c * CHUNK, CHUNK), CHUNK)


def _row_bcast(row):
    return jnp.broadcast_to(row, (SUBLANES, row.shape[-1]))


CHUNK_UNROLL = 3


def _for_chunks(n, body):
    assert n % CHUNK_UNROLL == 0

    def step(i, carry):
        for j in range(CHUNK_UNROLL):
            body(i * CHUNK_UNROLL + j)
        return carry

    lax.fori_loop(0, n // CHUNK_UNROLL, step, 0)


CUMSUM_GROUP = 6


def _chunk_cumsums(n, tri, load, emit):
    assert n % CUMSUM_GROUP == 0
    for c0 in range(0, n, CUMSUM_GROUP):
        xs = [load(c) for c in range(c0, c0 + CUMSUM_GROUP)]
        width = xs[0].shape[1]
        p = _tri_dot(tri, jnp.concatenate(xs, axis=1))
        for j in range(CUMSUM_GROUP):
            emit(c0 + j, xs[j], p[:, j * width:(j + 1) * width])


def _gla_kernel(u_ref, lr_ref, w2_ref, b2_ref, ng_ref, o_ref,
                cum_ref, oin_ref, q2_ref, ds_ref, a_ref, stf_ref, stb_ref, *, n_lat_chunks):
    t = u_ref.shape[0]
    n = t // CHUNK
    tri = _tri_incl(CHUNK)
    fwd = _lane_iota((CHUNK, LANES)) < LANES // 2
    row = _row_iota((CHUNK, CHUNK))
    col = _lane_iota((CHUNK, CHUNK))
    w2 = w2_ref[...]
    b2 = b2_ref[...]

    def log_decay(i, carry):
        rows = pl.ds(pl.multiple_of(i * TOK_TILE, TOK_TILE), TOK_TILE)
        cum_ref[rows, :] = _log_sigmoid(_hdot(lr_ref[rows, :], w2) + b2) * (1.0 / GLA_NORMALIZER)
        return carry

    lax.fori_loop(0, t // TOK_TILE, log_decay, 0)

    def emit_cum(c, la, p):
        tot = p[CHUNK - 1:CHUNK, :]
        cum_ref[c * CHUNK:(c + 1) * CHUNK, :] = jnp.where(fwd, p, tot - p + la)
        a_ref[c] = _row_bcast(jnp.exp(tot))

    _chunk_cumsums(n, tri, lambda c: cum_ref[c * CHUNK:(c + 1) * CHUNK, :], emit_cum)

    def local(c):
        rows = _chunk_rows(c)
        qk = u_ref[rows, 0:LANES]
        v = u_ref[rows, LANES:2 * LANES]
        cum = cum_ref[rows, :]
        tot = jnp.where(fwd[0:1], cum[CHUNK - 1:CHUNK, :], cum[0:1, :])
        mid = cum[CHUNK // 2:CHUNK // 2 + 1, :]
        sw = _swap_halves(qk)
        qq = jnp.where(fwd, qk, sw) * (GLA_DK ** -0.5)
        kk = jnp.where(fwd, sw, qk)
        qe = qq * jnp.exp(cum - mid)
        ke = kk * jnp.exp(mid - cum)
        af = _bdot_nt(jnp.where(fwd, qe, 0.0), ke)
        ab = _bdot_nt(jnp.where(fwd, 0.0, qe), ke)
        attn = jnp.where(row >= col, af, 0.0) + jnp.where(col >= row, ab, 0.0)
        oin_ref[rows, :] = _bdot(attn, v)
        q2_ref[rows, :] = qq * jnp.exp(cum)
        ds_ref[c] = _bdot_tn(v, kk * jnp.exp(tot - cum))

    _for_chunks(n, local)

    def scan(s, st):
        f = lax.rem(s + n_lat_chunks, n)
        g = n - 1 - s
        stf_ref[f] = st
        stb_ref[g] = st
        a = jnp.where(fwd[0:1], a_ref[f][0:1], a_ref[g][0:1])
        return st * a + jnp.where(fwd, ds_ref[f], ds_ref[g])

    lax.fori_loop(0, n, scan, jnp.zeros((GLA_DV, LANES), F32))

    def finish(c):
        rows = _chunk_rows(c)
        st = jnp.where(fwd, stf_ref[c], stb_ref[c])
        o = oin_ref[rows, :] + _bdot_nt(q2_ref[rows, :], st)
        o = _rms(o) * ng_ref[...]
        o_ref[rows, :] = (o * _silu(u_ref[rows, 2 * LANES:3 * LANES])).astype(o_ref.dtype)

    _for_chunks(n, finish)


def _gla_mixer(u, w2, b2, ng, n_lat):
    nb, t, _ = u.shape
    n = t // CHUNK
    return pl.pallas_call(
        functools.partial(_gla_kernel, n_lat_chunks=n_lat // CHUNK),
        grid=(nb, GLA_HEADS),
        in_specs=[pl.BlockSpec((None, t, GLA_HEAD_COLS), lambda b, h: (b, 0, h)),
                  pl.BlockSpec((None, t, LANES), lambda b, h: (b, 0, GLA_HEADS * GLA_HEAD_COLS // LANES)),
                  pl.BlockSpec((None, LANES, LANES), lambda b, h: (h, 0, 0)),
                  pl.BlockSpec((None, 1, LANES), lambda b, h: (h, 0, 0)),
                  pl.BlockSpec((1, LANES), lambda b, h: (0, 0))],
        out_specs=pl.BlockSpec((None, t, GLA_DV), lambda b, h: (b, 0, h)),
        out_shape=jax.ShapeDtypeStruct((nb, t, GLA_WIDTH), BF16),
        scratch_shapes=[pltpu.VMEM((t, LANES), F32), pltpu.VMEM((t, LANES), F32), pltpu.VMEM((t, LANES), F32),
                        pltpu.VMEM((n, GLA_DV, LANES), F32), pltpu.VMEM((n, SUBLANES, LANES), F32),
                        pltpu.VMEM((n, GLA_DV, LANES), F32), pltpu.VMEM((n, GLA_DV, LANES), F32)],
        compiler_params=pltpu.CompilerParams(dimension_semantics=("arbitrary", "arbitrary"),
                                             vmem_limit_bytes=VMEM_LIMIT),
        name="gla_mixer",
    )(u, u, w2, b2, ng)


def _zero_pads(pad_ref, n_lat, t):
    z = jnp.zeros((SUBLANES, pad_ref.shape[1]), F32)
    pad_ref[0:SUBLANES, :] = z
    pad_ref[SUBLANES + n_lat:2 * SUBLANES + n_lat, :] = z
    pad_ref[2 * SUBLANES + t:3 * SUBLANES + t, :] = z


def _pad_base(c, n_lat_chunks):
    return c * CHUNK + (SUBLANES if c < n_lat_chunks else 2 * SUBLANES)


def _conv_silu(pad_ref, w_ref, b_ref, out_ref, n, n_lat_chunks):
    width = pad_ref.shape[1]
    for c in range(n):
        base = _pad_base(c, n_lat_chunks)
        for l0 in range(0, width, LANES):
            acc = None
            for j in range(CONV_W):
                term = w_ref[j:j + 1, l0:l0 + LANES] * pad_ref[base + j - CONV_R:base + j - CONV_R + CHUNK,
                                                               l0:l0 + LANES]
                acc = term if acc is None else acc + term
            out_ref[c * CHUNK:(c + 1) * CHUNK, l0:l0 + LANES] = _silu(acc + b_ref[:, l0:l0 + LANES])


def _mlstm_kernel(u_ref, cw_ref, cb_ref, gb_ref, ng_ref, o_ref,
                  pad_ref, qk_ref, fc_ref, rc_ref, dc_ref, tot_ref, mloc_ref, stm_ref, *, n_lat_chunks):
    t = u_ref.shape[0]
    n = t // CHUNK
    n_lat = n_lat_chunks * CHUNK
    tri = _tri_incl(CHUNK)
    lane = _lane_iota((CHUNK, LANES))
    hi_half = lane >= LANES // 2
    row = _row_iota((CHUNK, CHUNK))
    col = _lane_iota((CHUNK, CHUNK))
    masks = (row >= col, col >= row)
    ones = jnp.ones((CHUNK, LANES), F32)

    _zero_pads(pad_ref, n_lat, t)
    pad_ref[SUBLANES:SUBLANES + n_lat, :] = u_ref[0:n_lat, 0:LANES]
    pad_ref[2 * SUBLANES + n_lat:2 * SUBLANES + t, :] = u_ref[n_lat:t, 0:LANES]
    _conv_silu(pad_ref, cw_ref, cb_ref, qk_ref, n, n_lat_chunks)

    def khat_of(qk):
        return jnp.where(hi_half, qk, 0.0) * (ML_DQK ** -0.5)

    sel_r = lax.broadcasted_iota(jnp.int32, (LANES, 4 * LANES), 0)
    sel_c = lax.broadcasted_iota(jnp.int32, (LANES, 4 * LANES), 1)
    spread = (sel_r == (sel_c // LANES) * ML_GATE_REP).astype(BF16)

    def load_gates(c):
        rows = slice(c * CHUNK, (c + 1) * CHUNK)
        g = u_ref[rows, 3 * LANES:4 * LANES] + gb_ref[...]
        g = jnp.where(hi_half, _log_sigmoid(g), g)
        hi, mid, lo = _split3(g)
        d = functools.partial(jnp.dot, preferred_element_type=F32)
        wide = d(hi, spread) + (d(mid, spread) + d(lo, spread))
        rc_ref[0, rows, :] = wide[:, 0:LANES]
        rc_ref[1, rows, :] = wide[:, LANES:2 * LANES]
        return wide[:, 2 * LANES:]

    def emit_gates(c, lf, p):
        rows = slice(c * CHUNK, (c + 1) * CHUNK)
        tot_b = p[CHUNK - 1:CHUNK, LANES:]
        f_dir = (p[:, 0:LANES], tot_b - p[:, LANES:] + lf[:, LANES:])
        for d in range(2):
            fc_ref[d, rows, :] = f_dir[d]
            rc_ref[d, rows, :] = rc_ref[d, rows, :] - f_dir[d]

    _chunk_cumsums(n, tri, load_gates, emit_gates)

    def local(c):
        rows = _chunk_rows(c)
        khat = khat_of(qk_ref[rows, :])
        vaug = jnp.concatenate([u_ref[rows, LANES:2 * LANES], ones], axis=1)
        for d in range(2):
            fc = fc_ref[d, rows, :]
            tt = fc[CHUNK - 1:CHUNK, :] if d == 0 else fc[0:1, :]
            gend = tt + rc_ref[d, rows, :]
            mloc = jnp.max(gend, axis=0, keepdims=True)
            dc_ref[d, c] = _bdot_tn(khat * jnp.exp(gend - mloc), vaug)
            tot_ref[d, c] = _row_bcast(tt)
            mloc_ref[d, c] = _row_bcast(mloc)

    _for_chunks(n, local)

    def scan(s, carry):
        new = []
        for d, idx in ((0, lax.rem(s + n_lat_chunks, n)), (1, n - 1 - s)):
            cst, m = carry[d]
            inc = dc_ref[d, idx]
            dc_ref[d, idx] = cst
            stm_ref[d, idx] = _row_bcast(m)
            tt = tot_ref[d, idx][0:1]
            ml = mloc_ref[d, idx][0:1]
            m_new = jnp.maximum(tt + m, ml)
            a = jnp.exp(tt + m - m_new)[:, 0:1]
            sc = jnp.exp(ml - m_new)[:, 0:1]
            new.append((a * cst + sc * inc, m_new))
        return tuple(new)

    init = (jnp.zeros((LANES, 2 * LANES), F32), jnp.full((1, LANES), M_INIT, F32))
    lax.fori_loop(0, n, scan, (init, init))

    def finish(c):
        rows = _chunk_rows(c)
        qk = qk_ref[rows, :]
        qhat = jnp.where(hi_half, _swap_halves(qk), 0.0)
        s_qk = _bdot_nt(qhat, khat_of(qk))
        vaug = jnp.concatenate([u_ref[rows, LANES:2 * LANES], ones], axis=1)
        rc_t = jnp.where(hi_half, rc_ref[1, rows, :], rc_ref[0, rows, :]).T
        h = None
        for d in range(2):
            fc = fc_ref[d, rows, :]
            rc_row = rc_t[d * (LANES // 2):d * (LANES // 2) + 1, :]
            dlog = jnp.where(masks[d], fc + rc_row, -jnp.inf)
            inter = fc + stm_ref[d, c][0:1]
            m_row = jnp.maximum(inter, jnp.max(dlog, axis=1, keepdims=True))
            w_inter = jnp.exp(inter - m_row)
            nd = (_bdot(s_qk * jnp.exp(dlog - m_row), vaug)
                  + jnp.concatenate([w_inter, w_inter], axis=1) * _bdot(qhat, dc_ref[d, c]))
            hd = nd[:, 0:LANES] / jnp.maximum(jnp.abs(nd[:, LANES:]), jnp.exp(-m_row))
            h = hd if h is None else h + hd
        h = _rms(h) * ng_ref[...]
        o_ref[rows, :] = (h * _sigmoid(u_ref[rows, 2 * LANES:3 * LANES])).astype(o_ref.dtype)

    _for_chunks(n, finish)


def _mlstm_mixer(u, cw, cb, gb, ng, n_lat):
    nb, t, _ = u.shape
    n = t // CHUNK
    head = lambda rows: pl.BlockSpec((None, rows, LANES), lambda b, h: (h, 0, 0))
    return pl.pallas_call(
        functools.partial(_mlstm_kernel, n_lat_chunks=n_lat // CHUNK),
        grid=(nb, ML_HEADS),
        in_specs=[pl.BlockSpec((None, t, ML_HEAD_COLS), lambda b, h: (b, 0, h)),
                  head(SUBLANES), head(1), head(1), head(1)],
        out_specs=pl.BlockSpec((None, t, ML_DV), lambda b, h: (b, 0, h)),
        out_shape=jax.ShapeDtypeStruct((nb, t, ML_WIDTH), BF16),
        scratch_shapes=[pltpu.VMEM((t + 3 * SUBLANES, LANES), F32), pltpu.VMEM((t, LANES), F32),
                        pltpu.VMEM((2, t, LANES), F32), pltpu.VMEM((2, t, LANES), F32),
                        pltpu.VMEM((2, n, LANES, 2 * LANES), F32),
                        pltpu.VMEM((2, n, SUBLANES, LANES), F32), pltpu.VMEM((2, n, SUBLANES, LANES), F32),
                        pltpu.VMEM((2, n, SUBLANES, LANES), F32)],
        compiler_params=pltpu.CompilerParams(dimension_semantics=("arbitrary", "arbitrary"),
                                             vmem_limit_bytes=VMEM_LIMIT),
        name="mlstm_mixer",
    )(u, cw, cb, gb, ng)


M2_CONV_COLS = M2_GROUP_X + 2 * M2_DSTATE
M2_X0 = M2_GROUP_X
M2_DT0 = M2_X0 + M2_CONV_COLS


def _ssd_kernel(u_ref, cw_ref, cb_ref, dtb_ref, alog_ref, dsk_ref, ng_ref, o_ref,
                pad_ref, xc_ref, dt_ref, dh_ref, a_ref, y_ref, *, n_lat_chunks):
    t = u_ref.shape[0]
    n = t // CHUNK
    n_lat = n_lat_chunks * CHUNK
    n_state = 2 * M2_PAIRS
    tri = _tri_incl(CHUNK)
    lo_half = _lane_iota((CHUNK, LANES)) < LANES // 2
    row = _row_iota((CHUNK, CHUNK))
    col = _lane_iota((CHUNK, CHUNK))
    masks = (row >= col, col >= row)
    fwd_cols = _lane_iota((CHUNK, M2_DT_COLS)) < M2_DT_COLS // 2
    a_row = -jnp.exp(alog_ref[...])

    _zero_pads(pad_ref, n_lat, t)
    pad_ref[SUBLANES:SUBLANES + n_lat, :] = u_ref[0:n_lat, M2_X0:M2_DT0]
    pad_ref[2 * SUBLANES + n_lat:2 * SUBLANES + t, :] = u_ref[n_lat:t, M2_X0:M2_DT0]
    _conv_silu(pad_ref, cw_ref, cb_ref, xc_ref, n, n_lat_chunks)

    cum_ref = pad_ref

    def load_decay(c):
        rows = slice(c * CHUNK, (c + 1) * CHUNK)
        dt = _softplus(u_ref[rows, M2_DT0:] + dtb_ref[...])
        dt_ref[rows, :] = dt
        return dt * a_row

    def emit_decay(c, da, p):
        tot = p[CHUNK - 1:CHUNK, :]
        cum_ref[c * CHUNK:(c + 1) * CHUNK, :] = jnp.where(fwd_cols, p, tot - p + da)

    _chunk_cumsums(n, tri, load_decay, emit_decay)

    def local(c):
        rows = _chunk_rows(c)
        dt = dt_ref[rows, :]
        cum = cum_ref[rows, :]
        tot = jnp.where(fwd_cols[0:1], cum[CHUNK - 1:CHUNK, :], cum[0:1, :])
        x = xc_ref[rows, 0:M2_GROUP_X]
        bm = xc_ref[rows, M2_GROUP_X:M2_GROUP_X + M2_DSTATE]
        cm = xc_ref[rows, M2_GROUP_X + M2_DSTATE:]
        g = _bdot_nt(cm, bm)
        y = [None] * M2_PAIRS
        for d in range(2):
            for p in range(M2_PAIRS):
                k = d * M2_PAIRS + p
                sl = slice(k * LANES, (k + 1) * LANES)
                fp = cum[:, sl]
                tt = tot[:, sl]
                xdt = x[:, p * LANES:(p + 1) * LANES] * dt[:, sl]
                dh_ref[c, k] = _bdot_tn(bm, jnp.exp(tt - fp) * xdt)
                a_ref[c, k] = _row_bcast(jnp.exp(tt))
                sw = _swap_halves(fp)
                fpt = fp.T
                halves = []
                for hh, fh in enumerate((jnp.where(lo_half, fp, sw), jnp.where(lo_half, sw, fp))):
                    f_row = fpt[hh * M2_HEADDIM:hh * M2_HEADDIM + 1, :]
                    dec = jnp.exp(jnp.where(masks[d], fh - f_row, -jnp.inf))
                    halves.append(_bdot(g * dec, xdt))
                yp = jnp.where(lo_half, halves[0], halves[1])
                y[p] = yp if y[p] is None else y[p] + yp
        for p in range(M2_PAIRS):
            y_ref[p, rows, :] = y[p]

    _for_chunks(n, local)

    def scan(s, carry):
        f = lax.rem(s + n_lat_chunks, n)
        g = n - 1 - s
        new = []
        for k in range(n_state):
            idx = f if k < M2_PAIRS else g
            inc = dh_ref[idx, k]
            dh_ref[idx, k] = carry[k]
            new.append(carry[k] * a_ref[idx, k][0:1] + inc)
        return tuple(new)

    lax.fori_loop(0, n, scan, tuple(jnp.zeros((M2_DSTATE, LANES), F32) for _ in range(n_state)))

    def finish(c):
        rows = _chunk_rows(c)
        cum = cum_ref[rows, :]
        cm = xc_ref[rows, M2_GROUP_X + M2_DSTATE:]
        y = [y_ref[p, rows, :] for p in range(M2_PAIRS)]
        for k in range(n_state):
            p = k % M2_PAIRS
            y[p] = y[p] + jnp.exp(cum[:, k * LANES:(k + 1) * LANES]) * _bdot(cm, dh_ref[c, k])
        y = jnp.concatenate(y, axis=1) + dsk_ref[...] * xc_ref[rows, 0:M2_GROUP_X]
        y = _rms(y * _silu(u_ref[rows, 0:M2_X0])) * ng_ref[...]
        o_ref[rows, :] = y.astype(o_ref.dtype)

    _for_chunks(n, finish)


def _ssd_mixer(u, cw, cb, dtb, alog, dsk, ng, n_lat):
    nb, t, _ = u.shape
    n = t // CHUNK
    grp = lambda rows, width: pl.BlockSpec((None, rows, width), lambda b, g: (g, 0, 0))
    return pl.pallas_call(
        functools.partial(_ssd_kernel, n_lat_chunks=n_lat // CHUNK),
        grid=(nb, M2_GROUPS),
        in_specs=[pl.BlockSpec((None, t, M2_GROUP_COLS), lambda b, g: (b, 0, g)),
                  grp(SUBLANES, M2_CONV_COLS), grp(1, M2_CONV_COLS), grp(1, M2_DT_COLS), grp(1, M2_DT_COLS),
                  grp(1, M2_GROUP_X), grp(1, M2_GROUP_X)],
        out_specs=pl.BlockSpec((None, t, M2_GROUP_X), lambda b, g: (b, 0, g)),
        out_shape=jax.ShapeDtypeStruct((nb, t, M2_WIDTH), BF16),
        scratch_shapes=[pltpu.VMEM((t + 3 * SUBLANES, M2_CONV_COLS), F32), pltpu.VMEM((t, M2_CONV_COLS), F32),
                        pltpu.VMEM((t, M2_DT_COLS), F32),
                        pltpu.VMEM((n, 2 * M2_PAIRS, M2_DSTATE, LANES), F32),
                        pltpu.VMEM((n, 2 * M2_PAIRS, SUBLANES, LANES), F32),
                        pltpu.VMEM((M2_PAIRS, t, LANES), F32)],
        compiler_params=pltpu.CompilerParams(dimension_semantics=("arbitrary", "arbitrary"),
                                             vmem_limit_bytes=VMEM_LIMIT),
        name="ssd_mixer",
    )(u, cw, cb, dtb, alog, dsk, ng)


ROW_SLABS = D_MODEL // LANES


def _slab(c, n_rows):
    return pl.ds(c, n_rows, stride=ROW_SLABS)


def _outproj_kernel(x_ref, og_ref, om_ref, osl_ref, osc_ref, w_ref, gate_ref, sh_ref, sc_ref, g2_ref, rw_ref,
                    rb_ref, xo_ref, h2_ref, lg_ref, *, n_lat_tiles):
    o_ssd = jnp.where(pl.program_id(1) < n_lat_tiles, osl_ref[...], osc_ref[...])
    mix = (jnp.dot(og_ref[...], w_ref[0:GLA_WIDTH, :], preferred_element_type=F32)
           + jnp.dot(om_ref[...], w_ref[GLA_WIDTH:GLA_WIDTH + ML_WIDTH, :], preferred_element_type=F32)
           + jnp.dot(o_ssd, w_ref[GLA_WIDTH + ML_WIDTH:, :], preferred_element_type=F32))
    x = x_ref[...] + gate_ref[...] * mix
    xo_ref[...] = x
    h2 = (_rms(x) * g2_ref[...]) * (1.0 + sc_ref[...]) + sh_ref[...]
    for c in range(ROW_SLABS):
        h2_ref[_slab(c, TOK_TILE), :] = h2[:, c * LANES:(c + 1) * LANES]
    lg_ref[...] = _hdot(h2, rw_ref[...]) + rb_ref[...]


def _out_proj(xa, o_gla, o_ml, o_m2_lat, o_m2, w_out, mod3, norm_g, rw, rb, n_lat):
    nb, t, d = xa.shape
    nt = t // TOK_TILE
    n_lat_tiles = n_lat // TOK_TILE
    tok = lambda width: pl.BlockSpec((None, TOK_TILE, width), lambda b, i: (b, i, 0))
    const = lambda r, c: pl.BlockSpec((r, c), lambda b, i: (0, 0))
    mod = lambda which: _mod_spec(which, n_lat_tiles, nb)
    ssd_lat = pl.BlockSpec((None, TOK_TILE, M2_WIDTH), lambda b, i: (b, jnp.minimum(i, n_lat_tiles - 1), 0))
    ssd_ctx = pl.BlockSpec((None, TOK_TILE, M2_WIDTH), lambda b, i: (b, jnp.maximum(i, n_lat_tiles), 0))
    return pl.pallas_call(
        functools.partial(_outproj_kernel, n_lat_tiles=n_lat_tiles),
        grid=(nb, nt),
        in_specs=[tok(d), tok(GLA_WIDTH), tok(ML_WIDTH), ssd_lat, ssd_ctx, const(MIX_WIDTH, d),
                  mod(2), mod(3), mod(4), const(1, d), const(d, LANES), const(1, LANES)],
        out_specs=[tok(d),
                   pl.BlockSpec((TOK_TILE * ROW_SLABS, LANES), lambda b, i: (b * nt + i, 0)),
                   pl.BlockSpec((TOK_TILE, LANES), lambda b, i: (b * nt + i, 0))],
        out_shape=[jax.ShapeDtypeStruct((nb, t, d), F32),
                   jax.ShapeDtypeStruct((nb * t * ROW_SLABS, LANES), F32),
                   jax.ShapeDtypeStruct((nb * t, LANES), F32)],
        compiler_params=pltpu.CompilerParams(dimension_semantics=("arbitrary", "arbitrary"),
                                             vmem_limit_bytes=VMEM_LIMIT),
        name="out_proj",
    )(xa, o_gla, o_ml, o_m2_lat, o_m2, w_out, mod3, mod3, mod3, norm_g.reshape(1, d), rw, rb)


ROUTE_TILE = 256


def _route_kernel(lg_ref, e_ref, gt_ref, rk_ref, cnt_ref, base_ref):
    @pl.when(pl.program_id(0) == 0)
    def _():
        base_ref[...] = jnp.zeros_like(base_ref)

    lane = _lane_iota((ROUTE_TILE, LANES))
    work = lg_ref[...]
    vals, idxs = [], []
    for _ in range(TOP_K):
        m = jnp.max(work, axis=1, keepdims=True)
        idx = jnp.min(jnp.where(work == m, lane, LANES), axis=1, keepdims=True)
        vals.append(m)
        idxs.append(idx)
        work = jnp.where(lane == idx, -jnp.inf, work)
    ex = [jnp.exp(v - vals[0]) for v in vals]
    inv = 1.0 / (ex[0] + ex[1] + ex[2] + ex[3])
    r = _row_iota((ROUTE_TILE, ROUTE_TILE))
    c = _lane_iota((ROUTE_TILE, ROUTE_TILE))
    earlier = (r > c).astype(BF16)
    base = base_ref[0:1, :]
    e_out = jnp.zeros((ROUTE_TILE, LANES), jnp.int32)
    g_out = jnp.zeros((ROUTE_TILE, LANES), F32)
    r_out = jnp.zeros((ROUTE_TILE, LANES), F32)
    for k in range(TOP_K):
        onehot = (lane == idxs[k]).astype(F32)
        within = jnp.dot(earlier, onehot.astype(BF16), preferred_element_type=F32)
        rank = jnp.sum((base + within) * onehot, axis=1, keepdims=True)
        base = base + jnp.sum(onehot, axis=0, keepdims=True)
        e_out = jnp.where(lane == k, idxs[k], e_out)
        g_out = jnp.where(lane == k, ex[k] * inv, g_out)
        r_out = jnp.where(lane == k, rank, r_out)
    base_ref[...] = _row_bcast(base)
    cnt_ref[...] = _row_bcast(base)
    e_ref[...] = e_out
    gt_ref[...] = g_out
    rk_ref[...] = r_out


def _route(logits):
    n_tok = logits.shape[0]
    tile = pl.BlockSpec((ROUTE_TILE, LANES), lambda i: (i, 0))
    return pl.pallas_call(
        _route_kernel,
        grid=(n_tok // ROUTE_TILE,),
        in_specs=[tile],
        out_specs=[tile, tile, tile, pl.BlockSpec((SUBLANES, LANES), lambda i: (0, 0))],
        out_shape=[jax.ShapeDtypeStruct((n_tok, LANES), jnp.int32),
                   jax.ShapeDtypeStruct((n_tok, LANES), F32),
                   jax.ShapeDtypeStruct((n_tok, LANES), F32),
                   jax.ShapeDtypeStruct((SUBLANES, LANES), F32)],
        scratch_shapes=[pltpu.VMEM((SUBLANES, LANES), F32)],
        compiler_params=pltpu.CompilerParams(dimension_semantics=("arbitrary",)),
        name="moe_route",
    )(logits)


DISPATCH_TILE = 512


def _row_slab(r):
    return pl.ds(pl.multiple_of(r * ROW_SLABS, ROW_SLABS), ROW_SLABS)


WAIT_UNROLL = 32


def _drain_rows(src_ref, dst_ref, sem, n_rows):
    assert n_rows % WAIT_UNROLL == 0

    def body(i, carry):
        for _ in range(WAIT_UNROLL):
            pltpu.make_async_copy(src_ref.at[_row_slab(0)], dst_ref.at[_row_slab(0)], sem).wait()
        return carry

    lax.fori_loop(0, n_rows // WAIT_UNROLL, body, 0)


def _dispatch_kernel(dest_ref, h_ref, xb_in, xb_hbm, sem):
    del xb_in

    def issue(t, carry):
        for k in range(TOP_K):
            pltpu.make_async_copy(h_ref.at[_row_slab(t)],
                                  xb_hbm.at[_row_slab(dest_ref[0, t * TOP_K + k])], sem).start(priority=k % 2)
        return carry

    lax.fori_loop(0, DISPATCH_TILE, issue, 0)
    _drain_rows(h_ref, xb_hbm, sem, DISPATCH_TILE * TOP_K)


def _dispatch(dest, h2, xb):
    n_tiles = dest.shape[0] // (DISPATCH_TILE * TOP_K)
    return pl.pallas_call(
        _dispatch_kernel,
        grid=(n_tiles,),
        in_specs=[pl.BlockSpec((None, 1, DISPATCH_TILE * TOP_K), lambda i: (i, 0, 0), memory_space=pltpu.SMEM),
                  pl.BlockSpec((DISPATCH_TILE * ROW_SLABS, LANES), lambda i: (i, 0)),
                  pl.BlockSpec(memory_space=pl.ANY)],
        out_specs=pl.BlockSpec(memory_space=pl.ANY),
        out_shape=jax.ShapeDtypeStruct(xb.shape, xb.dtype),
        scratch_shapes=[pltpu.SemaphoreType.DMA(())],
        input_output_aliases={2: 0},
        compiler_params=pltpu.CompilerParams(dimension_semantics=("arbitrary",)),
        name="moe_dispatch",
    )(dest.reshape(n_tiles, 1, DISPATCH_TILE * TOP_K), h2, xb)


def _expert_kernel(be_ref, nv_ref, x_ref, wgu_ref, bgu_ref, wdn_ref, bdn_ref, yb_in, y_ref,
                   wgu_bf, wdn_bf, act_ref):
    del yb_in
    i = pl.program_id(0)
    valid = i < nv_ref[0]
    fresh = jnp.logical_or(i == 0, be_ref[i] != be_ref[jnp.maximum(i - 1, 0)])

    @pl.when(jnp.logical_and(valid, fresh))
    def _():
        for r0 in range(0, D_MODEL, LANES):
            wgu_bf[r0:r0 + LANES, :] = wgu_ref[r0:r0 + LANES, :].astype(BF16)
            wdn_bf[r0:r0 + LANES, :] = wdn_ref[r0:r0 + LANES, :].astype(BF16)

    @pl.when(valid)
    def _():
        x = jnp.concatenate([x_ref[_slab(c, MOE_TILE), :] for c in range(ROW_SLABS)], axis=1).astype(BF16)
        half = D_EXPERT // 2
        for c0 in range(0, D_EXPERT, half):
            glu = jnp.dot(x, wgu_bf[:, c0:c0 + half], preferred_element_type=F32) + bgu_ref[:, c0:c0 + half]
            lin = (jnp.dot(x, wgu_bf[:, D_EXPERT + c0:D_EXPERT + c0 + half], preferred_element_type=F32)
                   + bgu_ref[:, D_EXPERT + c0:D_EXPERT + c0 + half])
            glu = jnp.minimum(glu, SWIGLU_LIMIT)
            lin = jnp.clip(lin, -SWIGLU_LIMIT, SWIGLU_LIMIT)
            act_ref[:, c0:c0 + half] = (glu * _sigmoid(SWIGLU_ALPHA * glu) * (lin + 1.0)).astype(BF16)
        y = jnp.dot(act_ref[...], wdn_bf[...], preferred_element_type=F32) + bdn_ref[...]
        for c in range(ROW_SLABS):
            y_ref[_slab(c, MOE_TILE), :] = y[:, c * LANES:(c + 1) * LANES]


def _experts(layer, block_e, n_valid, xb, w_gu, b_gu, w_dn, b_dn, yb):
    n_blocks = block_e.shape[0]
    blk = lambda i, be, nv: jnp.minimum(i, nv[0] - 1)
    rows = pl.BlockSpec((MOE_TILE * ROW_SLABS, LANES), lambda i, be, nv: (blk(i, be, nv), 0))
    per_e = lambda r, c: pl.BlockSpec((None, None, r, c), lambda i, be, nv: (layer, be[blk(i, be, nv)], 0, 0))
    return pl.pallas_call(
        _expert_kernel,
        grid_spec=pltpu.PrefetchScalarGridSpec(
            num_scalar_prefetch=2,
            grid=(n_blocks,),
            in_specs=[rows, per_e(D_MODEL, 2 * D_EXPERT), per_e(1, 2 * D_EXPERT),
                      per_e(D_EXPERT, D_MODEL), per_e(1, D_MODEL), pl.BlockSpec(memory_space=pl.ANY)],
            out_specs=rows,
            scratch_shapes=[pltpu.VMEM((D_MODEL, 2 * D_EXPERT), BF16), pltpu.VMEM((D_EXPERT, D_MODEL), BF16),
                            pltpu.VMEM((MOE_TILE, D_EXPERT), BF16)]),
        out_shape=jax.ShapeDtypeStruct(xb.shape, F32),
        input_output_aliases={7: 0},
        compiler_params=pltpu.CompilerParams(dimension_semantics=("arbitrary",), vmem_limit_bytes=VMEM_LIMIT),
        name="moe_experts",
    )(block_e, n_valid, xb, w_gu, b_gu[:, :, None, :], w_dn, b_dn[:, :, None, :], yb)


COMBINE_TILE = 128


def _combine_kernel(dest_ref, dnext_ref, yb_hbm, x_ref, gt_ref, mg_ref, o_ref, buf_ref, sem, *, n_steps):
    step = pl.program_id(0) * pl.num_programs(1) + pl.program_id(1)
    slot = lax.rem(step, 2)

    def gather(d_ref, into):
        def body(t, carry):
            for k in range(TOP_K):
                pltpu.make_async_copy(yb_hbm.at[_row_slab(d_ref[0, t * TOP_K + k])],
                                      buf_ref.at[into, _row_slab(k * COMBINE_TILE + t)],
                                      sem.at[into]).start(priority=k % 2)
            return carry

        lax.fori_loop(0, COMBINE_TILE, body, 0)

    @pl.when(step == 0)
    def _():
        gather(dest_ref, 0)

    @pl.when(step + 1 < n_steps)
    def _():
        gather(dnext_ref, 1 - slot)

    _drain_rows(yb_hbm, buf_ref.at[slot], sem.at[slot], COMBINE_TILE * TOP_K)
    gates = gt_ref[...]
    acc = None
    for k in range(TOP_K):
        yk = jnp.concatenate(
            [buf_ref[slot, pl.ds(k * COMBINE_TILE * ROW_SLABS + c, COMBINE_TILE, stride=ROW_SLABS), :]
             for c in range(ROW_SLABS)], axis=1)
        term = gates[:, k:k + 1] * yk
        acc = term if acc is None else acc + term
    o_ref[...] = x_ref[...] + mg_ref[...] * acc


def _combine(dest, yb, xa, gates, mod3, n_lat):
    nb, t, d = xa.shape
    nt = t // COMBINE_TILE
    n_steps = nb * nt
    n_lat_tiles = n_lat // COMBINE_TILE
    tok = pl.BlockSpec((None, COMBINE_TILE, d), lambda b, i: (b, i, 0))
    rows_of = lambda step_of: pl.BlockSpec((None, 1, COMBINE_TILE * TOP_K),
                                           lambda b, i: (step_of(b * nt + i), 0, 0), memory_space=pltpu.SMEM)
    dest3 = dest.reshape(n_steps, 1, COMBINE_TILE * TOP_K)
    return pl.pallas_call(
        functools.partial(_combine_kernel, n_steps=n_steps),
        grid=(nb, nt),
        in_specs=[rows_of(lambda s: s), rows_of(lambda s: jnp.minimum(s + 1, n_steps - 1)),
                  pl.BlockSpec(memory_space=pl.ANY), tok,
                  pl.BlockSpec((COMBINE_TILE, LANES), lambda b, i: (b * nt + i, 0)),
                  _mod_spec(5, n_lat_tiles, nb)],
        out_specs=tok,
        out_shape=jax.ShapeDtypeStruct((nb, t, d), F32),
        scratch_shapes=[pltpu.VMEM((2, TOP_K * COMBINE_TILE * ROW_SLABS, LANES), F32),
                        pltpu.SemaphoreType.DMA((2,))],
        compiler_params=pltpu.CompilerParams(dimension_semantics=("arbitrary", "arbitrary"),
                                             vmem_limit_bytes=VMEM_LIMIT),
        name="moe_combine",
    )(dest3, dest3, yb, xa, gates, mod3)


def _final_norm_kernel(x_ref, g_ref, o_ref):
    o_ref[...] = _rms(x_ref[...]) * g_ref[...]


def _final_norm(xa, g, n_lat):
    nb, _, d = xa.shape
    tok = pl.BlockSpec((None, TOK_TILE, d), lambda b, i: (b, i, 0))
    return pl.pallas_call(
        _final_norm_kernel,
        grid=(nb, n_lat // TOK_TILE),
        in_specs=[tok, pl.BlockSpec((1, d), lambda b, i: (0, 0))],
        out_specs=tok,
        out_shape=jax.ShapeDtypeStruct((nb, n_lat, d), F32),
        compiler_params=pltpu.CompilerParams(dimension_semantics=("arbitrary", "arbitrary")),
        name="final_norm",
    )(xa, g.reshape(1, d))


def _pack_w_in(w_in_t, cols):
    nl, n_src, d = w_in_t.shape
    idx = np.where(cols >= 0, cols, nl * n_src)
    idx = np.where(cols >= 0, idx[None, :] + n_src * np.arange(nl)[:, None], nl * n_src).reshape(-1)
    rows = jnp.take(w_in_t.reshape(nl * n_src, d), jnp.asarray(idx, jnp.int32), axis=0, mode='fill', fill_value=0)
    return jnp.swapaxes(rows.reshape(nl, len(cols), d), 1, 2)


def _pack_gla(w_gate2, b_gate):
    nl = w_gate2.shape[0]
    w = w_gate2.reshape(nl, 2, GLA_RANK, GLA_HEADS, GLA_DK).transpose(0, 3, 1, 2, 4)
    z = jnp.zeros((nl, GLA_HEADS, GLA_RANK, GLA_DK), F32)
    top = jnp.concatenate([w[:, :, 0], z], axis=-1)
    bot = jnp.concatenate([z, w[:, :, 1]], axis=-1)
    w2 = jnp.concatenate([top, bot, jnp.zeros((nl, GLA_HEADS, LANES - 2 * GLA_RANK, LANES), F32)], axis=2)
    b2 = b_gate.reshape(nl, 2, GLA_HEADS, GLA_DK).transpose(0, 2, 1, 3).reshape(nl, GLA_HEADS, 1, LANES)
    return w2, b2


def _pack_mlstm(conv_w, conv_b, b_i, b_f, norm_g):
    nl = conv_w.shape[0]
    cw = conv_w.reshape(nl, CONV_W, 2, ML_HEADS, ML_DQK).transpose(0, 3, 1, 2, 4).reshape(nl, ML_HEADS, CONV_W, LANES)
    cw = jnp.pad(cw, ((0, 0), (0, 0), (0, SUBLANES - CONV_W), (0, 0)))
    cb = conv_b.reshape(nl, 2, ML_HEADS, ML_DQK).transpose(0, 2, 1, 3).reshape(nl, ML_HEADS, 1, LANES)
    gates = jnp.stack([b_i[:, 0], b_i[:, 1], b_f[:, 0], b_f[:, 1]], axis=-1)
    gb = jnp.repeat(gates, ML_GATE_REP, axis=-1).reshape(nl, ML_HEADS, 1, LANES)
    return cw, cb, gb, norm_g.reshape(nl, ML_HEADS, 1, ML_DV)


def _pack_ssd(conv_w, conv_b, dt_bias, a_log, d_skip, norm_g):
    nl = conv_w.shape[0]
    bc = M2_GROUPS * M2_DSTATE

    def conv_cols(a):
        lead = a.shape[:-1]
        x = a[..., :M2_WIDTH].reshape(*lead, M2_GROUPS, M2_GROUP_X)
        b = a[..., M2_WIDTH:M2_WIDTH + bc].reshape(*lead, M2_GROUPS, M2_DSTATE)
        c = a[..., M2_WIDTH + bc:].reshape(*lead, M2_GROUPS, M2_DSTATE)
        return jnp.concatenate([x, b, c], axis=-1)

    cw = jnp.pad(conv_cols(conv_w).transpose(0, 2, 1, 3), ((0, 0), (0, 0), (0, SUBLANES - CONV_W), (0, 0)))
    cb = conv_cols(conv_b).reshape(nl, M2_GROUPS, 1, M2_CONV_COLS)

    def per_dir(a):
        a = a.reshape(nl, 2, M2_GROUPS, M2_GROUP_HEADS).transpose(0, 2, 1, 3)
        return jnp.repeat(a, M2_HEADDIM, axis=-1).reshape(nl, M2_GROUPS, 1, M2_DT_COLS)

    dsk = jnp.repeat(d_skip.reshape(nl, M2_GROUPS, M2_GROUP_HEADS), M2_HEADDIM, axis=-1)
    return (cw, cb, per_dir(dt_bias), per_dir(a_log), dsk.reshape(nl, M2_GROUPS, 1, M2_GROUP_X),
            norm_g.reshape(nl, M2_GROUPS, 1, M2_GROUP_X))


def _moe_plan(e_arr, rank_arr, counts_row, n_blocks):
    counts = counts_row[0, :N_EXPERTS].astype(jnp.int32)
    padded = (counts + MOE_TILE - 1) // MOE_TILE * MOE_TILE
    pad_end = jnp.cumsum(padded)
    pad_start = pad_end - padded
    e = e_arr[:, :TOP_K]
    dest = (jnp.take(pad_start, e) + rank_arr[:, :TOP_K].astype(jnp.int32)).reshape(-1)
    block_start = jnp.arange(n_blocks, dtype=jnp.int32) * MOE_TILE
    block_e = jnp.sum((pad_end[None, :] <= block_start[:, None]).astype(jnp.int32), axis=1)
    block_e = jnp.minimum(block_e, N_EXPERTS - 1)
    n_valid = (pad_end[-1:] // MOE_TILE).astype(jnp.int32)
    return dest, block_e, n_valid


def kernel(x, c, ctx, c_ctx, w_mod, b_mod, norm1_g, w_in, gla_w_gate2, gla_b_gate, gla_norm_g, ml_conv_w,
           ml_conv_b, ml_b_i, ml_b_f, ml_norm_g, m2_conv_w, m2_conv_b, m2_dt_bias, m2_A_log, m2_D, m2_norm_g,
           w_out, norm2_g, router_w, router_b, moe_w_gu, moe_b_gu, moe_w_dn, moe_b_dn, final_norm_g):
    nb, n_lat, d = x.shape
    n_ctx = ctx.shape[1]
    n_layers = w_mod.shape[0]
    t = n_lat + n_ctx
    assert d == D_MODEL and n_lat % (GRID_W * SUBLANES) == 0 and n_lat % TOK_TILE == 0 and n_ctx % TOK_TILE == 0
    assert (nb * t) % DISPATCH_TILE == 0

    mod_rows = -(-(nb + 1) // SUBLANES) * SUBLANES
    c_rows = jnp.concatenate([c, c_ctx[None], jnp.zeros((mod_rows - nb - 1, d), F32)], axis=0)
    mod = _mod_table(c_rows, w_mod, b_mod).reshape(n_layers, mod_rows * 6, 1, d)

    w_in_t = jnp.swapaxes(w_in.astype(BF16), 1, 2)
    w_in_a = _pack_w_in(w_in_t, _IN_COLS[:GLA_PACK + ML_PACK])
    w_in_b = _pack_w_in(w_in_t, _IN_COLS[GLA_PACK + ML_PACK:])
    w_out_p = w_out.astype(BF16)
    grid_rows = n_lat // GRID_W
    gla_w2, gla_b2 = _pack_gla(gla_w_gate2, gla_b_gate)
    ml_cw, ml_cb, ml_gb, ml_ng = _pack_mlstm(ml_conv_w, ml_conv_b, ml_b_i, ml_b_f, ml_norm_g)
    m2_cw, m2_cb, m2_dtb, m2_alog, m2_dsk, m2_ng = _pack_ssd(m2_conv_w, m2_conv_b, m2_dt_bias, m2_A_log, m2_D,
                                                            m2_norm_g)
    rw = jnp.pad(router_w, ((0, 0), (0, 0), (0, LANES - N_EXPERTS)))
    rb = jnp.pad(router_b, ((0, 0), (0, LANES - N_EXPERTS)), constant_values=M_INIT).reshape(n_layers, 1, LANES)

    n_assign = nb * t * TOP_K
    n_blocks = n_assign // MOE_TILE + N_EXPERTS
    xb = jnp.zeros((n_blocks * MOE_TILE * ROW_SLABS, LANES), F32)
    yb = jnp.zeros_like(xb)

    xa = jnp.concatenate([x, ctx], axis=1)
    for l in range(n_layers):
        u_gla, u_ml = _in_proj(xa, None, mod[l], norm1_g[l], w_in_a[l], (GLA_PACK, ML_PACK), n_lat)
        x_cm = xa[:, :n_lat].reshape(nb, grid_rows, GRID_W, d).transpose(0, 2, 1, 3).reshape(nb, n_lat, d)
        u_m2, = _in_proj(xa, x_cm, mod[l], norm1_g[l], w_in_b[l], (M2_PACK,), n_lat)
        o_gla = _gla_mixer(u_gla, gla_w2[l], gla_b2[l], gla_norm_g[l].reshape(1, GLA_DV), n_lat)
        o_ml = _mlstm_mixer(u_ml, ml_cw[l], ml_cb[l], ml_gb[l], ml_ng[l], n_lat)
        o_m2 = _ssd_mixer(u_m2, m2_cw[l], m2_cb[l], m2_dtb[l], m2_alog[l], m2_dsk[l], m2_ng[l], n_lat)
        o_m2_lat = (o_m2[:, :n_lat].reshape(nb, GRID_W, grid_rows, M2_WIDTH).transpose(0, 2, 1, 3)
                    .reshape(nb, n_lat, M2_WIDTH))
        xa, h2, logits = _out_proj(xa, o_gla, o_ml, o_m2_lat, o_m2, w_out_p[l], mod[l], norm2_g[l], rw[l], rb[l],
                                   n_lat)
        e_arr, gates, rank_arr, counts = _route(logits)
        dest, block_e, n_valid = _moe_plan(e_arr, rank_arr, counts, n_blocks)
        xb = _dispatch(dest, h2, xb)
        yb = _experts(l, block_e, n_valid, xb, moe_w_gu, moe_b_gu, moe_w_dn, moe_b_dn, yb)
        xa = _combine(dest, yb, xa, gates, mod[l], n_lat)
    return _final_norm(xa, final_norm_g, n_lat)
```

```python
import functools
import math

import numpy as np
import jax
import jax.numpy as jnp
from jax import lax
from jax.experimental import pallas as pl
from jax.experimental.pallas import tpu as pltpu

F32 = jnp.float32
BF16 = jnp.bfloat16

D_MODEL = 1024
GRID_W = 64
GLA_HEADS, GLA_DK, GLA_DV, GLA_RANK = 4, 64, 128, 16
GLA_NORMALIZER = 16.0
ML_HEADS, ML_DQK, ML_DV = 4, 64, 128
M2_HEADS, M2_HEADDIM, M2_GROUPS, M2_DSTATE = 8, 64, 2, 128
CONV_W = 7
CONV_R = CONV_W // 2
N_EXPERTS, TOP_K, D_EXPERT = 32, 4, 1024
SWIGLU_LIMIT, SWIGLU_ALPHA = 7.0, 1.702
EPS = 1e-6
M_INIT = -1e30

GLA_WIDTH = GLA_HEADS * GLA_DV
ML_WIDTH = ML_HEADS * ML_DV
M2_WIDTH = M2_HEADS * M2_HEADDIM
MIX_WIDTH = GLA_WIDTH + ML_WIDTH + M2_WIDTH
GLA_IN = 2 * GLA_HEADS * GLA_DK + 2 * GLA_WIDTH + 2 * GLA_RANK
ML_IN = 2 * ML_HEADS * ML_DQK + 2 * ML_WIDTH + 4 * ML_HEADS
M2_CONV_DIM = M2_WIDTH + 2 * M2_GROUPS * M2_DSTATE
M2_IN = M2_WIDTH + M2_CONV_DIM + 2 * M2_HEADS
IN_WIDTH = GLA_IN + ML_IN + M2_IN

LANES = 128
SUBLANES = 8
CHUNK = 128
TOK_TILE = 256
MOE_TILE = 512
VMEM_LIMIT = 52 * 1024 * 1024

GLA_HEAD_COLS = 3 * LANES
GLA_PACK = GLA_HEADS * GLA_HEAD_COLS + LANES
ML_HEAD_COLS = 4 * LANES
ML_PACK = ML_HEADS * ML_HEAD_COLS
M2_GROUP_HEADS = M2_HEADS // M2_GROUPS
M2_GROUP_X = M2_GROUP_HEADS * M2_HEADDIM
M2_PAIRS = M2_GROUP_HEADS // 2
M2_DT_COLS = 2 * M2_PAIRS * LANES
M2_GROUP_COLS = 2 * M2_GROUP_X + 2 * M2_DSTATE + LANES
M2_PACK = M2_GROUPS * M2_GROUP_COLS
IN_PACK = GLA_PACK + ML_PACK + M2_PACK
ML_GATE_REP = LANES // 4


def _in_proj_column_map():
    cols = []
    qk = GLA_HEADS * GLA_DK
    for h in range(GLA_HEADS):
        cols += list(range(h * GLA_DK, (h + 1) * GLA_DK))
        cols += list(range(qk + h * GLA_DK, qk + (h + 1) * GLA_DK))
        cols += list(range(2 * qk + h * GLA_DV, 2 * qk + (h + 1) * GLA_DV))
        cols += list(range(2 * qk + GLA_WIDTH + h * GLA_DV, 2 * qk + GLA_WIDTH + (h + 1) * GLA_DV))
    lr0 = 2 * qk + 2 * GLA_WIDTH
    cols += list(range(lr0, lr0 + 2 * GLA_RANK)) + [-1] * (LANES - 2 * GLA_RANK)
    a0 = GLA_IN
    qk = ML_HEADS * ML_DQK
    g0 = a0 + 2 * qk + 2 * ML_WIDTH
    for h in range(ML_HEADS):
        cols += list(range(a0 + h * ML_DQK, a0 + (h + 1) * ML_DQK))
        cols += list(range(a0 + qk + h * ML_DQK, a0 + qk + (h + 1) * ML_DQK))
        cols += list(range(a0 + 2 * qk + h * ML_DV, a0 + 2 * qk + (h + 1) * ML_DV))
        cols += list(range(a0 + 2 * qk + ML_WIDTH + h * ML_DV, a0 + 2 * qk + ML_WIDTH + (h + 1) * ML_DV))
        for gate in range(4):
            cols += [g0 + gate * ML_HEADS + h] * ML_GATE_REP
    a1 = GLA_IN + ML_IN
    x0 = a1 + M2_WIDTH
    dt0 = a1 + M2_WIDTH + M2_CONV_DIM
    for g in range(M2_GROUPS):
        cols += list(range(a1 + g * M2_GROUP_X, a1 + (g + 1) * M2_GROUP_X))
        cols += list(range(x0 + g * M2_GROUP_X, x0 + (g + 1) * M2_GROUP_X))
        cols += list(range(x0 + M2_WIDTH + g * M2_DSTATE, x0 + M2_WIDTH + (g + 1) * M2_DSTATE))
        cols += list(range(x0 + M2_WIDTH + M2_GROUPS * M2_DSTATE + g * M2_DSTATE,
                           x0 + M2_WIDTH + M2_GROUPS * M2_DSTATE + (g + 1) * M2_DSTATE))
        for d in range(2):
            for h in range(M2_GROUP_HEADS):
                cols += [dt0 + d * M2_HEADS + g * M2_GROUP_HEADS + h]
        cols += [-1] * (LANES - 2 * M2_GROUP_HEADS)
    cols = np.asarray(cols, np.int32)
    assert cols.shape == (IN_PACK,)
    return cols


_IN_COLS = _in_proj_column_map()


def _bdot(a, b):
    return jnp.dot(a.astype(BF16), b.astype(BF16), preferred_element_type=F32)


def _bdot_nt(a, b):
    return lax.dot_general(a.astype(BF16), b.astype(BF16), (((1,), (1,)), ((), ())),
                           preferred_element_type=F32)


def _bdot_tn(a, b):
    return lax.dot_general(a.astype(BF16), b.astype(BF16), (((0,), (0,)), ((), ())),
                           preferred_element_type=F32)


def _split2(a):
    hi = a.astype(BF16)
    lo = (a - hi.astype(F32)).astype(BF16)
    return hi, lo


def _split3(a):
    hi = a.astype(BF16)
    r = a - hi.astype(F32)
    mid = r.astype(BF16)
    lo = (r - mid.astype(F32)).astype(BF16)
    return hi, mid, lo


def _hdot(a, b):
    ah, al = _split2(a)
    bh, bl = _split2(b)
    d = functools.partial(jnp.dot, preferred_element_type=F32)
    return d(ah, bh) + (d(ah, bl) + d(al, bh))


def _tri_dot(tri, a):
    hi, mid, lo = _split3(a)
    d = functools.partial(jnp.dot, preferred_element_type=F32)
    return d(tri, hi) + (d(tri, mid) + d(tri, lo))


def _rms(x):
    return x * lax.rsqrt(jnp.mean(x * x, axis=-1, keepdims=True) + EPS)


def _sigmoid(x):
    return 1.0 / (1.0 + jnp.exp(-x))


def _silu(x):
    return x * _sigmoid(x)


def _log_sigmoid(x):
    return jnp.minimum(x, 0.0) - jnp.log(1.0 + jnp.exp(-jnp.abs(x)))


def _softplus(x):
    return jnp.maximum(x, 0.0) + jnp.log(1.0 + jnp.exp(-jnp.abs(x)))


def _lane_iota(shape):
    return lax.broadcasted_iota(jnp.int32, shape, len(shape) - 1)


def _row_iota(shape):
    return lax.broadcasted_iota(jnp.int32, shape, len(shape) - 2)


def _lane_rep(x, lo, width):
    lane = _lane_iota(x.shape)
    y = jnp.where((lane >= lo) & (lane < lo + width), x, 0.0)
    w = width
    while w < LANES:
        y = y + pltpu.roll(y, w, axis=1)
        w *= 2
    return y


def _swap_halves(x):
    return pltpu.roll(x, LANES // 2, axis=1)


def _tri_incl(n):
    r = lax.broadcasted_iota(jnp.int32, (n, n), 0)
    c = lax.broadcasted_iota(jnp.int32, (n, n), 1)
    return (r >= c).astype(BF16)


def _mod_kernel(c_ref, w_ref, b_ref, o_ref):
    o_ref[...] = _hdot(_silu(c_ref[...]), w_ref[...]) + b_ref[...]


def _mod_table(c_rows, w_mod, b_mod):
    n_layers, d, n6 = w_mod.shape
    r = c_rows.shape[0]
    tn = 1536
    return pl.pallas_call(
        _mod_kernel,
        grid=(n_layers, n6 // tn),
        in_specs=[pl.BlockSpec((r, d), lambda l, j: (0, 0)),
                  pl.BlockSpec((None, d, tn), lambda l, j: (l, 0, j)),
                  pl.BlockSpec((None, 1, tn), lambda l, j: (l, 0, j))],
        out_specs=pl.BlockSpec((None, r, tn), lambda l, j: (l, 0, j)),
        out_shape=jax.ShapeDtypeStruct((n_layers, r, n6), F32),
        compiler_params=pltpu.CompilerParams(dimension_semantics=("arbitrary", "arbitrary"),
                                             vmem_limit_bytes=VMEM_LIMIT),
        name="adaln_mod",
    )(c_rows, w_mod, b_mod.reshape(n_layers, 1, n6))


def _inproj_kernel(*refs, n_lat_tiles, widths, two_sources):
    if two_sources:
        xl_ref, xc_ref, sh_ref, sc_ref, g_ref, w_ref = refs[:6]
        outs = refs[6:]
        x = jnp.where(pl.program_id(1) < n_lat_tiles, xl_ref[...], xc_ref[...])
    else:
        x_ref, sh_ref, sc_ref, g_ref, w_ref = refs[:5]
        outs = refs[5:]
        x = x_ref[...]
    h = _rms(x) * g_ref[...]
    h = (h * (1.0 + sc_ref[...]) + sh_ref[...]).astype(BF16)
    c0 = 0
    for o_ref, width in zip(outs, widths):
        o_ref[...] = jnp.dot(h, w_ref[:, c0:c0 + width], preferred_element_type=F32)
        c0 += width


def _mod_spec(which, n_lat_tiles, n_batch):
    def imap(b, t):
        row = jnp.where(t < n_lat_tiles, b, n_batch)
        return (row * 6 + which, 0, 0)
    return pl.BlockSpec((None, 1, D_MODEL), imap)


def _in_proj(xa, x_lat, mod3, norm_g, w_pack, widths, n_lat):
    nb, t, d = xa.shape
    n_lat_tiles = n_lat // TOK_TILE
    tok = lambda width: pl.BlockSpec((None, TOK_TILE, width), lambda b, i: (b, i, 0))
    if x_lat is None:
        x_specs, x_args = [tok(d)], (xa,)
    else:
        x_specs = [pl.BlockSpec((None, TOK_TILE, d), lambda b, i: (b, jnp.minimum(i, n_lat_tiles - 1), 0)),
                   pl.BlockSpec((None, TOK_TILE, d), lambda b, i: (b, jnp.maximum(i, n_lat_tiles), 0))]
        x_args = (x_lat, xa)
    return pl.pallas_call(
        functools.partial(_inproj_kernel, n_lat_tiles=n_lat_tiles, widths=widths, two_sources=x_lat is not None),
        grid=(nb, t // TOK_TILE),
        in_specs=x_specs + [_mod_spec(0, n_lat_tiles, nb), _mod_spec(1, n_lat_tiles, nb),
                            pl.BlockSpec((1, d), lambda b, i: (0, 0)),
                            pl.BlockSpec((d, sum(widths)), lambda b, i: (0, 0))],
        out_specs=[tok(w) for w in widths],
        out_shape=[jax.ShapeDtypeStruct((nb, t, w), F32) for w in widths],
        compiler_params=pltpu.CompilerParams(dimension_semantics=("arbitrary", "arbitrary"),
                                             vmem_limit_bytes=VMEM_LIMIT),
        name="in_proj",
    )(*x_args, mod3, mod3, norm_g.reshape(1, d), w_pack)


def _chunk_rows(c):
    return pl.ds(pl.multiple_of(c * CHUNK, CHUNK), CHUNK)


def _row_bcast(row):
    return jnp.broadcast_to(row, (SUBLANES, row.shape[-1]))


CHUNK_UNROLL = 3


def _for_chunks(n, body):
    assert n % CHUNK_UNROLL == 0

    def step(i, carry):
        for j in range(CHUNK_UNROLL):
            body(i * CHUNK_UNROLL + j)
        return carry

    lax.fori_loop(0, n // CHUNK_UNROLL, step, 0)


CUMSUM_GROUP = 6


def _chunk_cumsums(n, tri, load, emit):
    assert n % CUMSUM_GROUP == 0
    for c0 in range(0, n, CUMSUM_GROUP):
        xs = [load(c) for c in range(c0, c0 + CUMSUM_GROUP)]
        width = xs[0].shape[1]
        p = _tri_dot(tri, jnp.concatenate(xs, axis=1))
        for j in range(CUMSUM_GROUP):
            emit(c0 + j, xs[j], p[:, j * width:(j + 1) * width])


def _gla_kernel(u_ref, lr_ref, w2_ref, b2_ref, ng_ref, o_ref,
                cum_ref, oin_ref, q2_ref, ds_ref, a_ref, stf_ref, stb_ref, *, n_lat_chunks):
    t = u_ref.shape[0]
    n = t // CHUNK
    tri = _tri_incl(CHUNK)
    fwd = _lane_iota((CHUNK, LANES)) < LANES // 2
    row = _row_iota((CHUNK, CHUNK))
    col = _lane_iota((CHUNK, CHUNK))
    w2 = w2_ref[...]
    b2 = b2_ref[...]

    def log_decay(i, carry):
        rows = pl.ds(pl.multiple_of(i * TOK_TILE, TOK_TILE), TOK_TILE)
        cum_ref[rows, :] = _log_sigmoid(_hdot(lr_ref[rows, :], w2) + b2) * (1.0 / GLA_NORMALIZER)
        return carry

    lax.fori_loop(0, t // TOK_TILE, log_decay, 0)

    def emit_cum(c, la, p):
        tot = p[CHUNK - 1:CHUNK, :]
        cum_ref[c * CHUNK:(c + 1) * CHUNK, :] = jnp.where(fwd, p, tot - p + la)
        a_ref[c] = _row_bcast(jnp.exp(tot))

    _chunk_cumsums(n, tri, lambda c: cum_ref[c * CHUNK:(c + 1) * CHUNK, :], emit_cum)

    def local(c):
        rows = _chunk_rows(c)
        qk = u_ref[rows, 0:LANES]
        v = u_ref[rows, LANES:2 * LANES]
        cum = cum_ref[rows, :]
        tot = jnp.where(fwd[0:1], cum[CHUNK - 1:CHUNK, :], cum[0:1, :])
        mid = cum[CHUNK // 2:CHUNK // 2 + 1, :]
        sw = _swap_halves(qk)
        qq = jnp.where(fwd, qk, sw) * (GLA_DK ** -0.5)
        kk = jnp.where(fwd, sw, qk)
        qe = qq * jnp.exp(cum - mid)
        ke = kk * jnp.exp(mid - cum)
        af = _bdot_nt(jnp.where(fwd, qe, 0.0), ke)
        ab = _bdot_nt(jnp.where(fwd, 0.0, qe), ke)
        attn = jnp.where(row >= col, af, 0.0) + jnp.where(col >= row, ab, 0.0)
        oin_ref[rows, :] = _bdot(attn, v)
        q2_ref[rows, :] = qq * jnp.exp(cum)
        ds_ref[c] = _bdot_tn(v, kk * jnp.exp(tot - cum))

    _for_chunks(n, local)

    def scan(s, st):
        f = lax.rem(s + n_lat_chunks, n)
        g = n - 1 - s
        stf_ref[f] = st
        stb_ref[g] = st
        a = jnp.where(fwd[0:1], a_ref[f][0:1], a_ref[g][0:1])
        return st * a + jnp.where(fwd, ds_ref[f], ds_ref[g])

    lax.fori_loop(0, n, scan, jnp.zeros((GLA_DV, LANES), F32))

    def finish(c):
        rows = _chunk_rows(c)
        st = jnp.where(fwd, stf_ref[c], stb_ref[c])
        o = oin_ref[rows, :] + _bdot_nt(q2_ref[rows, :], st)
        o = _rms(o) * ng_ref[...]
        o_ref[rows, :] = (o * _silu(u_ref[rows, 2 * LANES:3 * LANES])).astype(o_ref.dtype)

    _for_chunks(n, finish)


def _gla_mixer(u, w2, b2, ng, n_lat):
    nb, t, _ = u.shape
    n = t // CHUNK
    return pl.pallas_call(
        functools.partial(_gla_kernel, n_lat_chunks=n_lat // CHUNK),
        grid=(nb, GLA_HEADS),
        in_specs=[pl.BlockSpec((None, t, GLA_HEAD_COLS), lambda b, h: (b, 0, h)),
                  pl.BlockSpec((None, t, LANES), lambda b, h: (b, 0, GLA_HEADS * GLA_HEAD_COLS // LANES)),
                  pl.BlockSpec((None, LANES, LANES), lambda b, h: (h, 0, 0)),
                  pl.BlockSpec((None, 1, LANES), lambda b, h: (h, 0, 0)),
                  pl.BlockSpec((1, LANES), lambda b, h: (0, 0))],
        out_specs=pl.BlockSpec((None, t, GLA_DV), lambda b, h: (b, 0, h)),
        out_shape=jax.ShapeDtypeStruct((nb, t, GLA_WIDTH), BF16),
        scratch_shapes=[pltpu.VMEM((t, LANES), F32), pltpu.VMEM((t, LANES), F32), pltpu.VMEM((t, LANES), F32),
                        pltpu.VMEM((n, GLA_DV, LANES), F32), pltpu.VMEM((n, SUBLANES, LANES), F32),
                        pltpu.VMEM((n, GLA_DV, LANES), F32), pltpu.VMEM((n, GLA_DV, LANES), F32)],
        compiler_params=pltpu.CompilerParams(dimension_semantics=("arbitrary", "arbitrary"),
                                             vmem_limit_bytes=VMEM_LIMIT),
        name="gla_mixer",
    )(u, u, w2, b2, ng)


def _zero_pads(pad_ref, n_lat, t):
    z = jnp.zeros((SUBLANES, pad_ref.shape[1]), F32)
    pad_ref[0:SUBLANES, :] = z
    pad_ref[SUBLANES + n_lat:2 * SUBLANES + n_lat, :] = z
    pad_ref[2 * SUBLANES + t:3 * SUBLANES + t, :] = z


def _pad_base(c, n_lat_chunks):
    return c * CHUNK + (SUBLANES if c < n_lat_chunks else 2 * SUBLANES)


def _conv_silu(pad_ref, w_ref, b_ref, out_ref, n, n_lat_chunks):
    width = pad_ref.shape[1]
    for c in range(n):
        base = _pad_base(c, n_lat_chunks)
        for l0 in range(0, width, LANES):
            acc = None
            for j in range(CONV_W):
                term = w_ref[j:j + 1, l0:l0 + LANES] * pad_ref[base + j - CONV_R:base + j - CONV_R + CHUNK,
                                                               l0:l0 + LANES]
                acc = term if acc is None else acc + term
            out_ref[c * CHUNK:(c + 1) * CHUNK, l0:l0 + LANES] = _silu(acc + b_ref[:, l0:l0 + LANES])


def _mlstm_kernel(u_ref, cw_ref, cb_ref, gb_ref, ng_ref, o_ref,
                  pad_ref, qk_ref, fc_ref, rc_ref, dc_ref, tot_ref, mloc_ref, stm_ref, *, n_lat_chunks):
    t = u_ref.shape[0]
    n = t // CHUNK
    n_lat = n_lat_chunks * CHUNK
    tri = _tri_incl(CHUNK)
    lane = _lane_iota((CHUNK, LANES))
    hi_half = lane >= LANES // 2
    row = _row_iota((CHUNK, CHUNK))
    col = _lane_iota((CHUNK, CHUNK))
    masks = (row >= col, col >= row)
    ones = jnp.ones((CHUNK, LANES), F32)

    _zero_pads(pad_ref, n_lat, t)
    pad_ref[SUBLANES:SUBLANES + n_lat, :] = u_ref[0:n_lat, 0:LANES]
    pad_ref[2 * SUBLANES + n_lat:2 * SUBLANES + t, :] = u_ref[n_lat:t, 0:LANES]
    _conv_silu(pad_ref, cw_ref, cb_ref, qk_ref, n, n_lat_chunks)

    def khat_of(qk):
        return jnp.where(hi_half, qk, 0.0) * (ML_DQK ** -0.5)

    sel_r = lax.broadcasted_iota(jnp.int32, (LANES, 4 * LANES), 0)
    sel_c = lax.broadcasted_iota(jnp.int32, (LANES, 4 * LANES), 1)
    spread = (sel_r == (sel_c // LANES) * ML_GATE_REP).astype(BF16)

    def load_gates(c):
        rows = slice(c * CHUNK, (c + 1) * CHUNK)
        g = u_ref[rows, 3 * LANES:4 * LANES] + gb_ref[...]
        g = jnp.where(hi_half, _log_sigmoid(g), g)
        hi, mid, lo = _split3(g)
        d = functools.partial(jnp.dot, preferred_element_type=F32)
        wide = d(hi, spread) + (d(mid, spread) + d(lo, spread))
        rc_ref[0, rows, :] = wide[:, 0:LANES]
        rc_ref[1, rows, :] = wide[:, LANES:2 * LANES]
        return wide[:, 2 * LANES:]

    def emit_gates(c, lf, p):
        rows = slice(c * CHUNK, (c + 1) * CHUNK)
        tot_b = p[CHUNK - 1:CHUNK, LANES:]
        f_dir = (p[:, 0:LANES], tot_b - p[:, LANES:] + lf[:, LANES:])
        for d in range(2):
            fc_ref[d, rows, :] = f_dir[d]
            rc_ref[d, rows, :] = rc_ref[d, rows, :] - f_dir[d]

    _chunk_cumsums(n, tri, load_gates, emit_gates)

    def local(c):
        rows = _chunk_rows(c)
        khat = khat_of(qk_ref[rows, :])
        vaug = jnp.concatenate([u_ref[rows, LANES:2 * LANES], ones], axis=1)
        for d in range(2):
            fc = fc_ref[d, rows, :]
            tt = fc[CHUNK - 1:CHUNK, :] if d == 0 else fc[0:1, :]
            gend = tt + rc_ref[d, rows, :]
            mloc = jnp.max(gend, axis=0, keepdims=True)
            dc_ref[d, c] = _bdot_tn(khat * jnp.exp(gend - mloc), vaug)
            tot_ref[d, c] = _row_bcast(tt)
            mloc_ref[d, c] = _row_bcast(mloc)

    _for_chunks(n, local)

    def scan(s, carry):
        new = []
        for d, idx in ((0, lax.rem(s + n_lat_chunks, n)), (1, n - 1 - s)):
            cst, m = carry[d]
            inc = dc_ref[d, idx]
            dc_ref[d, idx] = cst
            stm_ref[d, idx] = _row_bcast(m)
            tt = tot_ref[d, idx][0:1]
            ml = mloc_ref[d, idx][0:1]
            m_new = jnp.maximum(tt + m, ml)
            a = jnp.exp(tt + m - m_new)[:, 0:1]
            sc = jnp.exp(ml - m_new)[:, 0:1]
            new.append((a * cst + sc * inc, m_new))
        return tuple(new)

    init = (jnp.zeros((LANES, 2 * LANES), F32), jnp.full((1, LANES), M_INIT, F32))
    lax.fori_loop(0, n, scan, (init, init))

    def finish(c):
        rows = _chunk_rows(c)
        qk = qk_ref[rows, :]
        qhat = jnp.where(hi_half, _swap_halves(qk), 0.0)
        s_qk = _bdot_nt(qhat, khat_of(qk))
        vaug = jnp.concatenate([u_ref[rows, LANES:2 * LANES], ones], axis=1)
        rc_t = jnp.where(hi_half, rc_ref[1, rows, :], rc_ref[0, rows, :]).T
        h = None
        for d in range(2):
            fc = fc_ref[d, rows, :]
            rc_row = rc_t[d * (LANES // 2):d * (LANES // 2) + 1, :]
            dlog = jnp.where(masks[d], fc + rc_row, -jnp.inf)
            inter = fc + stm_ref[d, c][0:1]
            m_row = jnp.maximum(inter, jnp.max(dlog, axis=1, keepdims=True))
            w_inter = jnp.exp(inter - m_row)
            nd = (_bdot(s_qk * jnp.exp(dlog - m_row), vaug)
                  + jnp.concatenate([w_inter, w_inter], axis=1) * _bdot(qhat, dc_ref[d, c]))
            hd = nd[:, 0:LANES] / jnp.maximum(jnp.abs(nd[:, LANES:]), jnp.exp(-m_row))
            h = hd if h is None else h + hd
        h = _rms(h) * ng_ref[...]
        o_ref[rows, :] = (h * _sigmoid(u_ref[rows, 2 * LANES:3 * LANES])).astype(o_ref.dtype)

    _for_chunks(n, finish)


def _mlstm_mixer(u, cw, cb, gb, ng, n_lat):
    nb, t, _ = u.shape
    n = t // CHUNK
    head = lambda rows: pl.BlockSpec((None, rows, LANES), lambda b, h: (h, 0, 0))
    return pl.pallas_call(
        functools.partial(_mlstm_kernel, n_lat_chunks=n_lat // CHUNK),
        grid=(nb, ML_HEADS),
        in_specs=[pl.BlockSpec((None, t, ML_HEAD_COLS), lambda b, h: (b, 0, h)),
                  head(SUBLANES), head(1), head(1), head(1)],
        out_specs=pl.BlockSpec((None, t, ML_DV), lambda b, h: (b, 0, h)),
        out_shape=jax.ShapeDtypeStruct((nb, t, ML_WIDTH), BF16),
        scratch_shapes=[pltpu.VMEM((t + 3 * SUBLANES, LANES), F32), pltpu.VMEM((t, LANES), F32),
                        pltpu.VMEM((2, t, LANES), F32), pltpu.VMEM((2, t, LANES), F32),
                        pltpu.VMEM((2, n, LANES, 2 * LANES), F32),
                        pltpu.VMEM((2, n, SUBLANES, LANES), F32), pltpu.VMEM((2, n, SUBLANES, LANES), F32),
                        pltpu.VMEM((2, n, SUBLANES, LANES), F32)],
        compiler_params=pltpu.CompilerParams(dimension_semantics=("arbitrary", "arbitrary"),
                                             vmem_limit_bytes=VMEM_LIMIT),
        name="mlstm_mixer",
    )(u, cw, cb, gb, ng)


M2_CONV_COLS = M2_GROUP_X + 2 * M2_DSTATE
M2_X0 = M2_GROUP_X
M2_DT0 = M2_X0 + M2_CONV_COLS


def _ssd_kernel(u_ref, cw_ref, cb_ref, dtb_ref, alog_ref, dsk_ref, ng_ref, o_ref,
                pad_ref, xc_ref, dt_ref, dh_ref, a_ref, y_ref, *, n_lat_chunks):
    t = u_ref.shape[0]
    n = t // CHUNK
    n_lat = n_lat_chunks * CHUNK
    n_state = 2 * M2_PAIRS
    tri = _tri_incl(CHUNK)
    lo_half = _lane_iota((CHUNK, LANES)) < LANES // 2
    row = _row_iota((CHUNK, CHUNK))
    col = _lane_iota((CHUNK, CHUNK))
    masks = (row >= col, col >= row)
    fwd_cols = _lane_iota((CHUNK, M2_DT_COLS)) < M2_DT_COLS // 2
    a_row = -jnp.exp(alog_ref[...])

    _zero_pads(pad_ref, n_lat, t)
    pad_ref[SUBLANES:SUBLANES + n_lat, :] = u_ref[0:n_lat, M2_X0:M2_DT0]
    pad_ref[2 * SUBLANES + n_lat:2 * SUBLANES + t, :] = u_ref[n_lat:t, M2_X0:M2_DT0]
    _conv_silu(pad_ref, cw_ref, cb_ref, xc_ref, n, n_lat_chunks)

    cum_ref = pad_ref

    spread = (lax.broadcasted_iota(jnp.int32, (LANES, M2_DT_COLS), 0)
              == lax.broadcasted_iota(jnp.int32, (LANES, M2_DT_COLS), 1) // M2_HEADDIM).astype(BF16)

    def load_decay(c):
        rows = slice(c * CHUNK, (c + 1) * CHUNK)
        hi, mid, lo = _split3(u_ref[rows, M2_DT0:])
        d = functools.partial(jnp.dot, preferred_element_type=F32)
        raw = d(hi, spread) + (d(mid, spread) + d(lo, spread))
        dt = _softplus(raw + dtb_ref[...])
        dt_ref[rows, :] = dt
        return dt * a_row

    def emit_decay(c, da, p):
        tot = p[CHUNK - 1:CHUNK, :]
        cum_ref[c * CHUNK:(c + 1) * CHUNK, :] = jnp.where(fwd_cols, p, tot - p + da)

    _chunk_cumsums(n, tri, load_decay, emit_decay)

    def local(c):
        rows = _chunk_rows(c)
        dt = dt_ref[rows, :]
        cum = cum_ref[rows, :]
        tot = jnp.where(fwd_cols[0:1], cum[CHUNK - 1:CHUNK, :], cum[0:1, :])
        x = xc_ref[rows, 0:M2_GROUP_X]
        bm = xc_ref[rows, M2_GROUP_X:M2_GROUP_X + M2_DSTATE]
        cm = xc_ref[rows, M2_GROUP_X + M2_DSTATE:]
        g = _bdot_nt(cm, bm)
        y = [None] * M2_PAIRS
        for d in range(2):
            for p in range(M2_PAIRS):
                k = d * M2_PAIRS + p
                sl = slice(k * LANES, (k + 1) * LANES)
                fp = cum[:, sl]
                tt = tot[:, sl]
                xdt = x[:, p * LANES:(p + 1) * LANES] * dt[:, sl]
                dh_ref[c, k] = _bdot_tn(bm, jnp.exp(tt - fp) * xdt)
                a_ref[c, k] = _row_bcast(jnp.exp(tt))
                sw = _swap_halves(fp)
                fpt = fp.T
                halves = []
                for hh, fh in enumerate((jnp.where(lo_half, fp, sw), jnp.where(lo_half, sw, fp))):
                    f_row = fpt[hh * M2_HEADDIM:hh * M2_HEADDIM + 1, :]
                    dec = jnp.exp(jnp.where(masks[d], fh - f_row, -jnp.inf))
                    halves.append(_bdot(g * dec, xdt))
                yp = jnp.where(lo_half, halves[0], halves[1])
                y[p] = yp if y[p] is None else y[p] + yp
        for p in range(M2_PAIRS):
            y_ref[p, rows, :] = y[p]

    _for_chunks(n, local)

    def scan(s, carry):
        f = lax.rem(s + n_lat_chunks, n)
        g = n - 1 - s
        new = []
        for k in range(n_state):
            idx = f if k < M2_PAIRS else g
            inc = dh_ref[idx, k]
            dh_ref[idx, k] = carry[k]
            new.append(carry[k] * a_ref[idx, k][0:1] + inc)
        return tuple(new)

    lax.fori_loop(0, n, scan, tuple(jnp.zeros((M2_DSTATE, LANES), F32) for _ in range(n_state)))

    def finish(c):
        rows = _chunk_rows(c)
        cum = cum_ref[rows, :]
        cm = xc_ref[rows, M2_GROUP_X + M2_DSTATE:]
        y = [y_ref[p, rows, :] for p in range(M2_PAIRS)]
        for k in range(n_state):
            p = k % M2_PAIRS
            y[p] = y[p] + jnp.exp(cum[:, k * LANES:(k + 1) * LANES]) * _bdot(cm, dh_ref[c, k])
        y = jnp.concatenate(y, axis=1) + dsk_ref[...] * xc_ref[rows, 0:M2_GROUP_X]
        y = _rms(y * _silu(u_ref[rows, 0:M2_X0])) * ng_ref[...]
        o_ref[rows, :] = y.astype(o_ref.dtype)

    _for_chunks(n, finish)


def _ssd_mixer(u, cw, cb, dtb, alog, dsk, ng, n_lat):
    nb, t, _ = u.shape
    n = t // CHUNK
    grp = lambda rows, width: pl.BlockSpec((None, rows, width), lambda b, g: (g, 0, 0))
    return pl.pallas_call(
        functools.partial(_ssd_kernel, n_lat_chunks=n_lat // CHUNK),
        grid=(nb, M2_GROUPS),
        in_specs=[pl.BlockSpec((None, t, M2_GROUP_COLS), lambda b, g: (b, 0, g)),
                  grp(SUBLANES, M2_CONV_COLS), grp(1, M2_CONV_COLS), grp(1, M2_DT_COLS), grp(1, M2_DT_COLS),
                  grp(1, M2_GROUP_X), grp(1, M2_GROUP_X)],
        out_specs=pl.BlockSpec((None, t, M2_GROUP_X), lambda b, g: (b, 0, g)),
        out_shape=jax.ShapeDtypeStruct((nb, t, M2_WIDTH), BF16),
        scratch_shapes=[pltpu.VMEM((t + 3 * SUBLANES, M2_CONV_COLS), F32), pltpu.VMEM((t, M2_CONV_COLS), F32),
                        pltpu.VMEM((t, M2_DT_COLS), F32),
                        pltpu.VMEM((n, 2 * M2_PAIRS, M2_DSTATE, LANES), F32),
                        pltpu.VMEM((n, 2 * M2_PAIRS, SUBLANES, LANES), F32),
                        pltpu.VMEM((M2_PAIRS, t, LANES), F32)],
        compiler_params=pltpu.CompilerParams(dimension_semantics=("arbitrary", "arbitrary"),
                                             vmem_limit_bytes=VMEM_LIMIT),
        name="ssd_mixer",
    )(u, cw, cb, dtb, alog, dsk, ng)


ROW_SLABS = D_MODEL // LANES


def _slab(c, n_rows):
    return pl.ds(c, n_rows, stride=ROW_SLABS)


def _outproj_kernel(x_ref, og_ref, om_ref, osl_ref, osc_ref, w_ref, gate_ref, sh_ref, sc_ref, g2_ref, rw_ref,
                    rb_ref, xo_ref, h2_ref, lg_ref, *, n_lat_tiles):
    o_ssd = jnp.where(pl.program_id(1) < n_lat_tiles, osl_ref[...], osc_ref[...])
    mix = (jnp.dot(og_ref[...], w_ref[0:GLA_WIDTH, :], preferred_element_type=F32)
           + jnp.dot(om_ref[...], w_ref[GLA_WIDTH:GLA_WIDTH + ML_WIDTH, :], preferred_element_type=F32)
           + jnp.dot(o_ssd, w_ref[GLA_WIDTH + ML_WIDTH:, :], preferred_element_type=F32))
    x = x_ref[...] + gate_ref[...] * mix
    xo_ref[...] = x
    h2 = (_rms(x) * g2_ref[...]) * (1.0 + sc_ref[...]) + sh_ref[...]
    for c in range(ROW_SLABS):
        h2_ref[_slab(c, TOK_TILE), :] = h2[:, c * LANES:(c + 1) * LANES]
    lg_ref[...] = _hdot(h2, rw_ref[...]) + rb_ref[...]


def _out_proj(xa, o_gla, o_ml, o_m2_lat, o_m2, w_out, mod3, norm_g, rw, rb, n_lat):
    nb, t, d = xa.shape
    nt = t // TOK_TILE
    n_lat_tiles = n_lat // TOK_TILE
    tok = lambda width: pl.BlockSpec((None, TOK_TILE, width), lambda b, i: (b, i, 0))
    const = lambda r, c: pl.BlockSpec((r, c), lambda b, i: (0, 0))
    mod = lambda which: _mod_spec(which, n_lat_tiles, nb)
    ssd_lat = pl.BlockSpec((None, TOK_TILE, M2_WIDTH), lambda b, i: (b, jnp.minimum(i, n_lat_tiles - 1), 0))
    ssd_ctx = pl.BlockSpec((None, TOK_TILE, M2_WIDTH), lambda b, i: (b, jnp.maximum(i, n_lat_tiles), 0))
    return pl.pallas_call(
        functools.partial(_outproj_kernel, n_lat_tiles=n_lat_tiles),
        grid=(nb, nt),
        in_specs=[tok(d), tok(GLA_WIDTH), tok(ML_WIDTH), ssd_lat, ssd_ctx, const(MIX_WIDTH, d),
                  mod(2), mod(3), mod(4), const(1, d), const(d, LANES), const(1, LANES)],
        out_specs=[tok(d),
                   pl.BlockSpec((TOK_TILE * ROW_SLABS, LANES), lambda b, i: (b * nt + i, 0)),
                   pl.BlockSpec((TOK_TILE, LANES), lambda b, i: (b * nt + i, 0))],
        out_shape=[jax.ShapeDtypeStruct((nb, t, d), F32),
                   jax.ShapeDtypeStruct((nb * t * ROW_SLABS, LANES), F32),
                   jax.ShapeDtypeStruct((nb * t, LANES), F32)],
        compiler_params=pltpu.CompilerParams(dimension_semantics=("arbitrary", "arbitrary"),
                                             vmem_limit_bytes=VMEM_LIMIT),
        name="out_proj",
    )(xa, o_gla, o_ml, o_m2_lat, o_m2, w_out, mod3, mod3, mod3, norm_g.reshape(1, d), rw, rb)


ROUTE_TILE = 256


def _route_kernel(lg_ref, e_ref, gt_ref, rk_ref, cnt_ref, base_ref):
    @pl.when(pl.program_id(0) == 0)
    def _():
        base_ref[...] = jnp.zeros_like(base_ref)

    lane = _lane_iota((ROUTE_TILE, LANES))
    work = lg_ref[...]
    vals, idxs = [], []
    for _ in range(TOP_K):
        m = jnp.max(work, axis=1, keepdims=True)
        idx = jnp.min(jnp.where(work == m, lane, LANES), axis=1, keepdims=True)
        vals.append(m)
        idxs.append(idx)
        work = jnp.where(lane == idx, -jnp.inf, work)
    ex = [jnp.exp(v - vals[0]) for v in vals]
    inv = 1.0 / (ex[0] + ex[1] + ex[2] + ex[3])
    r = _row_iota((ROUTE_TILE, ROUTE_TILE))
    c = _lane_iota((ROUTE_TILE, ROUTE_TILE))
    earlier = (r > c).astype(BF16)
    base = base_ref[0:1, :]
    e_out = jnp.zeros((ROUTE_TILE, LANES), jnp.int32)
    g_out = jnp.zeros((ROUTE_TILE, LANES), F32)
    r_out = jnp.zeros((ROUTE_TILE, LANES), F32)
    for k in range(TOP_K):
        onehot = (lane == idxs[k]).astype(F32)
        within = jnp.dot(earlier, onehot.astype(BF16), preferred_element_type=F32)
        rank = jnp.sum((base + within) * onehot, axis=1, keepdims=True)
        base = base + jnp.sum(onehot, axis=0, keepdims=True)
        e_out = jnp.where(lane == k, idxs[k], e_out)
        g_out = jnp.where(lane == k, ex[k] * inv, g_out)
        r_out = jnp.where(lane == k, rank, r_out)
    base_ref[...] = _row_bcast(base)
    cnt_ref[...] = _row_bcast(base)
    e_ref[...] = e_out
    gt_ref[...] = g_out
    rk_ref[...] = r_out


def _route(logits):
    n_tok = logits.shape[0]
    tile = pl.BlockSpec((ROUTE_TILE, LANES), lambda i: (i, 0))
    return pl.pallas_call(
        _route_kernel,
        grid=(n_tok // ROUTE_TILE,),
        in_specs=[tile],
        out_specs=[tile, tile, tile, pl.BlockSpec((SUBLANES, LANES), lambda i: (0, 0))],
        out_shape=[jax.ShapeDtypeStruct((n_tok, LANES), jnp.int32),
                   jax.ShapeDtypeStruct((n_tok, LANES), F32),
                   jax.ShapeDtypeStruct((n_tok, LANES), F32),
                   jax.ShapeDtypeStruct((SUBLANES, LANES), F32)],
        scratch_shapes=[pltpu.VMEM((SUBLANES, LANES), F32)],
        compiler_params=pltpu.CompilerParams(dimension_semantics=("arbitrary",)),
        name="moe_route",
    )(logits)


DISPATCH_TILE = 512


def _row_slab(r):
    return pl.ds(pl.multiple_of(r * ROW_SLABS, ROW_SLABS), ROW_SLABS)


WAIT_UNROLL = 32
ISSUE_UNROLL = 4


def _drain_rows(src_ref, dst_ref, sem, n_rows):
    assert n_rows % WAIT_UNROLL == 0

    def body(i, carry):
        for _ in range(WAIT_UNROLL):
            pltpu.make_async_copy(src_ref.at[_row_slab(0)], dst_ref.at[_row_slab(0)], sem).wait()
        return carry

    lax.fori_loop(0, n_rows // WAIT_UNROLL, body, 0)


def _dispatch_kernel(dest_ref, h_ref, xb_in, xb_hbm, sem):
    del xb_in

    def issue(i, carry):
        for j in range(ISSUE_UNROLL):
            t = i * ISSUE_UNROLL + j
            for k in range(TOP_K):
                pltpu.make_async_copy(h_ref.at[_row_slab(t)],
                                      xb_hbm.at[_row_slab(dest_ref[0, t * TOP_K + k])], sem).start(priority=k % 2)
        return carry

    lax.fori_loop(0, DISPATCH_TILE // ISSUE_UNROLL, issue, 0)
    _drain_rows(h_ref, xb_hbm, sem, DISPATCH_TILE * TOP_K)


def _dispatch(dest, h2, xb):
    n_tiles = dest.shape[0] // (DISPATCH_TILE * TOP_K)
    return pl.pallas_call(
        _dispatch_kernel,
        grid=(n_tiles,),
        in_specs=[pl.BlockSpec((None, 1, DISPATCH_TILE * TOP_K), lambda i: (i, 0, 0), memory_space=pltpu.SMEM),
                  pl.BlockSpec((DISPATCH_TILE * ROW_SLABS, LANES), lambda i: (i, 0)),
                  pl.BlockSpec(memory_space=pl.ANY)],
        out_specs=pl.BlockSpec(memory_space=pl.ANY),
        out_shape=jax.ShapeDtypeStruct(xb.shape, xb.dtype),
        scratch_shapes=[pltpu.SemaphoreType.DMA(())],
        input_output_aliases={2: 0},
        compiler_params=pltpu.CompilerParams(dimension_semantics=("arbitrary",)),
        name="moe_dispatch",
    )(dest.reshape(n_tiles, 1, DISPATCH_TILE * TOP_K), h2, xb)


def _expert_kernel(be_ref, nv_ref, x_ref, wgu_ref, bgu_ref, wdn_ref, bdn_ref, yb_in, y_ref,
                   wgu_bf, wdn_bf, act_ref):
    del yb_in
    i = pl.program_id(0)
    valid = i < nv_ref[0]
    fresh = jnp.logical_or(i == 0, be_ref[i] != be_ref[jnp.maximum(i - 1, 0)])

    @pl.when(jnp.logical_and(valid, fresh))
    def _():
        for r0 in range(0, D_MODEL, LANES):
            wgu_bf[r0:r0 + LANES, :] = wgu_ref[r0:r0 + LANES, :].astype(BF16)
            wdn_bf[r0:r0 + LANES, :] = wdn_ref[r0:r0 + LANES, :].astype(BF16)

    @pl.when(valid)
    def _():
        x = jnp.concatenate([x_ref[_slab(c, MOE_TILE), :] for c in range(ROW_SLABS)], axis=1).astype(BF16)
        half = D_EXPERT // 4
        for c0 in range(0, D_EXPERT, half):
            glu = jnp.dot(x, wgu_bf[:, c0:c0 + half], preferred_element_type=F32) + bgu_ref[:, c0:c0 + half]
            lin = (jnp.dot(x, wgu_bf[:, D_EXPERT + c0:D_EXPERT + c0 + half], preferred_element_type=F32)
                   + bgu_ref[:, D_EXPERT + c0:D_EXPERT + c0 + half])
            glu = jnp.minimum(glu, SWIGLU_LIMIT)
            lin = jnp.clip(lin, -SWIGLU_LIMIT, SWIGLU_LIMIT)
            act_ref[:, c0:c0 + half] = (glu * _sigmoid(SWIGLU_ALPHA * glu) * (lin + 1.0)).astype(BF16)
        y = jnp.dot(act_ref[...], wdn_bf[...], preferred_element_type=F32) + bdn_ref[...]
        for c in range(ROW_SLABS):
            y_ref[_slab(c, MOE_TILE), :] = y[:, c * LANES:(c + 1) * LANES]


def _experts(layer, block_e, n_valid, xb, w_gu, b_gu, w_dn, b_dn, yb):
    n_blocks = block_e.shape[0]
    blk = lambda i, be, nv: jnp.minimum(i, nv[0] - 1)
    rows = pl.BlockSpec((MOE_TILE * ROW_SLABS, LANES), lambda i, be, nv: (blk(i, be, nv), 0))
    per_e = lambda r, c: pl.BlockSpec((None, None, r, c), lambda i, be, nv: (layer, be[blk(i, be, nv)], 0, 0))
    return pl.pallas_call(
        _expert_kernel,
        grid_spec=pltpu.PrefetchScalarGridSpec(
            num_scalar_prefetch=2,
            grid=(n_blocks,),
            in_specs=[rows, per_e(D_MODEL, 2 * D_EXPERT), per_e(1, 2 * D_EXPERT),
                      per_e(D_EXPERT, D_MODEL), per_e(1, D_MODEL), pl.BlockSpec(memory_space=pl.ANY)],
            out_specs=rows,
            scratch_shapes=[pltpu.VMEM((D_MODEL, 2 * D_EXPERT), BF16), pltpu.VMEM((D_EXPERT, D_MODEL), BF16),
                            pltpu.VMEM((MOE_TILE, D_EXPERT), BF16)]),
        out_shape=jax.ShapeDtypeStruct(xb.shape, F32),
        input_output_aliases={7: 0},
        compiler_params=pltpu.CompilerParams(dimension_semantics=("arbitrary",), vmem_limit_bytes=VMEM_LIMIT),
        name="moe_experts",
    )(block_e, n_valid, xb, w_gu, b_gu[:, :, None, :], w_dn, b_dn[:, :, None, :], yb)


COMBINE_TILE = 128


def _combine_kernel(dest_ref, dnext_ref, yb_hbm, x_ref, gt_ref, mg_ref, o_ref, buf_ref, sem, *, n_steps):
    step = pl.program_id(0) * pl.num_programs(1) + pl.program_id(1)
    slot = lax.rem(step, 2)

    def gather(d_ref, into):
        def body(i, carry):
            for j in range(ISSUE_UNROLL):
                t = i * ISSUE_UNROLL + j
                for k in range(TOP_K):
                    pltpu.make_async_copy(yb_hbm.at[_row_slab(d_ref[0, t * TOP_K + k])],
                                          buf_ref.at[into, _row_slab(k * COMBINE_TILE + t)],
                                          sem.at[into]).start(priority=k % 2)
            return carry

        lax.fori_loop(0, COMBINE_TILE // ISSUE_UNROLL, body, 0)

    @pl.when(step == 0)
    def _():
        gather(dest_ref, 0)

    @pl.when(step + 1 < n_steps)
    def _():
        gather(dnext_ref, 1 - slot)

    _drain_rows(yb_hbm, buf_ref.at[slot], sem.at[slot], COMBINE_TILE * TOP_K)
    gates = gt_ref[...]
    acc = None
    for k in range(TOP_K):
        yk = jnp.concatenate(
            [buf_ref[slot, pl.ds(k * COMBINE_TILE * ROW_SLABS + c, COMBINE_TILE, stride=ROW_SLABS), :]
             for c in range(ROW_SLABS)], axis=1)
        term = gates[:, k:k + 1] * yk
        acc = term if acc is None else acc + term
    o_ref[...] = x_ref[...] + mg_ref[...] * acc


def _combine(dest, yb, xa, gates, mod3, n_lat):
    nb, t, d = xa.shape
    nt = t // COMBINE_TILE
    n_steps = nb * nt
    n_lat_tiles = n_lat // COMBINE_TILE
    tok = pl.BlockSpec((None, COMBINE_TILE, d), lambda b, i: (b, i, 0))
    rows_of = lambda step_of: pl.BlockSpec((None, 1, COMBINE_TILE * TOP_K),
                                           lambda b, i: (step_of(b * nt + i), 0, 0), memory_space=pltpu.SMEM)
    dest3 = dest.reshape(n_steps, 1, COMBINE_TILE * TOP_K)
    return pl.pallas_call(
        functools.partial(_combine_kernel, n_steps=n_steps),
        grid=(nb, nt),
        in_specs=[rows_of(lambda s: s), rows_of(lambda s: jnp.minimum(s + 1, n_steps - 1)),
                  pl.BlockSpec(memory_space=pl.ANY), tok,
                  pl.BlockSpec((COMBINE_TILE, LANES), lambda b, i: (b * nt + i, 0)),
                  _mod_spec(5, n_lat_tiles, nb)],
        out_specs=tok,
        out_shape=jax.ShapeDtypeStruct((nb, t, d), F32),
        scratch_shapes=[pltpu.VMEM((2, TOP_K * COMBINE_TILE * ROW_SLABS, LANES), F32),
                        pltpu.SemaphoreType.DMA((2,))],
        compiler_params=pltpu.CompilerParams(dimension_semantics=("arbitrary", "arbitrary"),
                                             vmem_limit_bytes=VMEM_LIMIT),
        name="moe_combine",
    )(dest3, dest3, yb, xa, gates, mod3)


def _final_norm_kernel(x_ref, g_ref, o_ref):
    o_ref[...] = _rms(x_ref[...]) * g_ref[...]


def _final_norm(xa, g, n_lat):
    nb, _, d = xa.shape
    tok = pl.BlockSpec((None, TOK_TILE, d), lambda b, i: (b, i, 0))
    return pl.pallas_call(
        _final_norm_kernel,
        grid=(nb, n_lat // TOK_TILE),
        in_specs=[tok, pl.BlockSpec((1, d), lambda b, i: (0, 0))],
        out_specs=tok,
        out_shape=jax.ShapeDtypeStruct((nb, n_lat, d), F32),
        compiler_params=pltpu.CompilerParams(dimension_semantics=("arbitrary", "arbitrary")),
        name="final_norm",
    )(xa, g.reshape(1, d))


def _pack_w_in(w_in, cols):
    lead = w_in.shape[:-1]
    parts = []
    i = 0
    while i < len(cols):
        c = int(cols[i])
        j = i + 1
        if c < 0:
            while j < len(cols) and cols[j] < 0:
                j += 1
            parts.append(jnp.zeros(lead + (j - i,), BF16))
        elif j < len(cols) and cols[j] == c:
            while j < len(cols) and cols[j] == c:
                j += 1
            parts.append(jnp.broadcast_to(w_in[..., c:c + 1].astype(BF16), lead + (j - i,)))
        else:
            while j < len(cols) and cols[j] == cols[j - 1] + 1:
                j += 1
            parts.append(w_in[..., c:c + j - i].astype(BF16))
        i = j
    return jnp.concatenate(parts, axis=-1)


def _pack_gla(w_gate2, b_gate):
    nl = w_gate2.shape[0]
    w = w_gate2.reshape(nl, 2, GLA_RANK, GLA_HEADS, GLA_DK).transpose(0, 3, 1, 2, 4)
    z = jnp.zeros((nl, GLA_HEADS, GLA_RANK, GLA_DK), F32)
    top = jnp.concatenate([w[:, :, 0], z], axis=-1)
    bot = jnp.concatenate([z, w[:, :, 1]], axis=-1)
    w2 = jnp.concatenate([top, bot, jnp.zeros((nl, GLA_HEADS, LANES - 2 * GLA_RANK, LANES), F32)], axis=2)
    b2 = b_gate.reshape(nl, 2, GLA_HEADS, GLA_DK).transpose(0, 2, 1, 3).reshape(nl, GLA_HEADS, 1, LANES)
    return w2, b2


def _pack_mlstm(conv_w, conv_b, b_i, b_f, norm_g):
    nl = conv_w.shape[0]
    cw = conv_w.reshape(nl, CONV_W, 2, ML_HEADS, ML_DQK).transpose(0, 3, 1, 2, 4).reshape(nl, ML_HEADS, CONV_W, LANES)
    cw = jnp.pad(cw, ((0, 0), (0, 0), (0, SUBLANES - CONV_W), (0, 0)))
    cb = conv_b.reshape(nl, 2, ML_HEADS, ML_DQK).transpose(0, 2, 1, 3).reshape(nl, ML_HEADS, 1, LANES)
    gates = jnp.stack([b_i[:, 0], b_i[:, 1], b_f[:, 0], b_f[:, 1]], axis=-1)
    gb = jnp.repeat(gates, ML_GATE_REP, axis=-1).reshape(nl, ML_HEADS, 1, LANES)
    return cw, cb, gb, norm_g.reshape(nl, ML_HEADS, 1, ML_DV)


def _pack_ssd(conv_w, conv_b, dt_bias, a_log, d_skip, norm_g):
    nl = conv_w.shape[0]
    bc = M2_GROUPS * M2_DSTATE

    def conv_cols(a):
        lead = a.shape[:-1]
        x = a[..., :M2_WIDTH].reshape(*lead, M2_GROUPS, M2_GROUP_X)
        b = a[..., M2_WIDTH:M2_WIDTH + bc].reshape(*lead, M2_GROUPS, M2_DSTATE)
        c = a[..., M2_WIDTH + bc:].reshape(*lead, M2_GROUPS, M2_DSTATE)
        return jnp.concatenate([x, b, c], axis=-1)

    cw = jnp.pad(conv_cols(conv_w).transpose(0, 2, 1, 3), ((0, 0), (0, 0), (0, SUBLANES - CONV_W), (0, 0)))
    cb = conv_cols(conv_b).reshape(nl, M2_GROUPS, 1, M2_CONV_COLS)

    def per_dir(a):
        a = a.reshape(nl, 2, M2_GROUPS, M2_GROUP_HEADS).transpose(0, 2, 1, 3)
        return jnp.repeat(a, M2_HEADDIM, axis=-1).reshape(nl, M2_GROUPS, 1, M2_DT_COLS)

    dsk = jnp.repeat(d_skip.reshape(nl, M2_GROUPS, M2_GROUP_HEADS), M2_HEADDIM, axis=-1)
    return (cw, cb, per_dir(dt_bias), per_dir(a_log), dsk.reshape(nl, M2_GROUPS, 1, M2_GROUP_X),
            norm_g.reshape(nl, M2_GROUPS, 1, M2_GROUP_X))


def _moe_plan(e_arr, rank_arr, counts_row, n_blocks):
    counts = counts_row[0, :N_EXPERTS].astype(jnp.int32)
    padded = (counts + MOE_TILE - 1) // MOE_TILE * MOE_TILE
    pad_end = jnp.cumsum(padded)
    pad_start = pad_end - padded
    e = e_arr[:, :TOP_K]
    dest = (jnp.take(pad_start, e) + rank_arr[:, :TOP_K].astype(jnp.int32)).reshape(-1)
    block_start = jnp.arange(n_blocks, dtype=jnp.int32) * MOE_TILE
    block_e = jnp.sum((pad_end[None, :] <= block_start[:, None]).astype(jnp.int32), axis=1)
    block_e = jnp.minimum(block_e, N_EXPERTS - 1)
    n_valid = (pad_end[-1:] // MOE_TILE).astype(jnp.int32)
    return dest, block_e, n_valid


def kernel(x, c, ctx, c_ctx, w_mod, b_mod, norm1_g, w_in, gla_w_gate2, gla_b_gate, gla_norm_g, ml_conv_w,
           ml_conv_b, ml_b_i, ml_b_f, ml_norm_g, m2_conv_w, m2_conv_b, m2_dt_bias, m2_A_log, m2_D, m2_norm_g,
           w_out, norm2_g, router_w, router_b, moe_w_gu, moe_b_gu, moe_w_dn, moe_b_dn, final_norm_g):
    nb, n_lat, d = x.shape
    n_ctx = ctx.shape[1]
    n_layers = w_mod.shape[0]
    t = n_lat + n_ctx
    assert d == D_MODEL and n_lat % (GRID_W * SUBLANES) == 0 and n_lat % TOK_TILE == 0 and n_ctx % TOK_TILE == 0
    assert (nb * t) % DISPATCH_TILE == 0

    mod_rows = -(-(nb + 1) // SUBLANES) * SUBLANES
    c_rows = jnp.concatenate([c, c_ctx[None], jnp.zeros((mod_rows - nb - 1, d), F32)], axis=0)
    mod = _mod_table(c_rows, w_mod, b_mod).reshape(n_layers, mod_rows * 6, 1, d)

    w_in_a = _pack_w_in(w_in, _IN_COLS[:GLA_PACK + ML_PACK])
    w_in_b = _pack_w_in(w_in, _IN_COLS[GLA_PACK + ML_PACK:])
    w_out_p = w_out.astype(BF16)
    grid_rows = n_lat // GRID_W
    gla_w2, gla_b2 = _pack_gla(gla_w_gate2, gla_b_gate)
    ml_cw, ml_cb, ml_gb, ml_ng = _pack_mlstm(ml_conv_w, ml_conv_b, ml_b_i, ml_b_f, ml_norm_g)
    m2_cw, m2_cb, m2_dtb, m2_alog, m2_dsk, m2_ng = _pack_ssd(m2_conv_w, m2_conv_b, m2_dt_bias, m2_A_log, m2_D,
                                                            m2_norm_g)
    rw = jnp.pad(router_w, ((0, 0), (0, 0), (0, LANES - N_EXPERTS)))
    rb = jnp.pad(router_b, ((0, 0), (0, LANES - N_EXPERTS)), constant_values=M_INIT).reshape(n_layers, 1, LANES)

    n_assign = nb * t * TOP_K
    n_blocks = n_assign // MOE_TILE + N_EXPERTS
    xb = jnp.zeros((n_blocks * MOE_TILE * ROW_SLABS, LANES), F32)
    yb = jnp.zeros_like(xb)

    xa = jnp.concatenate([x, ctx], axis=1)
    for l in range(n_layers):
        u_gla, u_ml = _in_proj(xa, None, mod[l], norm1_g[l], w_in_a[l], (GLA_PACK, ML_PACK), n_lat)
        x_cm = xa[:, :n_lat].reshape(nb, grid_rows, GRID_W, d).transpose(0, 2, 1, 3).reshape(nb, n_lat, d)
        u_m2, = _in_proj(xa, x_cm, mod[l], norm1_g[l], w_in_b[l], (M2_PACK,), n_lat)
        o_gla = _gla_mixer(u_gla, gla_w2[l], gla_b2[l], gla_norm_g[l].reshape(1, GLA_DV), n_lat)
        o_ml = _mlstm_mixer(u_ml, ml_cw[l], ml_cb[l], ml_gb[l], ml_ng[l], n_lat)
        o_m2 = _ssd_mixer(u_m2, m2_cw[l], m2_cb[l], m2_dtb[l], m2_alog[l], m2_dsk[l], m2_ng[l], n_lat)
        o_m2_lat = (o_m2[:, :n_lat].reshape(nb, GRID_W, grid_rows, M2_WIDTH).transpose(0, 2, 1, 3)
                    .reshape(nb, n_lat, M2_WIDTH))
        xa, h2, logits = _out_proj(xa, o_gla, o_ml, o_m2_lat, o_m2, w_out_p[l], mod[l], norm2_g[l], rw[l], rb[l],
                                   n_lat)
        e_arr, gates, rank_arr, counts = _route(logits)
        dest, block_e, n_valid = _moe_plan(e_arr, rank_arr, counts, n_blocks)
        xb = _dispatch(dest, h2, xb)
        yb = _experts(l, block_e, n_valid, xb, moe_w_gu, moe_b_gu, moe_w_dn, moe_b_dn, yb)
        xa = _combine(dest, yb, xa, gates, mod[l], n_lat)
    return _final_norm(xa, final_norm_g, n_lat)
```

```python
import functools
import math

import numpy as np
import jax
import jax.numpy as jnp
from jax import lax
from jax.experimental import pallas as pl
from jax.experimental.pallas import tpu as pltpu

F32 = jnp.float32
BF16 = jnp.bfloat16

D_MODEL = 1024
GRID_W = 64
GLA_HEADS, GLA_DK, GLA_DV, GLA_RANK = 4, 64, 128, 16
GLA_NORMALIZER = 16.0
ML_HEADS, ML_DQK, ML_DV = 4, 64, 128
M2_HEADS, M2_HEADDIM, M2_GROUPS, M2_DSTATE = 8, 64, 2, 128
CONV_W = 7
CONV_R = CONV_W // 2
N_EXPERTS, TOP_K, D_EXPERT = 32, 4, 1024
SWIGLU_LIMIT, SWIGLU_ALPHA = 7.0, 1.702
EPS = 1e-6
M_INIT = -1e30

GLA_WIDTH = GLA_HEADS * GLA_DV
ML_WIDTH = ML_HEADS * ML_DV
M2_WIDTH = M2_HEADS * M2_HEADDIM
MIX_WIDTH = GLA_WIDTH + ML_WIDTH + M2_WIDTH
GLA_IN = 2 * GLA_HEADS * GLA_DK + 2 * GLA_WIDTH + 2 * GLA_RANK
ML_IN = 2 * ML_HEADS * ML_DQK + 2 * ML_WIDTH + 4 * ML_HEADS
M2_CONV_DIM = M2_WIDTH + 2 * M2_GROUPS * M2_DSTATE
M2_IN = M2_WIDTH + M2_CONV_DIM + 2 * M2_HEADS
IN_WIDTH = GLA_IN + ML_IN + M2_IN

LANES = 128
SUBLANES = 8
CHUNK = 128
TOK_TILE = 256
MOE_TILE = 512
VMEM_LIMIT = 52 * 1024 * 1024

GLA_HEAD_COLS = 3 * LANES
GLA_PACK = GLA_HEADS * GLA_HEAD_COLS + LANES
ML_HEAD_COLS = 4 * LANES
ML_PACK = ML_HEADS * ML_HEAD_COLS
M2_GROUP_HEADS = M2_HEADS // M2_GROUPS
M2_GROUP_X = M2_GROUP_HEADS * M2_HEADDIM
M2_PAIRS = M2_GROUP_HEADS // 2
M2_DT_COLS = 2 * M2_PAIRS * LANES
M2_GROUP_COLS = 2 * M2_GROUP_X + 2 * M2_DSTATE + LANES
M2_PACK = M2_GROUPS * M2_GROUP_COLS
IN_PACK = GLA_PACK + ML_PACK + M2_PACK
ML_GATE_REP = LANES // 4


def _in_proj_column_map():
    cols = []
    qk = GLA_HEADS * GLA_DK
    for h in range(GLA_HEADS):
        cols += list(range(h * GLA_DK, (h + 1) * GLA_DK))
        cols += list(range(qk + h * GLA_DK, qk + (h + 1) * GLA_DK))
        cols += list(range(2 * qk + h * GLA_DV, 2 * qk + (h + 1) * GLA_DV))
        cols += list(range(2 * qk + GLA_WIDTH + h * GLA_DV, 2 * qk + GLA_WIDTH + (h + 1) * GLA_DV))
    lr0 = 2 * qk + 2 * GLA_WIDTH
    cols += list(range(lr0, lr0 + 2 * GLA_RANK)) + [-1] * (LANES - 2 * GLA_RANK)
    a0 = GLA_IN
    qk = ML_HEADS * ML_DQK
    g0 = a0 + 2 * qk + 2 * ML_WIDTH
    for h in range(ML_HEADS):
        cols += list(range(a0 + h * ML_DQK, a0 + (h + 1) * ML_DQK))
        cols += list(range(a0 + qk + h * ML_DQK, a0 + qk + (h + 1) * ML_DQK))
        cols += list(range(a0 + 2 * qk + h * ML_DV, a0 + 2 * qk + (h + 1) * ML_DV))
        cols += list(range(a0 + 2 * qk + ML_WIDTH + h * ML_DV, a0 + 2 * qk + ML_WIDTH + (h + 1) * ML_DV))
        for gate in range(4):
            cols += [g0 + gate * ML_HEADS + h] * ML_GATE_REP
    a1 = GLA_IN + ML_IN
    x0 = a1 + M2_WIDTH
    dt0 = a1 + M2_WIDTH + M2_CONV_DIM
    for g in range(M2_GROUPS):
        cols += list(range(a1 + g * M2_GROUP_X, a1 + (g + 1) * M2_GROUP_X))
        cols += list(range(x0 + g * M2_GROUP_X, x0 + (g + 1) * M2_GROUP_X))
        cols += list(range(x0 + M2_WIDTH + g * M2_DSTATE, x0 + M2_WIDTH + (g + 1) * M2_DSTATE))
        cols += list(range(x0 + M2_WIDTH + M2_GROUPS * M2_DSTATE + g * M2_DSTATE,
                           x0 + M2_WIDTH + M2_GROUPS * M2_DSTATE + (g + 1) * M2_DSTATE))
        for d in range(2):
            for h in range(M2_GROUP_HEADS):
                cols += [dt0 + d * M2_HEADS + g * M2_GROUP_HEADS + h]
        cols += [-1] * (LANES - 2 * M2_GROUP_HEADS)
    cols = np.asarray(cols, np.int32)
    assert cols.shape == (IN_PACK,)
    return cols


_IN_COLS = _in_proj_column_map()


def _bdot(a, b):
    return jnp.dot(a.astype(BF16), b.astype(BF16), preferred_element_type=F32)


def _bdot_nt(a, b):
    return lax.dot_general(a.astype(BF16), b.astype(BF16), (((1,), (1,)), ((), ())),
                           preferred_element_type=F32)


def _bdot_tn(a, b):
    return lax.dot_general(a.astype(BF16), b.astype(BF16), (((0,), (0,)), ((), ())),
                           preferred_element_type=F32)


def _split2(a):
    hi = a.astype(BF16)
    lo = (a - hi.astype(F32)).astype(BF16)
    return hi, lo


def _split3(a):
    hi = a.astype(BF16)
    r = a - hi.astype(F32)
    mid = r.astype(BF16)
    lo = (r - mid.astype(F32)).astype(BF16)
    return hi, mid, lo


def _hdot(a, b):
    ah, al = _split2(a)
    bh, bl = _split2(b)
    d = functools.partial(jnp.dot, preferred_element_type=F32)
    return d(ah, bh) + (d(ah, bl) + d(al, bh))


def _tri_dot(tri, a):
    hi, mid, lo = _split3(a)
    d = functools.partial(jnp.dot, preferred_element_type=F32)
    return d(tri, hi) + (d(tri, mid) + d(tri, lo))


def _rms(x):
    return x * lax.rsqrt(jnp.mean(x * x, axis=-1, keepdims=True) + EPS)


def _sigmoid(x):
    return 1.0 / (1.0 + jnp.exp(-x))


def _silu(x):
    return x * _sigmoid(x)


def _log_sigmoid(x):
    return jnp.minimum(x, 0.0) - jnp.log(1.0 + jnp.exp(-jnp.abs(x)))


def _softplus(x):
    return jnp.maximum(x, 0.0) + jnp.log(1.0 + jnp.exp(-jnp.abs(x)))


def _lane_iota(shape):
    return lax.broadcasted_iota(jnp.int32, shape, len(shape) - 1)


def _row_iota(shape):
    return lax.broadcasted_iota(jnp.int32, shape, len(shape) - 2)


def _lane_rep(x, lo, width):
    lane = _lane_iota(x.shape)
    y = jnp.where((lane >= lo) & (lane < lo + width), x, 0.0)
    w = width
    while w < LANES:
        y = y + pltpu.roll(y, w, axis=1)
        w *= 2
    return y


def _swap_halves(x):
    return pltpu.roll(x, LANES // 2, axis=1)


def _tri_incl(n):
    r = lax.broadcasted_iota(jnp.int32, (n, n), 0)
    c = lax.broadcasted_iota(jnp.int32, (n, n), 1)
    return (r >= c).astype(BF16)


def _mod_kernel(c_ref, w_ref, b_ref, o_ref):
    o_ref[...] = _hdot(_silu(c_ref[...]), w_ref[...]) + b_ref[...]


def _mod_table(c_rows, w_mod, b_mod):
    n_layers, d, n6 = w_mod.shape
    r = c_rows.shape[0]
    tn = 1536
    return pl.pallas_call(
        _mod_kernel,
        grid=(n_layers, n6 // tn),
        in_specs=[pl.BlockSpec((r, d), lambda l, j: (0, 0)),
                  pl.BlockSpec((None, d, tn), lambda l, j: (l, 0, j)),
                  pl.BlockSpec((None, 1, tn), lambda l, j: (l, 0, j))],
        out_specs=pl.BlockSpec((None, r, tn), lambda l, j: (l, 0, j)),
        out_shape=jax.ShapeDtypeStruct((n_layers, r, n6), F32),
        compiler_params=pltpu.CompilerParams(dimension_semantics=("arbitrary", "arbitrary"),
                                             vmem_limit_bytes=VMEM_LIMIT),
        name="adaln_mod",
    )(c_rows, w_mod, b_mod.reshape(n_layers, 1, n6))


def _inproj_kernel(*refs, n_lat_tiles, widths, two_sources):
    if two_sources:
        xl_ref, xc_ref, sh_ref, sc_ref, g_ref, w_ref = refs[:6]
        outs = refs[6:]
        x = jnp.where(pl.program_id(1) < n_lat_tiles, xl_ref[...], xc_ref[...])
    else:
        x_ref, sh_ref, sc_ref, g_ref, w_ref = refs[:5]
        outs = refs[5:]
        x = x_ref[...]
    h = _rms(x) * g_ref[...]
    h = (h * (1.0 + sc_ref[...]) + sh_ref[...]).astype(BF16)
    c0 = 0
    for o_ref, width in zip(outs, widths):
        o_ref[...] = jnp.dot(h, w_ref[:, c0:c0 + width], preferred_element_type=F32)
        c0 += width


def _mod_spec(which, n_lat_tiles, n_batch):
    def imap(b, t):
        row = jnp.where(t < n_lat_tiles, b, n_batch)
        return (row * 6 + which, 0, 0)
    return pl.BlockSpec((None, 1, D_MODEL), imap)


def _in_proj(xa, x_lat, mod3, norm_g, w_pack, widths, n_lat):
    nb, t, d = xa.shape
    n_lat_tiles = n_lat // TOK_TILE
    tok = lambda width: pl.BlockSpec((None, TOK_TILE, width), lambda b, i: (b, i, 0))
    if x_lat is None:
        x_specs, x_args = [tok(d)], (xa,)
    else:
        x_specs = [pl.BlockSpec((None, TOK_TILE, d), lambda b, i: (b, jnp.minimum(i, n_lat_tiles - 1), 0)),
                   pl.BlockSpec((None, TOK_TILE, d), lambda b, i: (b, jnp.maximum(i, n_lat_tiles), 0))]
        x_args = (x_lat, xa)
    return pl.pallas_call(
        functools.partial(_inproj_kernel, n_lat_tiles=n_lat_tiles, widths=widths, two_sources=x_lat is not None),
        grid=(nb, t // TOK_TILE),
        in_specs=x_specs + [_mod_spec(0, n_lat_tiles, nb), _mod_spec(1, n_lat_tiles, nb),
                            pl.BlockSpec((1, d), lambda b, i: (0, 0)),
                            pl.BlockSpec((d, sum(widths)), lambda b, i: (0, 0))],
        out_specs=[tok(w) for w in widths],
        out_shape=[jax.ShapeDtypeStruct((nb, t, w), F32) for w in widths],
        compiler_params=pltpu.CompilerParams(dimension_semantics=("arbitrary", "arbitrary"),
                                             vmem_limit_bytes=VMEM_LIMIT),
        name="in_proj",
    )(*x_args, mod3, mod3, norm_g.reshape(1, d), w_pack)


def _chunk_rows(c):
    return pl.ds(pl.multiple_of(c * CHUNK, CHUNK), CHUNK)


def _row_bcast(row):
    return jnp.broadcast_to(row, (SUBLANES, row.shape[-1]))


CHUNK_UNROLL = 6


def _for_chunks(n, body):
    assert n % CHUNK_UNROLL == 0

    def step(i, carry):
        for j in range(CHUNK_UNROLL):
            body(i * CHUNK_UNROLL + j)
        return carry

    lax.fori_loop(0, n // CHUNK_UNROLL, step, 0)


CUMSUM_GROUP = 6


def _chunk_cumsums(n, tri, load, emit):
    assert n % CUMSUM_GROUP == 0
    for c0 in range(0, n, CUMSUM_GROUP):
        xs = [load(c) for c in range(c0, c0 + CUMSUM_GROUP)]
        width = xs[0].shape[1]
        p = _tri_dot(tri, jnp.concatenate(xs, axis=1))
        for j in range(CUMSUM_GROUP):
            emit(c0 + j, xs[j], p[:, j * width:(j + 1) * width])


def _gla_kernel(u_ref, lr_ref, w2_ref, b2_ref, ng_ref, o_ref,
                cum_ref, attn_ref, q2_ref, ds_ref, a_ref, stf_ref, stb_ref, *, n_lat_chunks):
    t = u_ref.shape[0]
    n = t // CHUNK
    tri = _tri_incl(CHUNK)
    fwd = _lane_iota((CHUNK, LANES)) < LANES // 2
    row = _row_iota((CHUNK, CHUNK))
    col = _lane_iota((CHUNK, CHUNK))
    w2 = w2_ref[...]
    b2 = b2_ref[...]

    def log_decay(i, carry):
        rows = pl.ds(pl.multiple_of(i * TOK_TILE, TOK_TILE), TOK_TILE)
        cum_ref[rows, :] = _log_sigmoid(_hdot(lr_ref[rows, :], w2) + b2) * (1.0 / GLA_NORMALIZER)
        return carry

    lax.fori_loop(0, t // TOK_TILE, log_decay, 0)

    def emit_cum(c, la, p):
        tot = p[CHUNK - 1:CHUNK, :]
        cum_ref[c * CHUNK:(c + 1) * CHUNK, :] = jnp.where(fwd, p, tot - p + la)
        a_ref[c] = _row_bcast(jnp.exp(tot))

    _chunk_cumsums(n, tri, lambda c: cum_ref[c * CHUNK:(c + 1) * CHUNK, :], emit_cum)

    def local(c):
        rows = _chunk_rows(c)
        qk = u_ref[rows, 0:LANES]
        v = u_ref[rows, LANES:2 * LANES]
        cum = cum_ref[rows, :]
        tot = jnp.where(fwd[0:1], cum[CHUNK - 1:CHUNK, :], cum[0:1, :])
        mid = cum[CHUNK // 2:CHUNK // 2 + 1, :]
        sw = _swap_halves(qk)
        qq = jnp.where(fwd, qk, sw) * (GLA_DK ** -0.5)
        kk = jnp.where(fwd, sw, qk)
        qe = qq * jnp.exp(cum - mid)
        ke = kk * jnp.exp(mid - cum)
        af = _bdot_nt(jnp.where(fwd, qe, 0.0), ke)
        ab = _bdot_nt(jnp.where(fwd, 0.0, qe), ke)
        attn = jnp.where(row >= col, af, 0.0) + jnp.where(col >= row, ab, 0.0)
        attn_ref[c] = attn.astype(BF16)
        q2_ref[rows, :] = qq * jnp.exp(cum)
        ds_ref[c] = _bdot_tn(v, kk * jnp.exp(tot - cum))

    _for_chunks(n, local)

    def scan(s, st):
        f = lax.rem(s + n_lat_chunks, n)
        g = n - 1 - s
        stf_ref[f] = st
        stb_ref[g] = st
        a = jnp.where(fwd[0:1], a_ref[f][0:1], a_ref[g][0:1])
        return st * a + jnp.where(fwd, ds_ref[f], ds_ref[g])

    lax.fori_loop(0, n, scan, jnp.zeros((GLA_DV, LANES), F32))

    def finish(c):
        rows = _chunk_rows(c)
        st = jnp.where(fwd, stf_ref[c], stb_ref[c])
        o = _bdot(attn_ref[c], u_ref[rows, LANES:2 * LANES]) + _bdot_nt(q2_ref[rows, :], st)
        o = _rms(o) * ng_ref[...]
        o_ref[rows, :] = (o * _silu(u_ref[rows, 2 * LANES:3 * LANES])).astype(o_ref.dtype)

    _for_chunks(n, finish)


def _gla_mixer(u, w2, b2, ng, n_lat):
    nb, t, _ = u.shape
    n = t // CHUNK
    return pl.pallas_call(
        functools.partial(_gla_kernel, n_lat_chunks=n_lat // CHUNK),
        grid=(nb, GLA_HEADS),
        in_specs=[pl.BlockSpec((None, t, GLA_HEAD_COLS), lambda b, h: (b, 0, h)),
                  pl.BlockSpec((None, t, LANES), lambda b, h: (b, 0, GLA_HEADS * GLA_HEAD_COLS // LANES)),
                  pl.BlockSpec((None, LANES, LANES), lambda b, h: (h, 0, 0)),
                  pl.BlockSpec((None, 1, LANES), lambda b, h: (h, 0, 0)),
                  pl.BlockSpec((1, LANES), lambda b, h: (0, 0))],
        out_specs=pl.BlockSpec((None, t, GLA_DV), lambda b, h: (b, 0, h)),
        out_shape=jax.ShapeDtypeStruct((nb, t, GLA_WIDTH), BF16),
        scratch_shapes=[pltpu.VMEM((t, LANES), F32), pltpu.VMEM((n, CHUNK, CHUNK), BF16),
                        pltpu.VMEM((t, LANES), F32),
                        pltpu.VMEM((n, GLA_DV, LANES), F32), pltpu.VMEM((n, SUBLANES, LANES), F32),
                        pltpu.VMEM((n, GLA_DV, LANES), F32), pltpu.VMEM((n, GLA_DV, LANES), F32)],
        compiler_params=pltpu.CompilerParams(dimension_semantics=("arbitrary", "arbitrary"),
                                             vmem_limit_bytes=VMEM_LIMIT),
        name="gla_mixer",
    )(u, u, w2, b2, ng)


def _zero_pads(pad_ref, n_lat, t):
    z = jnp.zeros((SUBLANES, pad_ref.shape[1]), F32)
    pad_ref[0:SUBLANES, :] = z
    pad_ref[SUBLANES + n_lat:2 * SUBLANES + n_lat, :] = z
    pad_ref[2 * SUBLANES + t:3 * SUBLANES + t, :] = z


def _pad_base(c, n_lat_chunks):
    return c * CHUNK + (SUBLANES if c < n_lat_chunks else 2 * SUBLANES)


def _conv_silu(pad_ref, w_ref, b_ref, out_ref, n, n_lat_chunks):
    width = pad_ref.shape[1]
    for c in range(n):
        base = _pad_base(c, n_lat_chunks)
        for l0 in range(0, width, LANES):
            acc = None
            for j in range(CONV_W):
                term = w_ref[j:j + 1, l0:l0 + LANES] * pad_ref[base + j - CONV_R:base + j - CONV_R + CHUNK,
                                                               l0:l0 + LANES]
                acc = term if acc is None else acc + term
            out_ref[c * CHUNK:(c + 1) * CHUNK, l0:l0 + LANES] = _silu(acc + b_ref[:, l0:l0 + LANES])


def _mlstm_kernel(u_ref, cw_ref, cb_ref, gb_ref, ng_ref, o_ref,
                  pad_ref, qk_ref, fc_ref, rc_ref, dc_ref, tot_ref, mloc_ref, stm_ref, sqk_ref, *, n_lat_chunks):
    t = u_ref.shape[0]
    n = t // CHUNK
    n_lat = n_lat_chunks * CHUNK
    tri = _tri_incl(CHUNK)
    lane = _lane_iota((CHUNK, LANES))
    hi_half = lane >= LANES // 2
    row = _row_iota((CHUNK, CHUNK))
    col = _lane_iota((CHUNK, CHUNK))
    masks = (row >= col, col >= row)
    ones = jnp.ones((CHUNK, LANES), F32)

    _zero_pads(pad_ref, n_lat, t)
    pad_ref[SUBLANES:SUBLANES + n_lat, :] = u_ref[0:n_lat, 0:LANES]
    pad_ref[2 * SUBLANES + n_lat:2 * SUBLANES + t, :] = u_ref[n_lat:t, 0:LANES]
    _conv_silu(pad_ref, cw_ref, cb_ref, qk_ref, n, n_lat_chunks)

    def khat_of(qk):
        return jnp.where(hi_half, qk, 0.0) * (ML_DQK ** -0.5)

    sel_r = lax.broadcasted_iota(jnp.int32, (LANES, 4 * LANES), 0)
    sel_c = lax.broadcasted_iota(jnp.int32, (LANES, 4 * LANES), 1)
    spread = (sel_r == (sel_c // LANES) * ML_GATE_REP).astype(BF16)

    def load_gates(c):
        rows = slice(c * CHUNK, (c + 1) * CHUNK)
        g = u_ref[rows, 3 * LANES:4 * LANES] + gb_ref[...]
        g = jnp.where(hi_half, _log_sigmoid(g), g)
        hi, mid, lo = _split3(g)
        d = functools.partial(jnp.dot, preferred_element_type=F32)
        wide = d(hi, spread) + (d(mid, spread) + d(lo, spread))
        rc_ref[0, rows, :] = wide[:, 0:LANES]
        rc_ref[1, rows, :] = wide[:, LANES:2 * LANES]
        return wide[:, 2 * LANES:]

    def emit_gates(c, lf, p):
        rows = slice(c * CHUNK, (c + 1) * CHUNK)
        tot_b = p[CHUNK - 1:CHUNK, LANES:]
        f_dir = (p[:, 0:LANES], tot_b - p[:, LANES:] + lf[:, LANES:])
        for d in range(2):
            fc_ref[d, rows, :] = f_dir[d]
            rc_ref[d, rows, :] = rc_ref[d, rows, :] - f_dir[d]

    _chunk_cumsums(n, tri, load_gates, emit_gates)

    def local(c):
        rows = _chunk_rows(c)
        qk = qk_ref[rows, :]
        khat = khat_of(qk)
        sqk_ref[c] = _bdot_nt(jnp.where(hi_half, _swap_halves(qk), 0.0), khat)
        vaug = jnp.concatenate([u_ref[rows, LANES:2 * LANES], ones], axis=1)
        for d in range(2):
            fc = fc_ref[d, rows, :]
            tt = fc[CHUNK - 1:CHUNK, :] if d == 0 else fc[0:1, :]
            gend = tt + rc_ref[d, rows, :]
            mloc = jnp.max(gend, axis=0, keepdims=True)
            dc_ref[d, c] = _bdot_tn(khat * jnp.exp(gend - mloc), vaug)
            tot_ref[d, c] = _row_bcast(tt)
            mloc_ref[d, c] = _row_bcast(mloc)

    _for_chunks(n, local)

    def scan(s, carry):
        new = []
        for d, idx in ((0, lax.rem(s + n_lat_chunks, n)), (1, n - 1 - s)):
            cst, m = carry[d]
            inc = dc_ref[d, idx]
            dc_ref[d, idx] = cst
            stm_ref[d, idx] = _row_bcast(m)
            tt = tot_ref[d, idx][0:1]
            ml = mloc_ref[d, idx][0:1]
            m_new = jnp.maximum(tt + m, ml)
            a = jnp.exp(tt + m - m_new)[:, 0:1]
            sc = jnp.exp(ml - m_new)[:, 0:1]
            new.append((a * cst + sc * inc, m_new))
        return tuple(new)

    init = (jnp.zeros((LANES, 2 * LANES), F32), jnp.full((1, LANES), M_INIT, F32))
    lax.fori_loop(0, n, scan, (init, init))

    def finish(c):
        rows = _chunk_rows(c)
        qk = qk_ref[rows, :]
        qhat = jnp.where(hi_half, _swap_halves(qk), 0.0)
        s_qk = sqk_ref[c]
        vaug = jnp.concatenate([u_ref[rows, LANES:2 * LANES], ones], axis=1)
        rc_t = jnp.where(hi_half, rc_ref[1, rows, :], rc_ref[0, rows, :]).T
        h = None
        for d in range(2):
            fc = fc_ref[d, rows, :]
            rc_row = rc_t[d * (LANES // 2):d * (LANES // 2) + 1, :]
            dlog = jnp.where(masks[d], fc + rc_row, -jnp.inf)
            inter = fc + stm_ref[d, c][0:1]
            m_row = jnp.maximum(inter, jnp.max(dlog, axis=1, keepdims=True))
            w_inter = jnp.exp(inter - m_row)
            nd = (_bdot(s_qk * jnp.exp(dlog - m_row), vaug)
                  + jnp.concatenate([w_inter, w_inter], axis=1) * _bdot(qhat, dc_ref[d, c]))
            hd = nd[:, 0:LANES] / jnp.maximum(jnp.abs(nd[:, LANES:]), jnp.exp(-m_row))
            h = hd if h is None else h + hd
        h = _rms(h) * ng_ref[...]
        o_ref[rows, :] = (h * _sigmoid(u_ref[rows, 2 * LANES:3 * LANES])).astype(o_ref.dtype)

    _for_chunks(n, finish)


def _mlstm_mixer(u, cw, cb, gb, ng, n_lat):
    nb, t, _ = u.shape
    n = t // CHUNK
    head = lambda rows: pl.BlockSpec((None, rows, LANES), lambda b, h: (h, 0, 0))
    return pl.pallas_call(
        functools.partial(_mlstm_kernel, n_lat_chunks=n_lat // CHUNK),
        grid=(nb, ML_HEADS),
        in_specs=[pl.BlockSpec((None, t, ML_HEAD_COLS), lambda b, h: (b, 0, h)),
                  head(SUBLANES), head(1), head(1), head(1)],
        out_specs=pl.BlockSpec((None, t, ML_DV), lambda b, h: (b, 0, h)),
        out_shape=jax.ShapeDtypeStruct((nb, t, ML_WIDTH), BF16),
        scratch_shapes=[pltpu.VMEM((t + 3 * SUBLANES, LANES), F32), pltpu.VMEM((t, LANES), F32),
                        pltpu.VMEM((2, t, LANES), F32), pltpu.VMEM((2, t, LANES), F32),
                        pltpu.VMEM((2, n, LANES, 2 * LANES), F32),
                        pltpu.VMEM((2, n, SUBLANES, LANES), F32), pltpu.VMEM((2, n, SUBLANES, LANES), F32),
                        pltpu.VMEM((2, n, SUBLANES, LANES), F32), pltpu.VMEM((n, CHUNK, CHUNK), F32)],
        compiler_params=pltpu.CompilerParams(dimension_semantics=("arbitrary", "arbitrary"),
                                             vmem_limit_bytes=VMEM_LIMIT),
        name="mlstm_mixer",
    )(u, cw, cb, gb, ng)


M2_CONV_COLS = M2_GROUP_X + 2 * M2_DSTATE
M2_X0 = M2_GROUP_X
M2_DT0 = M2_X0 + M2_CONV_COLS


def _ssd_kernel(u_ref, cw_ref, cb_ref, dtb_ref, alog_ref, dsk_ref, ng_ref, o_ref,
                pad_ref, xc_ref, dt_ref, dh_ref, a_ref, y_ref, *, n_lat_chunks):
    t = u_ref.shape[0]
    n = t // CHUNK
    n_lat = n_lat_chunks * CHUNK
    n_state = 2 * M2_PAIRS
    tri = _tri_incl(CHUNK)
    lo_half = _lane_iota((CHUNK, LANES)) < LANES // 2
    row = _row_iota((CHUNK, CHUNK))
    col = _lane_iota((CHUNK, CHUNK))
    masks = (row >= col, col >= row)
    fwd_cols = _lane_iota((CHUNK, M2_DT_COLS)) < M2_DT_COLS // 2
    a_row = -jnp.exp(alog_ref[...])

    _zero_pads(pad_ref, n_lat, t)
    pad_ref[SUBLANES:SUBLANES + n_lat, :] = u_ref[0:n_lat, M2_X0:M2_DT0]
    pad_ref[2 * SUBLANES + n_lat:2 * SUBLANES + t, :] = u_ref[n_lat:t, M2_X0:M2_DT0]
    _conv_silu(pad_ref, cw_ref, cb_ref, xc_ref, n, n_lat_chunks)

    cum_ref = pad_ref

    spread = (lax.broadcasted_iota(jnp.int32, (LANES, M2_DT_COLS), 0)
              == lax.broadcasted_iota(jnp.int32, (LANES, M2_DT_COLS), 1) // M2_HEADDIM).astype(BF16)

    def load_decay(c):
        rows = slice(c * CHUNK, (c + 1) * CHUNK)
        hi, mid, lo = _split3(u_ref[rows, M2_DT0:])
        d = functools.partial(jnp.dot, preferred_element_type=F32)
        raw = d(hi, spread) + (d(mid, spread) + d(lo, spread))
        dt = _softplus(raw + dtb_ref[...])
        dt_ref[rows, :] = dt
        return dt * a_row

    def emit_decay(c, da, p):
        tot = p[CHUNK - 1:CHUNK, :]
        cum_ref[c * CHUNK:(c + 1) * CHUNK, :] = jnp.where(fwd_cols, p, tot - p + da)

    _chunk_cumsums(n, tri, load_decay, emit_decay)

    def local(c):
        rows = _chunk_rows(c)
        dt = dt_ref[rows, :]
        cum = cum_ref[rows, :]
        tot = jnp.where(fwd_cols[0:1], cum[CHUNK - 1:CHUNK, :], cum[0:1, :])
        x = xc_ref[rows, 0:M2_GROUP_X]
        bm = xc_ref[rows, M2_GROUP_X:M2_GROUP_X + M2_DSTATE]
        cm = xc_ref[rows, M2_GROUP_X + M2_DSTATE:]
        g = _bdot_nt(cm, bm)
        y = [None] * M2_PAIRS
        for d in range(2):
            for p in range(M2_PAIRS):
                k = d * M2_PAIRS + p
                sl = slice(k * LANES, (k + 1) * LANES)
                fp = cum[:, sl]
                tt = tot[:, sl]
                xdt = x[:, p * LANES:(p + 1) * LANES] * dt[:, sl]
                dh_ref[c, k] = _bdot_tn(bm, jnp.exp(tt - fp) * xdt)
                a_ref[c, k] = _row_bcast(jnp.exp(tt))
                sw = _swap_halves(fp)
                fpt = fp.T
                halves = []
                for hh, fh in enumerate((jnp.where(lo_half, fp, sw), jnp.where(lo_half, sw, fp))):
                    f_row = fpt[hh * M2_HEADDIM:hh * M2_HEADDIM + 1, :]
                    dec = jnp.exp(jnp.where(masks[d], fh - f_row, -jnp.inf))
                    halves.append(_bdot(g * dec, xdt))
                yp = jnp.where(lo_half, halves[0], halves[1])
                y[p] = yp if y[p] is None else y[p] + yp
        for p in range(M2_PAIRS):
            y_ref[p, rows, :] = y[p]

    _for_chunks(n, local)

    def scan(s, carry):
        f = lax.rem(s + n_lat_chunks, n)
        g = n - 1 - s
        new = []
        for k in range(n_state):
            idx = f if k < M2_PAIRS else g
            inc = dh_ref[idx, k]
            dh_ref[idx, k] = carry[k]
            new.append(carry[k] * a_ref[idx, k][0:1] + inc)
        return tuple(new)

    lax.fori_loop(0, n, scan, tuple(jnp.zeros((M2_DSTATE, LANES), F32) for _ in range(n_state)))

    def finish(c):
        rows = _chunk_rows(c)
        cum = cum_ref[rows, :]
        cm = xc_ref[rows, M2_GROUP_X + M2_DSTATE:]
        y = [y_ref[p, rows, :] for p in range(M2_PAIRS)]
        for k in range(n_state):
            p = k % M2_PAIRS
            y[p] = y[p] + jnp.exp(cum[:, k * LANES:(k + 1) * LANES]) * _bdot(cm, dh_ref[c, k])
        y = jnp.concatenate(y, axis=1) + dsk_ref[...] * xc_ref[rows, 0:M2_GROUP_X]
        y = _rms(y * _silu(u_ref[rows, 0:M2_X0])) * ng_ref[...]
        o_ref[rows, :] = y.astype(o_ref.dtype)

    _for_chunks(n, finish)


def _ssd_mixer(u, cw, cb, dtb, alog, dsk, ng, n_lat):
    nb, t, _ = u.shape
    n = t // CHUNK
    grp = lambda rows, width: pl.BlockSpec((None, rows, width), lambda b, g: (g, 0, 0))
    return pl.pallas_call(
        functools.partial(_ssd_kernel, n_lat_chunks=n_lat // CHUNK),
        grid=(nb, M2_GROUPS),
        in_specs=[pl.BlockSpec((None, t, M2_GROUP_COLS), lambda b, g: (b, 0, g)),
                  grp(SUBLANES, M2_CONV_COLS), grp(1, M2_CONV_COLS), grp(1, M2_DT_COLS), grp(1, M2_DT_COLS),
                  grp(1, M2_GROUP_X), grp(1, M2_GROUP_X)],
        out_specs=pl.BlockSpec((None, t, M2_GROUP_X), lambda b, g: (b, 0, g)),
        out_shape=jax.ShapeDtypeStruct((nb, t, M2_WIDTH), BF16),
        scratch_shapes=[pltpu.VMEM((t + 3 * SUBLANES, M2_CONV_COLS), F32), pltpu.VMEM((t, M2_CONV_COLS), F32),
                        pltpu.VMEM((t, M2_DT_COLS), F32),
                        pltpu.VMEM((n, 2 * M2_PAIRS, M2_DSTATE, LANES), F32),
                        pltpu.VMEM((n, 2 * M2_PAIRS, SUBLANES, LANES), F32),
                        pltpu.VMEM((M2_PAIRS, t, LANES), F32)],
        compiler_params=pltpu.CompilerParams(dimension_semantics=("arbitrary", "arbitrary"),
                                             vmem_limit_bytes=VMEM_LIMIT),
        name="ssd_mixer",
    )(u, cw, cb, dtb, alog, dsk, ng)


ROW_SLABS = D_MODEL // LANES


def _slab(c, n_rows):
    return pl.ds(c, n_rows, stride=ROW_SLABS)


def _outproj_kernel(x_ref, og_ref, om_ref, osl_ref, osc_ref, w_ref, gate_ref, sh_ref, sc_ref, g2_ref, rw_ref,
                    rb_ref, xo_ref, h2_ref, lg_ref, *, n_lat_tiles):
    o_ssd = jnp.where(pl.program_id(1) < n_lat_tiles, osl_ref[...], osc_ref[...])
    mix = (jnp.dot(og_ref[...], w_ref[0:GLA_WIDTH, :], preferred_element_type=F32)
           + jnp.dot(om_ref[...], w_ref[GLA_WIDTH:GLA_WIDTH + ML_WIDTH, :], preferred_element_type=F32)
           + jnp.dot(o_ssd, w_ref[GLA_WIDTH + ML_WIDTH:, :], preferred_element_type=F32))
    x = x_ref[...] + gate_ref[...] * mix
    xo_ref[...] = x
    h2 = (_rms(x) * g2_ref[...]) * (1.0 + sc_ref[...]) + sh_ref[...]
    for c in range(ROW_SLABS):
        h2_ref[_slab(c, TOK_TILE), :] = h2[:, c * LANES:(c + 1) * LANES]
    lg_ref[...] = _hdot(h2, rw_ref[...]) + rb_ref[...]


def _out_proj(xa, o_gla, o_ml, o_m2_lat, o_m2, w_out, mod3, norm_g, rw, rb, n_lat):
    nb, t, d = xa.shape
    nt = t // TOK_TILE
    n_lat_tiles = n_lat // TOK_TILE
    tok = lambda width: pl.BlockSpec((None, TOK_TILE, width), lambda b, i: (b, i, 0))
    const = lambda r, c: pl.BlockSpec((r, c), lambda b, i: (0, 0))
    mod = lambda which: _mod_spec(which, n_lat_tiles, nb)
    ssd_lat = pl.BlockSpec((None, TOK_TILE, M2_WIDTH), lambda b, i: (b, jnp.minimum(i, n_lat_tiles - 1), 0))
    ssd_ctx = pl.BlockSpec((None, TOK_TILE, M2_WIDTH), lambda b, i: (b, jnp.maximum(i, n_lat_tiles), 0))
    return pl.pallas_call(
        functools.partial(_outproj_kernel, n_lat_tiles=n_lat_tiles),
        grid=(nb, nt),
        in_specs=[tok(d), tok(GLA_WIDTH), tok(ML_WIDTH), ssd_lat, ssd_ctx, const(MIX_WIDTH, d),
                  mod(2), mod(3), mod(4), const(1, d), const(d, LANES), const(1, LANES)],
        out_specs=[tok(d),
                   pl.BlockSpec((TOK_TILE * ROW_SLABS, LANES), lambda b, i: (b * nt + i, 0)),
                   pl.BlockSpec((TOK_TILE, LANES), lambda b, i: (b * nt + i, 0))],
        out_shape=[jax.ShapeDtypeStruct((nb, t, d), F32),
                   jax.ShapeDtypeStruct((nb * t * ROW_SLABS, LANES), F32),
                   jax.ShapeDtypeStruct((nb * t, LANES), F32)],
        compiler_params=pltpu.CompilerParams(dimension_semantics=("arbitrary", "arbitrary"),
                                             vmem_limit_bytes=VMEM_LIMIT),
        name="out_proj",
    )(xa, o_gla, o_ml, o_m2_lat, o_m2, w_out, mod3, mod3, mod3, norm_g.reshape(1, d), rw, rb)


ROUTE_TILE = 256


def _route_kernel(lg_ref, e_ref, gt_ref, rk_ref, cnt_ref, base_ref):
    @pl.when(pl.program_id(0) == 0)
    def _():
        base_ref[...] = jnp.zeros_like(base_ref)

    lane = _lane_iota((ROUTE_TILE, LANES))
    work = lg_ref[...]
    vals, idxs = [], []
    for _ in range(TOP_K):
        m = jnp.max(work, axis=1, keepdims=True)
        idx = jnp.min(jnp.where(work == m, lane, LANES), axis=1, keepdims=True)
        vals.append(m)
        idxs.append(idx)
        work = jnp.where(lane == idx, -jnp.inf, work)
    ex = [jnp.exp(v - vals[0]) for v in vals]
    inv = 1.0 / (ex[0] + ex[1] + ex[2] + ex[3])
    r = _row_iota((ROUTE_TILE, ROUTE_TILE))
    c = _lane_iota((ROUTE_TILE, ROUTE_TILE))
    earlier = (r > c).astype(BF16)
    base = base_ref[0:1, :]
    e_out = jnp.zeros((ROUTE_TILE, LANES), jnp.int32)
    g_out = jnp.zeros((ROUTE_TILE, LANES), F32)
    r_out = jnp.zeros((ROUTE_TILE, LANES), F32)
    for k in range(TOP_K):
        onehot = (lane == idxs[k]).astype(F32)
        within = jnp.dot(earlier, onehot.astype(BF16), preferred_element_type=F32)
        rank = jnp.sum((base + within) * onehot, axis=1, keepdims=True)
        base = base + jnp.sum(onehot, axis=0, keepdims=True)
        e_out = jnp.where(lane == k, idxs[k], e_out)
        g_out = jnp.where(lane == k, ex[k] * inv, g_out)
        r_out = jnp.where(lane == k, rank, r_out)
    base_ref[...] = _row_bcast(base)
    cnt_ref[...] = _row_bcast(base)
    e_ref[...] = e_out
    gt_ref[...] = g_out
    rk_ref[...] = r_out


def _route(logits):
    n_tok = logits.shape[0]
    tile = pl.BlockSpec((ROUTE_TILE, LANES), lambda i: (i, 0))
    return pl.pallas_call(
        _route_kernel,
        grid=(n_tok // ROUTE_TILE,),
        in_specs=[tile],
        out_specs=[tile, tile, tile, pl.BlockSpec((SUBLANES, LANES), lambda i: (0, 0))],
        out_shape=[jax.ShapeDtypeStruct((n_tok, LANES), jnp.int32),
                   jax.ShapeDtypeStruct((n_tok, LANES), F32),
                   jax.ShapeDtypeStruct((n_tok, LANES), F32),
                   jax.ShapeDtypeStruct((SUBLANES, LANES), F32)],
        scratch_shapes=[pltpu.VMEM((SUBLANES, LANES), F32)],
        compiler_params=pltpu.CompilerParams(dimension_semantics=("arbitrary",)),
        name="moe_route",
    )(logits)


DISPATCH_TILE = 512


def _row_slab(r):
    return pl.ds(pl.multiple_of(r * ROW_SLABS, ROW_SLABS), ROW_SLABS)


WAIT_UNROLL = 32
ISSUE_UNROLL = 4


def _drain_rows(src_ref, dst_ref, sem, n_rows):
    assert n_rows % WAIT_UNROLL == 0

    def body(i, carry):
        for _ in range(WAIT_UNROLL):
            pltpu.make_async_copy(src_ref.at[_row_slab(0)], dst_ref.at[_row_slab(0)], sem).wait()
        return carry

    lax.fori_loop(0, n_rows // WAIT_UNROLL, body, 0)


def _dispatch_kernel(dest_ref, h_ref, xb_in, xb_hbm, sem):
    del xb_in

    def issue(i, carry):
        for j in range(ISSUE_UNROLL):
            t = i * ISSUE_UNROLL + j
            for k in range(TOP_K):
                pltpu.make_async_copy(h_ref.at[_row_slab(t)],
                                      xb_hbm.at[_row_slab(dest_ref[0, t * TOP_K + k])], sem).start(priority=k % 2)
        return carry

    lax.fori_loop(0, DISPATCH_TILE // ISSUE_UNROLL, issue, 0)
    _drain_rows(h_ref, xb_hbm, sem, DISPATCH_TILE * TOP_K)


def _dispatch(dest, h2, xb):
    n_tiles = dest.shape[0] // (DISPATCH_TILE * TOP_K)
    return pl.pallas_call(
        _dispatch_kernel,
        grid=(n_tiles,),
        in_specs=[pl.BlockSpec((None, 1, DISPATCH_TILE * TOP_K), lambda i: (i, 0, 0), memory_space=pltpu.SMEM),
                  pl.BlockSpec((DISPATCH_TILE * ROW_SLABS, LANES), lambda i: (i, 0)),
                  pl.BlockSpec(memory_space=pl.ANY)],
        out_specs=pl.BlockSpec(memory_space=pl.ANY),
        out_shape=jax.ShapeDtypeStruct(xb.shape, xb.dtype),
        scratch_shapes=[pltpu.SemaphoreType.DMA(())],
        input_output_aliases={2: 0},
        compiler_params=pltpu.CompilerParams(dimension_semantics=("arbitrary",)),
        name="moe_dispatch",
    )(dest.reshape(n_tiles, 1, DISPATCH_TILE * TOP_K), h2, xb)


def _expert_kernel(be_ref, nv_ref, x_ref, wgu_ref, bgu_ref, wdn_ref, bdn_ref, y_ref, wgu_bf, wdn_bf, act_ref):
    i = pl.program_id(0)
    valid = i < nv_ref[0]
    fresh = jnp.logical_or(i == 0, be_ref[i] != be_ref[jnp.maximum(i - 1, 0)])

    @pl.when(jnp.logical_not(valid))
    def _():
        y_ref[...] = jnp.zeros_like(y_ref)

    @pl.when(jnp.logical_and(valid, fresh))
    def _():
        for r0 in range(0, D_MODEL, LANES):
            wgu_bf[r0:r0 + LANES, :] = wgu_ref[r0:r0 + LANES, :].astype(BF16)
            wdn_bf[r0:r0 + LANES, :] = wdn_ref[r0:r0 + LANES, :].astype(BF16)

    @pl.when(valid)
    def _():
        x = jnp.concatenate([x_ref[_slab(c, MOE_TILE), :] for c in range(ROW_SLABS)], axis=1).astype(BF16)
        half = D_EXPERT // 2
        for c0 in range(0, D_EXPERT, half):
            glu = jnp.dot(x, wgu_bf[:, c0:c0 + half], preferred_element_type=F32) + bgu_ref[:, c0:c0 + half]
            lin = (jnp.dot(x, wgu_bf[:, D_EXPERT + c0:D_EXPERT + c0 + half], preferred_element_type=F32)
                   + bgu_ref[:, D_EXPERT + c0:D_EXPERT + c0 + half])
            glu = jnp.minimum(glu, SWIGLU_LIMIT)
            lin = jnp.clip(lin, -SWIGLU_LIMIT, SWIGLU_LIMIT)
            act_ref[:, c0:c0 + half] = (glu * _sigmoid(SWIGLU_ALPHA * glu) * (lin + 1.0)).astype(BF16)
        y = jnp.dot(act_ref[...], wdn_bf[...], preferred_element_type=F32) + bdn_ref[...]
        for c in range(ROW_SLABS):
            y_ref[_slab(c, MOE_TILE), :] = y[:, c * LANES:(c + 1) * LANES]


def _experts(layer, block_e, n_valid, xb, w_gu, b_gu, w_dn, b_dn):
    n_blocks = block_e.shape[0]
    blk = lambda i, be, nv: jnp.minimum(i, nv[0] - 1)
    rows = pl.BlockSpec((MOE_TILE * ROW_SLABS, LANES), lambda i, be, nv: (blk(i, be, nv), 0))
    per_e = lambda r, c: pl.BlockSpec((None, None, r, c), lambda i, be, nv: (layer, be[blk(i, be, nv)], 0, 0))
    return pl.pallas_call(
        _expert_kernel,
        grid_spec=pltpu.PrefetchScalarGridSpec(
            num_scalar_prefetch=2,
            grid=(n_blocks,),
            in_specs=[rows, per_e(D_MODEL, 2 * D_EXPERT), per_e(1, 2 * D_EXPERT),
                      per_e(D_EXPERT, D_MODEL), per_e(1, D_MODEL)],
            out_specs=pl.BlockSpec((MOE_TILE * ROW_SLABS, LANES), lambda i, be, nv: (i, 0)),
            scratch_shapes=[pltpu.VMEM((D_MODEL, 2 * D_EXPERT), BF16), pltpu.VMEM((D_EXPERT, D_MODEL), BF16),
                            pltpu.VMEM((MOE_TILE, D_EXPERT), BF16)]),
        out_shape=jax.ShapeDtypeStruct(xb.shape, F32),
        compiler_params=pltpu.CompilerParams(dimension_semantics=("arbitrary",), vmem_limit_bytes=VMEM_LIMIT),
        name="moe_experts",
    )(block_e, n_valid, xb, w_gu, b_gu[:, :, None, :], w_dn, b_dn[:, :, None, :])


COMBINE_TILE = 128


def _combine_kernel(dest_ref, dnext_ref, yb_hbm, x_ref, gt_ref, mg_ref, o_ref, buf_ref, sem, *, n_steps):
    step = pl.program_id(0) * pl.num_programs(1) + pl.program_id(1)
    slot = lax.rem(step, 2)

    def gather(d_ref, into):
        def body(i, carry):
            for j in range(ISSUE_UNROLL):
                t = i * ISSUE_UNROLL + j
                for k in range(TOP_K):
                    pltpu.make_async_copy(yb_hbm.at[_row_slab(d_ref[0, t * TOP_K + k])],
                                          buf_ref.at[into, _row_slab(k * COMBINE_TILE + t)],
                                          sem.at[into]).start(priority=k % 2)
            return carry

        lax.fori_loop(0, COMBINE_TILE // ISSUE_UNROLL, body, 0)

    @pl.when(step == 0)
    def _():
        gather(dest_ref, 0)

    @pl.when(step + 1 < n_steps)
    def _():
        gather(dnext_ref, 1 - slot)

    _drain_rows(yb_hbm, buf_ref.at[slot], sem.at[slot], COMBINE_TILE * TOP_K)
    gates = gt_ref[...]
    acc = None
    for k in range(TOP_K):
        yk = jnp.concatenate(
            [buf_ref[slot, pl.ds(k * COMBINE_TILE * ROW_SLABS + c, COMBINE_TILE, stride=ROW_SLABS), :]
             for c in range(ROW_SLABS)], axis=1)
        term = gates[:, k:k + 1] * yk
        acc = term if acc is None else acc + term
    o_ref[...] = x_ref[...] + mg_ref[...] * acc


def _combine(dest, yb, xa, gates, mod3, n_lat):
    nb, t, d = xa.shape
    nt = t // COMBINE_TILE
    n_steps = nb * nt
    n_lat_tiles = n_lat // COMBINE_TILE
    tok = pl.BlockSpec((None, COMBINE_TILE, d), lambda b, i: (b, i, 0))
    rows_of = lambda step_of: pl.BlockSpec((None, 1, COMBINE_TILE * TOP_K),
                                           lambda b, i: (step_of(b * nt + i), 0, 0), memory_space=pltpu.SMEM)
    dest3 = dest.reshape(n_steps, 1, COMBINE_TILE * TOP_K)
    return pl.pallas_call(
        functools.partial(_combine_kernel, n_steps=n_steps),
        grid=(nb, nt),
        in_specs=[rows_of(lambda s: s), rows_of(lambda s: jnp.minimum(s + 1, n_steps - 1)),
                  pl.BlockSpec(memory_space=pl.ANY), tok,
                  pl.BlockSpec((COMBINE_TILE, LANES), lambda b, i: (b * nt + i, 0)),
                  _mod_spec(5, n_lat_tiles, nb)],
        out_specs=tok,
        out_shape=jax.ShapeDtypeStruct((nb, t, d), F32),
        scratch_shapes=[pltpu.VMEM((2, TOP_K * COMBINE_TILE * ROW_SLABS, LANES), F32),
                        pltpu.SemaphoreType.DMA((2,))],
        compiler_params=pltpu.CompilerParams(dimension_semantics=("arbitrary", "arbitrary"),
                                             vmem_limit_bytes=VMEM_LIMIT),
        name="moe_combine",
    )(dest3, dest3, yb, xa, gates, mod3)


def _final_norm_kernel(x_ref, g_ref, o_ref):
    o_ref[...] = _rms(x_ref[...]) * g_ref[...]


def _final_norm(xa, g, n_lat):
    nb, _, d = xa.shape
    tok = pl.BlockSpec((None, TOK_TILE, d), lambda b, i: (b, i, 0))
    return pl.pallas_call(
        _final_norm_kernel,
        grid=(nb, n_lat // TOK_TILE),
        in_specs=[tok, pl.BlockSpec((1, d), lambda b, i: (0, 0))],
        out_specs=tok,
        out_shape=jax.ShapeDtypeStruct((nb, n_lat, d), F32),
        compiler_params=pltpu.CompilerParams(dimension_semantics=("arbitrary", "arbitrary")),
        name="final_norm",
    )(xa, g.reshape(1, d))


def _pack_w_in(w_in, cols):
    lead = w_in.shape[:-1]
    parts = []
    i = 0
    while i < len(cols):
        c = int(cols[i])
        j = i + 1
        if c < 0:
            while j < len(cols) and cols[j] < 0:
                j += 1
            parts.append(jnp.zeros(lead + (j - i,), BF16))
        elif j < len(cols) and cols[j] == c:
            while j < len(cols) and cols[j] == c:
                j += 1
            parts.append(jnp.broadcast_to(w_in[..., c:c + 1].astype(BF16), lead + (j - i,)))
        else:
            while j < len(cols) and cols[j] == cols[j - 1] + 1:
                j += 1
            parts.append(w_in[..., c:c + j - i].astype(BF16))
        i = j
    return jnp.concatenate(parts, axis=-1)


def _pack_gla(w_gate2, b_gate):
    nl = w_gate2.shape[0]
    w = w_gate2.reshape(nl, 2, GLA_RANK, GLA_HEADS, GLA_DK).transpose(0, 3, 1, 2, 4)
    z = jnp.zeros((nl, GLA_HEADS, GLA_RANK, GLA_DK), F32)
    top = jnp.concatenate([w[:, :, 0], z], axis=-1)
    bot = jnp.concatenate([z, w[:, :, 1]], axis=-1)
    w2 = jnp.concatenate([top, bot, jnp.zeros((nl, GLA_HEADS, LANES - 2 * GLA_RANK, LANES), F32)], axis=2)
    b2 = b_gate.reshape(nl, 2, GLA_HEADS, GLA_DK).transpose(0, 2, 1, 3).reshape(nl, GLA_HEADS, 1, LANES)
    return w2, b2


def _pack_mlstm(conv_w, conv_b, b_i, b_f, norm_g):
    nl = conv_w.shape[0]
    cw = conv_w.reshape(nl, CONV_W, 2, ML_HEADS, ML_DQK).transpose(0, 3, 1, 2, 4).reshape(nl, ML_HEADS, CONV_W, LANES)
    cw = jnp.pad(cw, ((0, 0), (0, 0), (0, SUBLANES - CONV_W), (0, 0)))
    cb = conv_b.reshape(nl, 2, ML_HEADS, ML_DQK).transpose(0, 2, 1, 3).reshape(nl, ML_HEADS, 1, LANES)
    gates = jnp.stack([b_i[:, 0], b_i[:, 1], b_f[:, 0], b_f[:, 1]], axis=-1)
    gb = jnp.repeat(gates, ML_GATE_REP, axis=-1).reshape(nl, ML_HEADS, 1, LANES)
    return cw, cb, gb, norm_g.reshape(nl, ML_HEADS, 1, ML_DV)


def _pack_ssd(conv_w, conv_b, dt_bias, a_log, d_skip, norm_g):
    nl = conv_w.shape[0]
    bc = M2_GROUPS * M2_DSTATE

    def conv_cols(a):
        lead = a.shape[:-1]
        x = a[..., :M2_WIDTH].reshape(*lead, M2_GROUPS, M2_GROUP_X)
        b = a[..., M2_WIDTH:M2_WIDTH + bc].reshape(*lead, M2_GROUPS, M2_DSTATE)
        c = a[..., M2_WIDTH + bc:].reshape(*lead, M2_GROUPS, M2_DSTATE)
        return jnp.concatenate([x, b, c], axis=-1)

    cw = jnp.pad(conv_cols(conv_w).transpose(0, 2, 1, 3), ((0, 0), (0, 0), (0, SUBLANES - CONV_W), (0, 0)))
    cb = conv_cols(conv_b).reshape(nl, M2_GROUPS, 1, M2_CONV_COLS)

    def per_dir(a):
        a = a.reshape(nl, 2, M2_GROUPS, M2_GROUP_HEADS).transpose(0, 2, 1, 3)
        return jnp.repeat(a, M2_HEADDIM, axis=-1).reshape(nl, M2_GROUPS, 1, M2_DT_COLS)

    dsk = jnp.repeat(d_skip.reshape(nl, M2_GROUPS, M2_GROUP_HEADS), M2_HEADDIM, axis=-1)
    return (cw, cb, per_dir(dt_bias), per_dir(a_log), dsk.reshape(nl, M2_GROUPS, 1, M2_GROUP_X),
            norm_g.reshape(nl, M2_GROUPS, 1, M2_GROUP_X))


def _moe_plan(e_arr, rank_arr, counts_row, n_blocks):
    counts = counts_row[0, :N_EXPERTS].astype(jnp.int32)
    padded = (counts + MOE_TILE - 1) // MOE_TILE * MOE_TILE
    pad_end = jnp.cumsum(padded)
    pad_start = pad_end - padded
    e = e_arr[:, :TOP_K]
    dest = (jnp.take(pad_start, e) + rank_arr[:, :TOP_K].astype(jnp.int32)).reshape(-1)
    block_start = jnp.arange(n_blocks, dtype=jnp.int32) * MOE_TILE
    block_e = jnp.sum((pad_end[None, :] <= block_start[:, None]).astype(jnp.int32), axis=1)
    block_e = jnp.minimum(block_e, N_EXPERTS - 1)
    n_valid = (pad_end[-1:] // MOE_TILE).astype(jnp.int32)
    return dest, block_e, n_valid


def kernel(x, c, ctx, c_ctx, w_mod, b_mod, norm1_g, w_in, gla_w_gate2, gla_b_gate, gla_norm_g, ml_conv_w,
           ml_conv_b, ml_b_i, ml_b_f, ml_norm_g, m2_conv_w, m2_conv_b, m2_dt_bias, m2_A_log, m2_D, m2_norm_g,
           w_out, norm2_g, router_w, router_b, moe_w_gu, moe_b_gu, moe_w_dn, moe_b_dn, final_norm_g):
    nb, n_lat, d = x.shape
    n_ctx = ctx.shape[1]
    n_layers = w_mod.shape[0]
    t = n_lat + n_ctx
    assert d == D_MODEL and n_lat % (GRID_W * SUBLANES) == 0 and n_lat % TOK_TILE == 0 and n_ctx % TOK_TILE == 0
    assert (nb * t) % DISPATCH_TILE == 0

    mod_rows = -(-(nb + 1) // SUBLANES) * SUBLANES
    c_rows = jnp.concatenate([c, c_ctx[None], jnp.zeros((mod_rows - nb - 1, d), F32)], axis=0)
    mod = _mod_table(c_rows, w_mod, b_mod).reshape(n_layers, mod_rows * 6, 1, d)

    w_in_a = _pack_w_in(w_in, _IN_COLS[:GLA_PACK + ML_PACK])
    w_in_b = _pack_w_in(w_in, _IN_COLS[GLA_PACK + ML_PACK:])
    w_out_p = w_out.astype(BF16)
    grid_rows = n_lat // GRID_W
    gla_w2, gla_b2 = _pack_gla(gla_w_gate2, gla_b_gate)
    ml_cw, ml_cb, ml_gb, ml_ng = _pack_mlstm(ml_conv_w, ml_conv_b, ml_b_i, ml_b_f, ml_norm_g)
    m2_cw, m2_cb, m2_dtb, m2_alog, m2_dsk, m2_ng = _pack_ssd(m2_conv_w, m2_conv_b, m2_dt_bias, m2_A_log, m2_D,
                                                            m2_norm_g)
    rw = jnp.pad(router_w, ((0, 0), (0, 0), (0, LANES - N_EXPERTS)))
    rb = jnp.pad(router_b, ((0, 0), (0, LANES - N_EXPERTS)), constant_values=M_INIT).reshape(n_layers, 1, LANES)

    n_assign = nb * t * TOP_K
    n_blocks = n_assign // MOE_TILE + N_EXPERTS
    xb = jnp.zeros((n_blocks * MOE_TILE * ROW_SLABS, LANES), F32)

    xa = jnp.concatenate([x, ctx], axis=1)
    for l in range(n_layers):
        u_gla, u_ml = _in_proj(xa, None, mod[l], norm1_g[l], w_in_a[l], (GLA_PACK, ML_PACK), n_lat)
        x_cm = xa[:, :n_lat].reshape(nb, grid_rows, GRID_W, d).transpose(0, 2, 1, 3).reshape(nb, n_lat, d)
        u_m2, = _in_proj(xa, x_cm, mod[l], norm1_g[l], w_in_b[l], (M2_PACK,), n_lat)
        o_gla = _gla_mixer(u_gla, gla_w2[l], gla_b2[l], gla_norm_g[l].reshape(1, GLA_DV), n_lat)
        o_ml = _mlstm_mixer(u_ml, ml_cw[l], ml_cb[l], ml_gb[l], ml_ng[l], n_lat)
        o_m2 = _ssd_mixer(u_m2, m2_cw[l], m2_cb[l], m2_dtb[l], m2_alog[l], m2_dsk[l], m2_ng[l], n_lat)
        o_m2_lat = (o_m2[:, :n_lat].reshape(nb, GRID_W, grid_rows, M2_WIDTH).transpose(0, 2, 1, 3)
                    .reshape(nb, n_lat, M2_WIDTH))
        xa, h2, logits = _out_proj(xa, o_gla, o_ml, o_m2_lat, o_m2, w_out_p[l], mod[l], norm2_g[l], rw[l], rb[l],
                                   n_lat)
        e_arr, gates, rank_arr, counts = _route(logits)
        dest, block_e, n_valid = _moe_plan(e_arr, rank_arr, counts, n_blocks)
        xb = _dispatch(dest, h2, xb)
        yb = _experts(l, block_e, n_valid, xb, moe_w_gu, moe_b_gu, moe_w_dn, moe_b_dn)
        xa = _combine(dest, yb, xa, gates, mod[l], n_lat)
    return _final_norm(xa, final_norm_g, n_lat)
```

```python
import functools
import math

import numpy as np
import jax
import jax.numpy as jnp
from jax import lax
from jax.experimental import pallas as pl
from jax.experimental.pallas import tpu as pltpu

F32 = jnp.float32
BF16 = jnp.bfloat16

D_MODEL = 1024
GRID_W = 64
GLA_HEADS, GLA_DK, GLA_DV, GLA_RANK = 4, 64, 128, 16
GLA_NORMALIZER = 16.0
ML_HEADS, ML_DQK, ML_DV = 4, 64, 128
M2_HEADS, M2_HEADDIM, M2_GROUPS, M2_DSTATE = 8, 64, 2, 128
CONV_W = 7
CONV_R = CONV_W // 2
N_EXPERTS, TOP_K, D_EXPERT = 32, 4, 1024
SWIGLU_LIMIT, SWIGLU_ALPHA = 7.0, 1.702
EPS = 1e-6
M_INIT = -1e30

GLA_WIDTH = GLA_HEADS * GLA_DV
ML_WIDTH = ML_HEADS * ML_DV
M2_WIDTH = M2_HEADS * M2_HEADDIM
MIX_WIDTH = GLA_WIDTH + ML_WIDTH + M2_WIDTH
GLA_IN = 2 * GLA_HEADS * GLA_DK + 2 * GLA_WIDTH + 2 * GLA_RANK
ML_IN = 2 * ML_HEADS * ML_DQK + 2 * ML_WIDTH + 4 * ML_HEADS
M2_CONV_DIM = M2_WIDTH + 2 * M2_GROUPS * M2_DSTATE
M2_IN = M2_WIDTH + M2_CONV_DIM + 2 * M2_HEADS
IN_WIDTH = GLA_IN + ML_IN + M2_IN

LANES = 128
SUBLANES = 8
CHUNK = 128
TOK_TILE = 256
MOE_TILE = 512
VMEM_LIMIT = 52 * 1024 * 1024

GLA_HEAD_COLS = 3 * LANES
GLA_PACK = GLA_HEADS * GLA_HEAD_COLS + LANES
ML_HEAD_COLS = 4 * LANES
ML_PACK = ML_HEADS * ML_HEAD_COLS
M2_GROUP_HEADS = M2_HEADS // M2_GROUPS
M2_GROUP_X = M2_GROUP_HEADS * M2_HEADDIM
M2_PAIRS = M2_GROUP_HEADS // 2
M2_DT_COLS = 2 * M2_PAIRS * LANES
M2_GROUP_COLS = 2 * M2_GROUP_X + 2 * M2_DSTATE + LANES
M2_PACK = M2_GROUPS * M2_GROUP_COLS
IN_PACK = GLA_PACK + ML_PACK + M2_PACK
ML_GATE_REP = LANES // 4


def _in_proj_column_map():
    cols = []
    qk = GLA_HEADS * GLA_DK
    for h in range(GLA_HEADS):
        cols += list(range(h * GLA_DK, (h + 1) * GLA_DK))
        cols += list(range(qk + h * GLA_DK, qk + (h + 1) * GLA_DK))
        cols += list(range(2 * qk + h * GLA_DV, 2 * qk + (h + 1) * GLA_DV))
        cols += list(range(2 * qk + GLA_WIDTH + h * GLA_DV, 2 * qk + GLA_WIDTH + (h + 1) * GLA_DV))
    lr0 = 2 * qk + 2 * GLA_WIDTH
    cols += list(range(lr0, lr0 + 2 * GLA_RANK)) + [-1] * (LANES - 2 * GLA_RANK)
    a0 = GLA_IN
    qk = ML_HEADS * ML_DQK
    g0 = a0 + 2 * qk + 2 * ML_WIDTH
    for h in range(ML_HEADS):
        cols += list(range(a0 + h * ML_DQK, a0 + (h + 1) * ML_DQK))
        cols += list(range(a0 + qk + h * ML_DQK, a0 + qk + (h + 1) * ML_DQK))
        cols += list(range(a0 + 2 * qk + h * ML_DV, a0 + 2 * qk + (h + 1) * ML_DV))
        cols += list(range(a0 + 2 * qk + ML_WIDTH + h * ML_DV, a0 + 2 * qk + ML_WIDTH + (h + 1) * ML_DV))
        for gate in range(4):
            cols += [g0 + gate * ML_HEADS + h] * ML_GATE_REP
    a1 = GLA_IN + ML_IN
    x0 = a1 + M2_WIDTH
    dt0 = a1 + M2_WIDTH + M2_CONV_DIM
    for g in range(M2_GROUPS):
        cols += list(range(a1 + g * M2_GROUP_X, a1 + (g + 1) * M2_GROUP_X))
        cols += list(range(x0 + g * M2_GROUP_X, x0 + (g + 1) * M2_GROUP_X))
        cols += list(range(x0 + M2_WIDTH + g * M2_DSTATE, x0 + M2_WIDTH + (g + 1) * M2_DSTATE))
        cols += list(range(x0 + M2_WIDTH + M2_GROUPS * M2_DSTATE + g * M2_DSTATE,
                           x0 + M2_WIDTH + M2_GROUPS * M2_DSTATE + (g + 1) * M2_DSTATE))
        for d in range(2):
            for h in range(M2_GROUP_HEADS):
                cols += [dt0 + d * M2_HEADS + g * M2_GROUP_HEADS + h]
        cols += [-1] * (LANES - 2 * M2_GROUP_HEADS)
    cols = np.asarray(cols, np.int32)
    assert cols.shape == (IN_PACK,)
    return cols


_IN_COLS = _in_proj_column_map()


def _bdot(a, b):
    return jnp.dot(a.astype(BF16), b.astype(BF16), preferred_element_type=F32)


def _bdot_nt(a, b):
    return lax.dot_general(a.astype(BF16), b.astype(BF16), (((1,), (1,)), ((), ())),
                           preferred_element_type=F32)


def _bdot_tn(a, b):
    return lax.dot_general(a.astype(BF16), b.astype(BF16), (((0,), (0,)), ((), ())),
                           preferred_element_type=F32)


def _split2(a):
    hi = a.astype(BF16)
    lo = (a - hi.astype(F32)).astype(BF16)
    return hi, lo


def _split3(a):
    hi = a.astype(BF16)
    r = a - hi.astype(F32)
    mid = r.astype(BF16)
    lo = (r - mid.astype(F32)).astype(BF16)
    return hi, mid, lo


def _hdot(a, b):
    ah, al = _split2(a)
    bh, bl = _split2(b)
    d = functools.partial(jnp.dot, preferred_element_type=F32)
    return d(ah, bh) + (d(ah, bl) + d(al, bh))


def _tri_dot(tri, a):
    hi, mid, lo = _split3(a)
    d = functools.partial(jnp.dot, preferred_element_type=F32)
    return d(tri, hi) + (d(tri, mid) + d(tri, lo))


def _rms(x):
    return x * lax.rsqrt(jnp.mean(x * x, axis=-1, keepdims=True) + EPS)


def _sigmoid(x):
    return 1.0 / (1.0 + jnp.exp(-x))


def _silu(x):
    return x * _sigmoid(x)


def _log_sigmoid(x):
    return jnp.minimum(x, 0.0) - jnp.log(1.0 + jnp.exp(-jnp.abs(x)))


def _softplus(x):
    return jnp.maximum(x, 0.0) + jnp.log(1.0 + jnp.exp(-jnp.abs(x)))


def _lane_iota(shape):
    return lax.broadcasted_iota(jnp.int32, shape, len(shape) - 1)


def _row_iota(shape):
    return lax.broadcasted_iota(jnp.int32, shape, len(shape) - 2)


def _lane_rep(x, lo, width):
    lane = _lane_iota(x.shape)
    y = jnp.where((lane >= lo) & (lane < lo + width), x, 0.0)
    w = width
    while w < LANES:
        y = y + pltpu.roll(y, w, axis=1)
        w *= 2
    return y


def _swap_halves(x):
    return pltpu.roll(x, LANES // 2, axis=1)


def _tri_incl(n):
    r = lax.broadcasted_iota(jnp.int32, (n, n), 0)
    c = lax.broadcasted_iota(jnp.int32, (n, n), 1)
    return (r >= c).astype(BF16)


def _mod_kernel(c_ref, w_ref, b_ref, o_ref):
    o_ref[...] = _hdot(_silu(c_ref[...]), w_ref[...]) + b_ref[...]


def _mod_table(c_rows, w_mod, b_mod):
    n_layers, d, n6 = w_mod.shape
    r = c_rows.shape[0]
    tn = 1536
    return pl.pallas_call(
        _mod_kernel,
        grid=(n_layers, n6 // tn),
        in_specs=[pl.BlockSpec((r, d), lambda l, j: (0, 0)),
                  pl.BlockSpec((None, d, tn), lambda l, j: (l, 0, j)),
                  pl.BlockSpec((None, 1, tn), lambda l, j: (l, 0, j))],
        out_specs=pl.BlockSpec((None, r, tn), lambda l, j: (l, 0, j)),
        out_shape=jax.ShapeDtypeStruct((n_layers, r, n6), F32),
        compiler_params=pltpu.CompilerParams(dimension_semantics=("arbitrary", "arbitrary"),
                                             vmem_limit_bytes=VMEM_LIMIT),
        name="adaln_mod",
    )(c_rows, w_mod, b_mod.reshape(n_layers, 1, n6))


def _inproj_kernel(*refs, n_lat_tiles, widths, two_sources):
    if two_sources:
        xl_ref, xc_ref, sh_ref, sc_ref, g_ref, w_ref = refs[:6]
        outs = refs[6:]
        x = jnp.where(pl.program_id(1) < n_lat_tiles, xl_ref[...], xc_ref[...])
    else:
        x_ref, sh_ref, sc_ref, g_ref, w_ref = refs[:5]
        outs = refs[5:]
        x = x_ref[...]
    h = _rms(x) * g_ref[...]
    h = (h * (1.0 + sc_ref[...]) + sh_ref[...]).astype(BF16)
    c0 = 0
    for o_ref, width in zip(outs, widths):
        o_ref[...] = jnp.dot(h, w_ref[:, c0:c0 + width], preferred_element_type=F32)
        c0 += width


def _mod_spec(which, n_lat_tiles, n_batch):
    def imap(b, t):
        row = jnp.where(t < n_lat_tiles, b, n_batch)
        return (row * 6 + which, 0, 0)
    return pl.BlockSpec((None, 1, D_MODEL), imap)


def _in_proj(xa, x_lat, mod3, norm_g, w_pack, widths, n_lat):
    nb, t, d = xa.shape
    n_lat_tiles = n_lat // TOK_TILE
    tok = lambda width: pl.BlockSpec((None, TOK_TILE, width), lambda b, i: (b, i, 0))
    if x_lat is None:
        x_specs, x_args = [tok(d)], (xa,)
    else:
        x_specs = [pl.BlockSpec((None, TOK_TILE, d), lambda b, i: (b, jnp.minimum(i, n_lat_tiles - 1), 0)),
                   pl.BlockSpec((None, TOK_TILE, d), lambda b, i: (b, jnp.maximum(i, n_lat_tiles), 0))]
        x_args = (x_lat, xa)
    return pl.pallas_call(
        functools.partial(_inproj_kernel, n_lat_tiles=n_lat_tiles, widths=widths, two_sources=x_lat is not None),
        grid=(nb, t // TOK_TILE),
        in_specs=x_specs + [_mod_spec(0, n_lat_tiles, nb), _mod_spec(1, n_lat_tiles, nb),
                            pl.BlockSpec((1, d), lambda b, i: (0, 0)),
                            pl.BlockSpec((d, sum(widths)), lambda b, i: (0, 0))],
        out_specs=[tok(w) for w in widths],
        out_shape=[jax.ShapeDtypeStruct((nb, t, w), F32) for w in widths],
        compiler_params=pltpu.CompilerParams(dimension_semantics=("arbitrary", "arbitrary"),
                                             vmem_limit_bytes=VMEM_LIMIT),
        name="in_proj",
    )(*x_args, mod3, mod3, norm_g.reshape(1, d), w_pack)


def _chunk_rows(c):
    return pl.ds(pl.multiple_of(c * CHUNK, CHUNK), CHUNK)


def _row_bcast(row):
    return jnp.broadcast_to(row, (SUBLANES, row.shape[-1]))


CHUNK_UNROLL = 6


def _for_chunks(n, body):
    assert n % CHUNK_UNROLL == 0

    def step(i, carry):
        for j in range(CHUNK_UNROLL):
            body(i * CHUNK_UNROLL + j)
        return carry

    lax.fori_loop(0, n // CHUNK_UNROLL, step, 0)


CUMSUM_GROUP = 6


def _chunk_cumsums(n, tri, load, emit):
    assert n % CUMSUM_GROUP == 0
    for c0 in range(0, n, CUMSUM_GROUP):
        xs = [load(c) for c in range(c0, c0 + CUMSUM_GROUP)]
        width = xs[0].shape[1]
        p = _tri_dot(tri, jnp.concatenate(xs, axis=1))
        for j in range(CUMSUM_GROUP):
            emit(c0 + j, xs[j], p[:, j * width:(j + 1) * width])


def _gla_kernel(u_ref, lr_ref, w2_ref, b2_ref, ng_ref, o_ref,
                cum_ref, attn_ref, q2_ref, ds_ref, a_ref, stf_ref, stb_ref, *, n_lat_chunks):
    t = u_ref.shape[0]
    n = t // CHUNK
    tri = _tri_incl(CHUNK)
    fwd = _lane_iota((CHUNK, LANES)) < LANES // 2
    row = _row_iota((CHUNK, CHUNK))
    col = _lane_iota((CHUNK, CHUNK))
    w2 = w2_ref[...]
    b2 = b2_ref[...]

    def log_decay(i, carry):
        rows = pl.ds(pl.multiple_of(i * TOK_TILE, TOK_TILE), TOK_TILE)
        cum_ref[rows, :] = _log_sigmoid(_hdot(lr_ref[rows, :], w2) + b2) * (1.0 / GLA_NORMALIZER)
        return carry

    lax.fori_loop(0, t // TOK_TILE, log_decay, 0)

    def emit_cum(c, la, p):
        tot = p[CHUNK - 1:CHUNK, :]
        cum_ref[c * CHUNK:(c + 1) * CHUNK, :] = jnp.where(fwd, p, tot - p + la)
        a_ref[c] = _row_bcast(jnp.exp(tot))

    _chunk_cumsums(n, tri, lambda c: cum_ref[c * CHUNK:(c + 1) * CHUNK, :], emit_cum)

    def local(c):
        rows = _chunk_rows(c)
        qk = u_ref[rows, 0:LANES]
        v = u_ref[rows, LANES:2 * LANES]
        cum = cum_ref[rows, :]
        tot = jnp.where(fwd[0:1], cum[CHUNK - 1:CHUNK, :], cum[0:1, :])
        mid = cum[CHUNK // 2:CHUNK // 2 + 1, :]
        sw = _swap_halves(qk)
        qq = jnp.where(fwd, qk, sw) * (GLA_DK ** -0.5)
        kk = jnp.where(fwd, sw, qk)
        qe = qq * jnp.exp(cum - mid)
        ke = kk * jnp.exp(mid - cum)
        af = _bdot_nt(jnp.where(fwd, qe, 0.0), ke)
        ab = _bdot_nt(jnp.where(fwd, 0.0, qe), ke)
        attn = jnp.where(row >= col, af, 0.0) + jnp.where(col >= row, ab, 0.0)
        attn_ref[c] = attn.astype(BF16)
        q2_ref[rows, :] = qq * jnp.exp(cum)
        ds_ref[c] = _bdot_tn(v, kk * jnp.exp(tot - cum))

    _for_chunks(n, local)

    def scan(s, st):
        f = lax.rem(s + n_lat_chunks, n)
        g = n - 1 - s
        stf_ref[f] = st
        stb_ref[g] = st
        a = jnp.where(fwd[0:1], a_ref[f][0:1], a_ref[g][0:1])
        return st * a + jnp.where(fwd, ds_ref[f], ds_ref[g])

    lax.fori_loop(0, n, scan, jnp.zeros((GLA_DV, LANES), F32))

    def finish(c):
        rows = _chunk_rows(c)
        st = jnp.where(fwd, stf_ref[c], stb_ref[c])
        o = _bdot(attn_ref[c], u_ref[rows, LANES:2 * LANES]) + _bdot_nt(q2_ref[rows, :], st)
        o = _rms(o) * ng_ref[...]
        o_ref[rows, :] = (o * _silu(u_ref[rows, 2 * LANES:3 * LANES])).astype(o_ref.dtype)

    _for_chunks(n, finish)


def _gla_mixer(u, w2, b2, ng, n_lat):
    nb, t, _ = u.shape
    n = t // CHUNK
    return pl.pallas_call(
        functools.partial(_gla_kernel, n_lat_chunks=n_lat // CHUNK),
        grid=(nb, GLA_HEADS),
        in_specs=[pl.BlockSpec((None, t, GLA_HEAD_COLS), lambda b, h: (b, 0, h)),
                  pl.BlockSpec((None, t, LANES), lambda b, h: (b, 0, GLA_HEADS * GLA_HEAD_COLS // LANES)),
                  pl.BlockSpec((None, LANES, LANES), lambda b, h: (h, 0, 0)),
                  pl.BlockSpec((None, 1, LANES), lambda b, h: (h, 0, 0)),
                  pl.BlockSpec((1, LANES), lambda b, h: (0, 0))],
        out_specs=pl.BlockSpec((None, t, GLA_DV), lambda b, h: (b, 0, h)),
        out_shape=jax.ShapeDtypeStruct((nb, t, GLA_WIDTH), BF16),
        scratch_shapes=[pltpu.VMEM((t, LANES), F32), pltpu.VMEM((n, CHUNK, CHUNK), BF16),
                        pltpu.VMEM((t, LANES), F32),
                        pltpu.VMEM((n, GLA_DV, LANES), F32), pltpu.VMEM((n, SUBLANES, LANES), F32),
                        pltpu.VMEM((n, GLA_DV, LANES), F32), pltpu.VMEM((n, GLA_DV, LANES), F32)],
        compiler_params=pltpu.CompilerParams(dimension_semantics=("arbitrary", "arbitrary"),
                                             vmem_limit_bytes=VMEM_LIMIT),
        name="gla_mixer",
    )(u, u, w2, b2, ng)


def _zero_pads(pad_ref, n_lat, t):
    z = jnp.zeros((SUBLANES, pad_ref.shape[1]), F32)
    pad_ref[0:SUBLANES, :] = z
    pad_ref[SUBLANES + n_lat:2 * SUBLANES + n_lat, :] = z
    pad_ref[2 * SUBLANES + t:3 * SUBLANES + t, :] = z


def _pad_base(c, n_lat_chunks):
    return c * CHUNK + (SUBLANES if c < n_lat_chunks else 2 * SUBLANES)


def _conv_silu(pad_ref, w_ref, b_ref, out_ref, n, n_lat_chunks):
    width = pad_ref.shape[1]
    for c in range(n):
        base = _pad_base(c, n_lat_chunks)
        for l0 in range(0, width, LANES):
            acc = None
            for j in range(CONV_W):
                term = w_ref[j:j + 1, l0:l0 + LANES] * pad_ref[base + j - CONV_R:base + j - CONV_R + CHUNK,
                                                               l0:l0 + LANES]
                acc = term if acc is None else acc + term
            out_ref[c * CHUNK:(c + 1) * CHUNK, l0:l0 + LANES] = _silu(acc + b_ref[:, l0:l0 + LANES])


def _mlstm_kernel(u_ref, cw_ref, cb_ref, gb_ref, ng_ref, o_ref,
                  pad_ref, qk_ref, fc_ref, rc_ref, dc_ref, tot_ref, mloc_ref, stm_ref, sqk_ref, *, n_lat_chunks):
    t = u_ref.shape[0]
    n = t // CHUNK
    n_lat = n_lat_chunks * CHUNK
    tri = _tri_incl(CHUNK)
    lane = _lane_iota((CHUNK, LANES))
    hi_half = lane >= LANES // 2
    row = _row_iota((CHUNK, CHUNK))
    col = _lane_iota((CHUNK, CHUNK))
    masks = (row >= col, col >= row)
    ones = jnp.ones((CHUNK, LANES), F32)

    _zero_pads(pad_ref, n_lat, t)
    pad_ref[SUBLANES:SUBLANES + n_lat, :] = u_ref[0:n_lat, 0:LANES]
    pad_ref[2 * SUBLANES + n_lat:2 * SUBLANES + t, :] = u_ref[n_lat:t, 0:LANES]
    _conv_silu(pad_ref, cw_ref, cb_ref, qk_ref, n, n_lat_chunks)

    def khat_of(qk):
        return jnp.where(hi_half, qk, 0.0) * (ML_DQK ** -0.5)

    sel_r = lax.broadcasted_iota(jnp.int32, (LANES, 4 * LANES), 0)
    sel_c = lax.broadcasted_iota(jnp.int32, (LANES, 4 * LANES), 1)
    spread = (sel_r == (sel_c // LANES) * ML_GATE_REP).astype(BF16)

    def load_gates(c):
        rows = slice(c * CHUNK, (c + 1) * CHUNK)
        g = u_ref[rows, 3 * LANES:4 * LANES] + gb_ref[...]
        g = jnp.where(hi_half, _log_sigmoid(g), g)
        hi, mid, lo = _split3(g)
        d = functools.partial(jnp.dot, preferred_element_type=F32)
        wide = d(hi, spread) + (d(mid, spread) + d(lo, spread))
        rc_ref[0, rows, :] = wide[:, 0:LANES]
        rc_ref[1, rows, :] = wide[:, LANES:2 * LANES]
        return wide[:, 2 * LANES:]

    def emit_gates(c, lf, p):
        rows = slice(c * CHUNK, (c + 1) * CHUNK)
        tot_b = p[CHUNK - 1:CHUNK, LANES:]
        f_dir = (p[:, 0:LANES], tot_b - p[:, LANES:] + lf[:, LANES:])
        for d in range(2):
            fc_ref[d, rows, :] = f_dir[d]
            rc_ref[d, rows, :] = rc_ref[d, rows, :] - f_dir[d]

    _chunk_cumsums(n, tri, load_gates, emit_gates)

    def local(c):
        rows = _chunk_rows(c)
        qk = qk_ref[rows, :]
        khat = khat_of(qk)
        sqk_ref[c] = _bdot_nt(jnp.where(hi_half, _swap_halves(qk), 0.0), khat)
        vaug = jnp.concatenate([u_ref[rows, LANES:2 * LANES], ones], axis=1)
        for d in range(2):
            fc = fc_ref[d, rows, :]
            tt = fc[CHUNK - 1:CHUNK, :] if d == 0 else fc[0:1, :]
            gend = tt + rc_ref[d, rows, :]
            mloc = jnp.max(gend, axis=0, keepdims=True)
            dc_ref[d, c] = _bdot_tn(khat * jnp.exp(gend - mloc), vaug)
            tot_ref[d, c] = _row_bcast(tt)
            mloc_ref[d, c] = _row_bcast(mloc)

    _for_chunks(n, local)

    def scan(s, carry):
        new = []
        for d, idx in ((0, lax.rem(s + n_lat_chunks, n)), (1, n - 1 - s)):
            cst, m = carry[d]
            inc = dc_ref[d, idx]
            dc_ref[d, idx] = cst
            stm_ref[d, idx] = _row_bcast(m)
            tt = tot_ref[d, idx][0:1]
            ml = mloc_ref[d, idx][0:1]
            m_new = jnp.maximum(tt + m, ml)
            a = jnp.exp(tt + m - m_new)[:, 0:1]
            sc = jnp.exp(ml - m_new)[:, 0:1]
            new.append((a * cst + sc * inc, m_new))
        return tuple(new)

    init = (jnp.zeros((LANES, 2 * LANES), F32), jnp.full((1, LANES), M_INIT, F32))
    lax.fori_loop(0, n, scan, (init, init))

    def finish(c):
        rows = _chunk_rows(c)
        qk = qk_ref[rows, :]
        qhat = jnp.where(hi_half, _swap_halves(qk), 0.0)
        s_qk = sqk_ref[c]
        vaug = jnp.concatenate([u_ref[rows, LANES:2 * LANES], ones], axis=1)
        rc_t = jnp.where(hi_half, rc_ref[1, rows, :], rc_ref[0, rows, :]).T
        h = None
        for d in range(2):
            fc = fc_ref[d, rows, :]
            rc_row = rc_t[d * (LANES // 2):d * (LANES // 2) + 1, :]
            dlog = jnp.where(masks[d], fc + rc_row, -jnp.inf)
            inter = fc + stm_ref[d, c][0:1]
            m_row = jnp.maximum(inter, jnp.max(dlog, axis=1, keepdims=True))
            w_inter = jnp.exp(inter - m_row)
            nd = (_bdot(s_qk * jnp.exp(dlog - m_row), vaug)
                  + jnp.concatenate([w_inter, w_inter], axis=1) * _bdot(qhat, dc_ref[d, c]))
            hd = nd[:, 0:LANES] / jnp.maximum(jnp.abs(nd[:, LANES:]), jnp.exp(-m_row))
            h = hd if h is None else h + hd
        h = _rms(h) * ng_ref[...]
        o_ref[rows, :] = (h * _sigmoid(u_ref[rows, 2 * LANES:3 * LANES])).astype(o_ref.dtype)

    _for_chunks(n, finish)


def _mlstm_mixer(u, cw, cb, gb, ng, n_lat):
    nb, t, _ = u.shape
    n = t // CHUNK
    head = lambda rows: pl.BlockSpec((None, rows, LANES), lambda b, h: (h, 0, 0))
    return pl.pallas_call(
        functools.partial(_mlstm_kernel, n_lat_chunks=n_lat // CHUNK),
        grid=(nb, ML_HEADS),
        in_specs=[pl.BlockSpec((None, t, ML_HEAD_COLS), lambda b, h: (b, 0, h)),
                  head(SUBLANES), head(1), head(1), head(1)],
        out_specs=pl.BlockSpec((None, t, ML_DV), lambda b, h: (b, 0, h)),
        out_shape=jax.ShapeDtypeStruct((nb, t, ML_WIDTH), BF16),
        scratch_shapes=[pltpu.VMEM((t + 3 * SUBLANES, LANES), F32), pltpu.VMEM((t, LANES), F32),
                        pltpu.VMEM((2, t, LANES), F32), pltpu.VMEM((2, t, LANES), F32),
                        pltpu.VMEM((2, n, LANES, 2 * LANES), F32),
                        pltpu.VMEM((2, n, SUBLANES, LANES), F32), pltpu.VMEM((2, n, SUBLANES, LANES), F32),
                        pltpu.VMEM((2, n, SUBLANES, LANES), F32), pltpu.VMEM((n, CHUNK, CHUNK), F32)],
        compiler_params=pltpu.CompilerParams(dimension_semantics=("arbitrary", "arbitrary"),
                                             vmem_limit_bytes=VMEM_LIMIT),
        name="mlstm_mixer",
    )(u, cw, cb, gb, ng)


M2_CONV_COLS = M2_GROUP_X + 2 * M2_DSTATE
M2_X0 = M2_GROUP_X
M2_DT0 = M2_X0 + M2_CONV_COLS


def _ssd_kernel(u_ref, cw_ref, cb_ref, dtb_ref, alog_ref, dsk_ref, ng_ref, o_ref,
                pad_ref, xc_ref, dt_ref, dh_ref, a_ref, y_ref, *, n_lat_chunks):
    t = u_ref.shape[0]
    n = t // CHUNK
    n_lat = n_lat_chunks * CHUNK
    n_state = 2 * M2_PAIRS
    tri = _tri_incl(CHUNK)
    lo_half = _lane_iota((CHUNK, LANES)) < LANES // 2
    row = _row_iota((CHUNK, CHUNK))
    col = _lane_iota((CHUNK, CHUNK))
    masks = (row >= col, col >= row)
    fwd_cols = _lane_iota((CHUNK, M2_DT_COLS)) < M2_DT_COLS // 2
    a_row = -jnp.exp(alog_ref[...])

    _zero_pads(pad_ref, n_lat, t)
    pad_ref[SUBLANES:SUBLANES + n_lat, :] = u_ref[0:n_lat, M2_X0:M2_DT0]
    pad_ref[2 * SUBLANES + n_lat:2 * SUBLANES + t, :] = u_ref[n_lat:t, M2_X0:M2_DT0]
    _conv_silu(pad_ref, cw_ref, cb_ref, xc_ref, n, n_lat_chunks)

    cum_ref = pad_ref

    spread = (lax.broadcasted_iota(jnp.int32, (LANES, M2_DT_COLS), 0)
              == lax.broadcasted_iota(jnp.int32, (LANES, M2_DT_COLS), 1) // M2_HEADDIM).astype(BF16)

    def load_decay(c):
        rows = slice(c * CHUNK, (c + 1) * CHUNK)
        hi, mid, lo = _split3(u_ref[rows, M2_DT0:])
        d = functools.partial(jnp.dot, preferred_element_type=F32)
        raw = d(hi, spread) + (d(mid, spread) + d(lo, spread))
        dt = _softplus(raw + dtb_ref[...])
        dt_ref[rows, :] = dt
        return dt * a_row

    def emit_decay(c, da, p):
        tot = p[CHUNK - 1:CHUNK, :]
        cum_ref[c * CHUNK:(c + 1) * CHUNK, :] = jnp.where(fwd_cols, p, tot - p + da)

    _chunk_cumsums(n, tri, load_decay, emit_decay)

    def local(c):
        rows = _chunk_rows(c)
        dt = dt_ref[rows, :]
        cum = cum_ref[rows, :]
        tot = jnp.where(fwd_cols[0:1], cum[CHUNK - 1:CHUNK, :], cum[0:1, :])
        x = xc_ref[rows, 0:M2_GROUP_X]
        bm = xc_ref[rows, M2_GROUP_X:M2_GROUP_X + M2_DSTATE]
        cm = xc_ref[rows, M2_GROUP_X + M2_DSTATE:]
        g = _bdot_nt(cm, bm)
        y = [None] * M2_PAIRS
        for d in range(2):
            for p in range(M2_PAIRS):
                k = d * M2_PAIRS + p
                sl = slice(k * LANES, (k + 1) * LANES)
                fp = cum[:, sl]
                tt = tot[:, sl]
                xdt = x[:, p * LANES:(p + 1) * LANES] * dt[:, sl]
                dh_ref[c, k] = _bdot_tn(bm, jnp.exp(tt - fp) * xdt)
                a_ref[c, k] = _row_bcast(jnp.exp(tt))
                sw = _swap_halves(fp)
                fpt = fp.T
                halves = []
                for hh, fh in enumerate((jnp.where(lo_half, fp, sw), jnp.where(lo_half, sw, fp))):
                    f_row = fpt[hh * M2_HEADDIM:hh * M2_HEADDIM + 1, :]
                    dec = jnp.exp(jnp.where(masks[d], fh - f_row, -jnp.inf))
                    halves.append(_bdot(g * dec, xdt))
                yp = jnp.where(lo_half, halves[0], halves[1])
                y[p] = yp if y[p] is None else y[p] + yp
        for p in range(M2_PAIRS):
            y_ref[p, rows, :] = y[p]

    _for_chunks(n, local)

    def scan(s, carry):
        f = lax.rem(s + n_lat_chunks, n)
        g = n - 1 - s
        new = []
        for k in range(n_state):
            idx = f if k < M2_PAIRS else g
            inc = dh_ref[idx, k]
            dh_ref[idx, k] = carry[k]
            new.append(carry[k] * a_ref[idx, k][0:1] + inc)
        return tuple(new)

    lax.fori_loop(0, n, scan, tuple(jnp.zeros((M2_DSTATE, LANES), F32) for _ in range(n_state)))

    def finish(c):
        rows = _chunk_rows(c)
        cum = cum_ref[rows, :]
        cm = xc_ref[rows, M2_GROUP_X + M2_DSTATE:]
        y = [y_ref[p, rows, :] for p in range(M2_PAIRS)]
        for k in range(n_state):
            p = k % M2_PAIRS
            y[p] = y[p] + jnp.exp(cum[:, k * LANES:(k + 1) * LANES]) * _bdot(cm, dh_ref[c, k])
        y = jnp.concatenate(y, axis=1) + dsk_ref[...] * xc_ref[rows, 0:M2_GROUP_X]
        y = _rms(y * _silu(u_ref[rows, 0:M2_X0])) * ng_ref[...]
        o_ref[rows, :] = y.astype(o_ref.dtype)

    _for_chunks(n, finish)


def _ssd_mixer(u, cw, cb, dtb, alog, dsk, ng, n_lat):
    nb, t, _ = u.shape
    n = t // CHUNK
    grp = lambda rows, width: pl.BlockSpec((None, rows, width), lambda b, g: (g, 0, 0))
    return pl.pallas_call(
        functools.partial(_ssd_kernel, n_lat_chunks=n_lat // CHUNK),
        grid=(nb, M2_GROUPS),
        in_specs=[pl.BlockSpec((None, t, M2_GROUP_COLS), lambda b, g: (b, 0, g)),
                  grp(SUBLANES, M2_CONV_COLS), grp(1, M2_CONV_COLS), grp(1, M2_DT_COLS), grp(1, M2_DT_COLS),
                  grp(1, M2_GROUP_X), grp(1, M2_GROUP_X)],
        out_specs=pl.BlockSpec((None, t, M2_GROUP_X), lambda b, g: (b, 0, g)),
        out_shape=jax.ShapeDtypeStruct((nb, t, M2_WIDTH), BF16),
        scratch_shapes=[pltpu.VMEM((t + 3 * SUBLANES, M2_CONV_COLS), F32), pltpu.VMEM((t, M2_CONV_COLS), F32),
                        pltpu.VMEM((t, M2_DT_COLS), F32),
                        pltpu.VMEM((n, 2 * M2_PAIRS, M2_DSTATE, LANES), F32),
                        pltpu.VMEM((n, 2 * M2_PAIRS, SUBLANES, LANES), F32),
                        pltpu.VMEM((M2_PAIRS, t, LANES), F32)],
        compiler_params=pltpu.CompilerParams(dimension_semantics=("arbitrary", "arbitrary"),
                                             vmem_limit_bytes=VMEM_LIMIT),
        name="ssd_mixer",
    )(u, cw, cb, dtb, alog, dsk, ng)


ROW_SLABS = D_MODEL // LANES


def _slab(c, n_rows):
    return pl.ds(c, n_rows, stride=ROW_SLABS)


def _outproj_kernel(x_ref, og_ref, om_ref, osl_ref, osc_ref, w_ref, gate_ref, sh_ref, sc_ref, g2_ref, rw_ref,
                    rb_ref, xo_ref, h2_ref, lg_ref, *, n_lat_tiles):
    o_ssd = jnp.where(pl.program_id(1) < n_lat_tiles, osl_ref[...], osc_ref[...])
    mix = (jnp.dot(og_ref[...], w_ref[0:GLA_WIDTH, :], preferred_element_type=F32)
           + jnp.dot(om_ref[...], w_ref[GLA_WIDTH:GLA_WIDTH + ML_WIDTH, :], preferred_element_type=F32)
           + jnp.dot(o_ssd, w_ref[GLA_WIDTH + ML_WIDTH:, :], preferred_element_type=F32))
    x = x_ref[...] + gate_ref[...] * mix
    xo_ref[...] = x
    h2 = (_rms(x) * g2_ref[...]) * (1.0 + sc_ref[...]) + sh_ref[...]
    for c in range(ROW_SLABS):
        h2_ref[_slab(c, TOK_TILE), :] = h2[:, c * LANES:(c + 1) * LANES]
    lg_ref[...] = _hdot(h2, rw_ref[...]) + rb_ref[...]


def _out_proj(xa, o_gla, o_ml, o_m2_lat, o_m2, w_out, mod3, norm_g, rw, rb, n_lat):
    nb, t, d = xa.shape
    nt = t // TOK_TILE
    n_lat_tiles = n_lat // TOK_TILE
    tok = lambda width: pl.BlockSpec((None, TOK_TILE, width), lambda b, i: (b, i, 0))
    const = lambda r, c: pl.BlockSpec((r, c), lambda b, i: (0, 0))
    mod = lambda which: _mod_spec(which, n_lat_tiles, nb)
    ssd_lat = pl.BlockSpec((None, TOK_TILE, M2_WIDTH), lambda b, i: (b, jnp.minimum(i, n_lat_tiles - 1), 0))
    ssd_ctx = pl.BlockSpec((None, TOK_TILE, M2_WIDTH), lambda b, i: (b, jnp.maximum(i, n_lat_tiles), 0))
    return pl.pallas_call(
        functools.partial(_outproj_kernel, n_lat_tiles=n_lat_tiles),
        grid=(nb, nt),
        in_specs=[tok(d), tok(GLA_WIDTH), tok(ML_WIDTH), ssd_lat, ssd_ctx, const(MIX_WIDTH, d),
                  mod(2), mod(3), mod(4), const(1, d), const(d, LANES), const(1, LANES)],
        out_specs=[tok(d),
                   pl.BlockSpec((TOK_TILE * ROW_SLABS, LANES), lambda b, i: (b * nt + i, 0)),
                   pl.BlockSpec((TOK_TILE, LANES), lambda b, i: (b * nt + i, 0))],
        out_shape=[jax.ShapeDtypeStruct((nb, t, d), F32),
                   jax.ShapeDtypeStruct((nb * t * ROW_SLABS, LANES), F32),
                   jax.ShapeDtypeStruct((nb * t, LANES), F32)],
        compiler_params=pltpu.CompilerParams(dimension_semantics=("arbitrary", "arbitrary"),
                                             vmem_limit_bytes=VMEM_LIMIT),
        name="out_proj",
    )(xa, o_gla, o_ml, o_m2_lat, o_m2, w_out, mod3, mod3, mod3, norm_g.reshape(1, d), rw, rb)


ROUTE_TILE = 256


def _route_kernel(lg_ref, e_ref, gt_ref, rk_ref, cnt_ref, base_ref):
    @pl.when(pl.program_id(0) == 0)
    def _():
        base_ref[...] = jnp.zeros_like(base_ref)

    lane = _lane_iota((ROUTE_TILE, LANES))
    work = lg_ref[...]
    vals, idxs = [], []
    for _ in range(TOP_K):
        m = jnp.max(work, axis=1, keepdims=True)
        idx = jnp.min(jnp.where(work == m, lane, LANES), axis=1, keepdims=True)
        vals.append(m)
        idxs.append(idx)
        work = jnp.where(lane == idx, -jnp.inf, work)
    ex = [jnp.exp(v - vals[0]) for v in vals]
    inv = 1.0 / (ex[0] + ex[1] + ex[2] + ex[3])
    r = _row_iota((ROUTE_TILE, ROUTE_TILE))
    c = _lane_iota((ROUTE_TILE, ROUTE_TILE))
    earlier = (r > c).astype(BF16)
    base = base_ref[0:1, :]
    e_out = jnp.zeros((ROUTE_TILE, LANES), jnp.int32)
    g_out = jnp.zeros((ROUTE_TILE, LANES), F32)
    r_out = jnp.zeros((ROUTE_TILE, LANES), F32)
    for k in range(TOP_K):
        onehot = (lane == idxs[k]).astype(F32)
        within = jnp.dot(earlier, onehot.astype(BF16), preferred_element_type=F32)
        rank = jnp.sum((base + within) * onehot, axis=1, keepdims=True)
        base = base + jnp.sum(onehot, axis=0, keepdims=True)
        e_out = jnp.where(lane == k, idxs[k], e_out)
        g_out = jnp.where(lane == k, ex[k] * inv, g_out)
        r_out = jnp.where(lane == k, rank, r_out)
    base_ref[...] = _row_bcast(base)
    cnt_ref[...] = _row_bcast(base)
    e_ref[...] = e_out
    gt_ref[...] = g_out
    rk_ref[...] = r_out


def _route(logits):
    n_tok = logits.shape[0]
    tile = pl.BlockSpec((ROUTE_TILE, LANES), lambda i: (i, 0))
    return pl.pallas_call(
        _route_kernel,
        grid=(n_tok // ROUTE_TILE,),
        in_specs=[tile],
        out_specs=[tile, tile, tile, pl.BlockSpec((SUBLANES, LANES), lambda i: (0, 0))],
        out_shape=[jax.ShapeDtypeStruct((n_tok, LANES), jnp.int32),
                   jax.ShapeDtypeStruct((n_tok, LANES), F32),
                   jax.ShapeDtypeStruct((n_tok, LANES), F32),
                   jax.ShapeDtypeStruct((SUBLANES, LANES), F32)],
        scratch_shapes=[pltpu.VMEM((SUBLANES, LANES), F32)],
        compiler_params=pltpu.CompilerParams(dimension_semantics=("arbitrary",)),
        name="moe_route",
    )(logits)


DISPATCH_TILE = 512


def _row_slab(r):
    return pl.ds(pl.multiple_of(r * ROW_SLABS, ROW_SLABS), ROW_SLABS)


WAIT_UNROLL = 32
ISSUE_UNROLL = 8


def _drain_rows(src_ref, dst_ref, sem, n_rows):
    assert n_rows % WAIT_UNROLL == 0

    def body(i, carry):
        for _ in range(WAIT_UNROLL):
            pltpu.make_async_copy(src_ref.at[_row_slab(0)], dst_ref.at[_row_slab(0)], sem).wait()
        return carry

    lax.fori_loop(0, n_rows // WAIT_UNROLL, body, 0)


def _dispatch_kernel(dest_ref, h_ref, xb_in, xb_hbm, sem):
    del xb_in

    def issue(i, carry):
        for j in range(ISSUE_UNROLL):
            t = i * ISSUE_UNROLL + j
            for k in range(TOP_K):
                pltpu.make_async_copy(h_ref.at[_row_slab(t)],
                                      xb_hbm.at[_row_slab(dest_ref[0, t * TOP_K + k])], sem).start(priority=k % 2)
        return carry

    lax.fori_loop(0, DISPATCH_TILE // ISSUE_UNROLL, issue, 0)
    _drain_rows(h_ref, xb_hbm, sem, DISPATCH_TILE * TOP_K)


def _dispatch(dest, h2, xb):
    n_tiles = dest.shape[0] // (DISPATCH_TILE * TOP_K)
    return pl.pallas_call(
        _dispatch_kernel,
        grid=(n_tiles,),
        in_specs=[pl.BlockSpec((None, 1, DISPATCH_TILE * TOP_K), lambda i: (i, 0, 0), memory_space=pltpu.SMEM),
                  pl.BlockSpec((DISPATCH_TILE * ROW_SLABS, LANES), lambda i: (i, 0)),
                  pl.BlockSpec(memory_space=pl.ANY)],
        out_specs=pl.BlockSpec(memory_space=pl.ANY),
        out_shape=jax.ShapeDtypeStruct(xb.shape, xb.dtype),
        scratch_shapes=[pltpu.SemaphoreType.DMA(())],
        input_output_aliases={2: 0},
        compiler_params=pltpu.CompilerParams(dimension_semantics=("arbitrary",)),
        name="moe_dispatch",
    )(dest.reshape(n_tiles, 1, DISPATCH_TILE * TOP_K), h2, xb)


def _expert_kernel(be_ref, nv_ref, x_ref, wgu_ref, bgu_ref, wdn_ref, bdn_ref, y_ref, wgu_bf, wdn_bf, act_ref):
    i = pl.program_id(0)
    valid = i < nv_ref[0]
    fresh = jnp.logical_or(i == 0, be_ref[i] != be_ref[jnp.maximum(i - 1, 0)])

    @pl.when(jnp.logical_not(valid))
    def _():
        y_ref[...] = jnp.zeros_like(y_ref)

    @pl.when(jnp.logical_and(valid, fresh))
    def _():
        for r0 in range(0, D_MODEL, LANES):
            wgu_bf[r0:r0 + LANES, :] = wgu_ref[r0:r0 + LANES, :].astype(BF16)
            wdn_bf[r0:r0 + LANES, :] = wdn_ref[r0:r0 + LANES, :].astype(BF16)

    @pl.when(valid)
    def _():
        x = jnp.concatenate([x_ref[_slab(c, MOE_TILE), :] for c in range(ROW_SLABS)], axis=1).astype(BF16)
        half = D_EXPERT // 2
        for c0 in range(0, D_EXPERT, half):
            glu = jnp.dot(x, wgu_bf[:, c0:c0 + half], preferred_element_type=F32) + bgu_ref[:, c0:c0 + half]
            lin = (jnp.dot(x, wgu_bf[:, D_EXPERT + c0:D_EXPERT + c0 + half], preferred_element_type=F32)
                   + bgu_ref[:, D_EXPERT + c0:D_EXPERT + c0 + half])
            glu = jnp.minimum(glu, SWIGLU_LIMIT)
            lin = jnp.clip(lin, -SWIGLU_LIMIT, SWIGLU_LIMIT)
            act_ref[:, c0:c0 + half] = (glu * _sigmoid(SWIGLU_ALPHA * glu) * (lin + 1.0)).astype(BF16)
        y = jnp.dot(act_ref[...], wdn_bf[...], preferred_element_type=F32) + bdn_ref[...]
        for c in range(ROW_SLABS):
            y_ref[_slab(c, MOE_TILE), :] = y[:, c * LANES:(c + 1) * LANES]


def _experts(layer, block_e, n_valid, xb, w_gu, b_gu, w_dn, b_dn):
    n_blocks = block_e.shape[0]
    blk = lambda i, be, nv: jnp.minimum(i, nv[0] - 1)
    rows = pl.BlockSpec((MOE_TILE * ROW_SLABS, LANES), lambda i, be, nv: (blk(i, be, nv), 0))
    per_e = lambda r, c: pl.BlockSpec((None, None, r, c), lambda i, be, nv: (layer, be[blk(i, be, nv)], 0, 0))
    return pl.pallas_call(
        _expert_kernel,
        grid_spec=pltpu.PrefetchScalarGridSpec(
            num_scalar_prefetch=2,
            grid=(n_blocks,),
            in_specs=[rows, per_e(D_MODEL, 2 * D_EXPERT), per_e(1, 2 * D_EXPERT),
                      per_e(D_EXPERT, D_MODEL), per_e(1, D_MODEL)],
            out_specs=pl.BlockSpec((MOE_TILE * ROW_SLABS, LANES), lambda i, be, nv: (i, 0)),
            scratch_shapes=[pltpu.VMEM((D_MODEL, 2 * D_EXPERT), BF16), pltpu.VMEM((D_EXPERT, D_MODEL), BF16),
                            pltpu.VMEM((MOE_TILE, D_EXPERT), BF16)]),
        out_shape=jax.ShapeDtypeStruct(xb.shape, F32),
        compiler_params=pltpu.CompilerParams(dimension_semantics=("arbitrary",), vmem_limit_bytes=VMEM_LIMIT),
        name="moe_experts",
    )(block_e, n_valid, xb, w_gu, b_gu[:, :, None, :], w_dn, b_dn[:, :, None, :])


COMBINE_TILE = 256


def _combine_kernel(dest_ref, dnext_ref, yb_hbm, x_ref, gt_ref, mg_ref, o_ref, buf_ref, sem, *, n_steps):
    step = pl.program_id(0) * pl.num_programs(1) + pl.program_id(1)
    slot = lax.rem(step, 2)

    def gather(d_ref, into):
        def body(i, carry):
            for j in range(ISSUE_UNROLL):
                t = i * ISSUE_UNROLL + j
                for k in range(TOP_K):
                    pltpu.make_async_copy(yb_hbm.at[_row_slab(d_ref[0, t * TOP_K + k])],
                                          buf_ref.at[into, _row_slab(k * COMBINE_TILE + t)],
                                          sem.at[into]).start(priority=k % 2)
            return carry

        lax.fori_loop(0, COMBINE_TILE // ISSUE_UNROLL, body, 0)

    @pl.when(step == 0)
    def _():
        gather(dest_ref, 0)

    @pl.when(step + 1 < n_steps)
    def _():
        gather(dnext_ref, 1 - slot)

    _drain_rows(yb_hbm, buf_ref.at[slot], sem.at[slot], COMBINE_TILE * TOP_K)
    gates = gt_ref[...]
    acc = None
    for k in range(TOP_K):
        yk = jnp.concatenate(
            [buf_ref[slot, pl.ds(k * COMBINE_TILE * ROW_SLABS + c, COMBINE_TILE, stride=ROW_SLABS), :]
             for c in range(ROW_SLABS)], axis=1)
        term = gates[:, k:k + 1] * yk
        acc = term if acc is None else acc + term
    o_ref[...] = x_ref[...] + mg_ref[...] * acc


def _combine(dest, yb, xa, gates, mod3, n_lat):
    nb, t, d = xa.shape
    nt = t // COMBINE_TILE
    n_steps = nb * nt
    n_lat_tiles = n_lat // COMBINE_TILE
    tok = pl.BlockSpec((None, COMBINE_TILE, d), lambda b, i: (b, i, 0))
    rows_of = lambda step_of: pl.BlockSpec((None, 1, COMBINE_TILE * TOP_K),
                                           lambda b, i: (step_of(b * nt + i), 0, 0), memory_space=pltpu.SMEM)
    dest3 = dest.reshape(n_steps, 1, COMBINE_TILE * TOP_K)
    return pl.pallas_call(
        functools.partial(_combine_kernel, n_steps=n_steps),
        grid=(nb, nt),
        in_specs=[rows_of(lambda s: s), rows_of(lambda s: jnp.minimum(s + 1, n_steps - 1)),
                  pl.BlockSpec(memory_space=pl.ANY), tok,
                  pl.BlockSpec((COMBINE_TILE, LANES), lambda b, i: (b * nt + i, 0)),
                  _mod_spec(5, n_lat_tiles, nb)],
        out_specs=tok,
        out_shape=jax.ShapeDtypeStruct((nb, t, d), F32),
        scratch_shapes=[pltpu.VMEM((2, TOP_K * COMBINE_TILE * ROW_SLABS, LANES), F32),
                        pltpu.SemaphoreType.DMA((2,))],
        compiler_params=pltpu.CompilerParams(dimension_semantics=("arbitrary", "arbitrary"),
                                             vmem_limit_bytes=VMEM_LIMIT),
        name="moe_combine",
    )(dest3, dest3, yb, xa, gates, mod3)


def _final_norm_kernel(x_ref, g_ref, o_ref):
    o_ref[...] = _rms(x_ref[...]) * g_ref[...]


def _final_norm(xa, g, n_lat):
    nb, _, d = xa.shape
    tok = pl.BlockSpec((None, TOK_TILE, d), lambda b, i: (b, i, 0))
    return pl.pallas_call(
        _final_norm_kernel,
        grid=(nb, n_lat // TOK_TILE),
        in_specs=[tok, pl.BlockSpec((1, d), lambda b, i: (0, 0))],
        out_specs=tok,
        out_shape=jax.ShapeDtypeStruct((nb, n_lat, d), F32),
        compiler_params=pltpu.CompilerParams(dimension_semantics=("arbitrary", "arbitrary")),
        name="final_norm",
    )(xa, g.reshape(1, d))


def _pack_w_in(w_in, cols):
    lead = w_in.shape[:-1]
    parts = []
    i = 0
    while i < len(cols):
        c = int(cols[i])
        j = i + 1
        if c < 0:
            while j < len(cols) and cols[j] < 0:
                j += 1
            parts.append(jnp.zeros(lead + (j - i,), BF16))
        elif j < len(cols) and cols[j] == c:
            while j < len(cols) and cols[j] == c:
                j += 1
            parts.append(jnp.broadcast_to(w_in[..., c:c + 1].astype(BF16), lead + (j - i,)))
        else:
            while j < len(cols) and cols[j] == cols[j - 1] + 1:
                j += 1
            parts.append(w_in[..., c:c + j - i].astype(BF16))
        i = j
    return jnp.concatenate(parts, axis=-1)


def _pack_gla(w_gate2, b_gate):
    nl = w_gate2.shape[0]
    w = w_gate2.reshape(nl, 2, GLA_RANK, GLA_HEADS, GLA_DK).transpose(0, 3, 1, 2, 4)
    z = jnp.zeros((nl, GLA_HEADS, GLA_RANK, GLA_DK), F32)
    top = jnp.concatenate([w[:, :, 0], z], axis=-1)
    bot = jnp.concatenate([z, w[:, :, 1]], axis=-1)
    w2 = jnp.concatenate([top, bot, jnp.zeros((nl, GLA_HEADS, LANES - 2 * GLA_RANK, LANES), F32)], axis=2)
    b2 = b_gate.reshape(nl, 2, GLA_HEADS, GLA_DK).transpose(0, 2, 1, 3).reshape(nl, GLA_HEADS, 1, LANES)
    return w2, b2


def _pack_mlstm(conv_w, conv_b, b_i, b_f, norm_g):
    nl = conv_w.shape[0]
    cw = conv_w.reshape(nl, CONV_W, 2, ML_HEADS, ML_DQK).transpose(0, 3, 1, 2, 4).reshape(nl, ML_HEADS, CONV_W, LANES)
    cw = jnp.pad(cw, ((0, 0), (0, 0), (0, SUBLANES - CONV_W), (0, 0)))
    cb = conv_b.reshape(nl, 2, ML_HEADS, ML_DQK).transpose(0, 2, 1, 3).reshape(nl, ML_HEADS, 1, LANES)
    gates = jnp.stack([b_i[:, 0], b_i[:, 1], b_f[:, 0], b_f[:, 1]], axis=-1)
    gb = jnp.repeat(gates, ML_GATE_REP, axis=-1).reshape(nl, ML_HEADS, 1, LANES)
    return cw, cb, gb, norm_g.reshape(nl, ML_HEADS, 1, ML_DV)


def _pack_ssd(conv_w, conv_b, dt_bias, a_log, d_skip, norm_g):
    nl = conv_w.shape[0]
    bc = M2_GROUPS * M2_DSTATE

    def conv_cols(a):
        lead = a.shape[:-1]
        x = a[..., :M2_WIDTH].reshape(*lead, M2_GROUPS, M2_GROUP_X)
        b = a[..., M2_WIDTH:M2_WIDTH + bc].reshape(*lead, M2_GROUPS, M2_DSTATE)
        c = a[..., M2_WIDTH + bc:].reshape(*lead, M2_GROUPS, M2_DSTATE)
        return jnp.concatenate([x, b, c], axis=-1)

    cw = jnp.pad(conv_cols(conv_w).transpose(0, 2, 1, 3), ((0, 0), (0, 0), (0, SUBLANES - CONV_W), (0, 0)))
    cb = conv_cols(conv_b).reshape(nl, M2_GROUPS, 1, M2_CONV_COLS)

    def per_dir(a):
        a = a.reshape(nl, 2, M2_GROUPS, M2_GROUP_HEADS).transpose(0, 2, 1, 3)
        return jnp.repeat(a, M2_HEADDIM, axis=-1).reshape(nl, M2_GROUPS, 1, M2_DT_COLS)

    dsk = jnp.repeat(d_skip.reshape(nl, M2_GROUPS, M2_GROUP_HEADS), M2_HEADDIM, axis=-1)
    return (cw, cb, per_dir(dt_bias), per_dir(a_log), dsk.reshape(nl, M2_GROUPS, 1, M2_GROUP_X),
            norm_g.reshape(nl, M2_GROUPS, 1, M2_GROUP_X))


def _moe_plan(e_arr, rank_arr, counts_row, n_blocks):
    counts = counts_row[0, :N_EXPERTS].astype(jnp.int32)
    padded = (counts + MOE_TILE - 1) // MOE_TILE * MOE_TILE
    pad_end = jnp.cumsum(padded)
    pad_start = pad_end - padded
    e = e_arr[:, :TOP_K]
    dest = (jnp.take(pad_start, e) + rank_arr[:, :TOP_K].astype(jnp.int32)).reshape(-1)
    block_start = jnp.arange(n_blocks, dtype=jnp.int32) * MOE_TILE
    block_e = jnp.sum((pad_end[None, :] <= block_start[:, None]).astype(jnp.int32), axis=1)
    block_e = jnp.minimum(block_e, N_EXPERTS - 1)
    n_valid = (pad_end[-1:] // MOE_TILE).astype(jnp.int32)
    return dest, block_e, n_valid


def kernel(x, c, ctx, c_ctx, w_mod, b_mod, norm1_g, w_in, gla_w_gate2, gla_b_gate, gla_norm_g, ml_conv_w,
           ml_conv_b, ml_b_i, ml_b_f, ml_norm_g, m2_conv_w, m2_conv_b, m2_dt_bias, m2_A_log, m2_D, m2_norm_g,
           w_out, norm2_g, router_w, router_b, moe_w_gu, moe_b_gu, moe_w_dn, moe_b_dn, final_norm_g):
    nb, n_lat, d = x.shape
    n_ctx = ctx.shape[1]
    n_layers = w_mod.shape[0]
    t = n_lat + n_ctx
    assert d == D_MODEL and n_lat % (GRID_W * SUBLANES) == 0 and n_lat % TOK_TILE == 0 and n_ctx % TOK_TILE == 0
    assert (nb * t) % DISPATCH_TILE == 0

    mod_rows = -(-(nb + 1) // SUBLANES) * SUBLANES
    c_rows = jnp.concatenate([c, c_ctx[None], jnp.zeros((mod_rows - nb - 1, d), F32)], axis=0)
    mod = _mod_table(c_rows, w_mod, b_mod).reshape(n_layers, mod_rows * 6, 1, d)

    w_in_a = _pack_w_in(w_in, _IN_COLS[:GLA_PACK + ML_PACK])
    w_in_b = _pack_w_in(w_in, _IN_COLS[GLA_PACK + ML_PACK:])
    w_out_p = w_out.astype(BF16)
    grid_rows = n_lat // GRID_W
    gla_w2, gla_b2 = _pack_gla(gla_w_gate2, gla_b_gate)
    ml_cw, ml_cb, ml_gb, ml_ng = _pack_mlstm(ml_conv_w, ml_conv_b, ml_b_i, ml_b_f, ml_norm_g)
    m2_cw, m2_cb, m2_dtb, m2_alog, m2_dsk, m2_ng = _pack_ssd(m2_conv_w, m2_conv_b, m2_dt_bias, m2_A_log, m2_D,
                                                            m2_norm_g)
    rw = jnp.pad(router_w, ((0, 0), (0, 0), (0, LANES - N_EXPERTS)))
    rb = jnp.pad(router_b, ((0, 0), (0, LANES - N_EXPERTS)), constant_values=M_INIT).reshape(n_layers, 1, LANES)

    n_assign = nb * t * TOP_K
    n_blocks = n_assign // MOE_TILE + N_EXPERTS
    xb = jnp.zeros((n_blocks * MOE_TILE * ROW_SLABS, LANES), F32)

    xa = jnp.concatenate([x, ctx], axis=1)
    for l in range(n_layers):
        u_gla, u_ml = _in_proj(xa, None, mod[l], norm1_g[l], w_in_a[l], (GLA_PACK, ML_PACK), n_lat)
        x_cm = xa[:, :n_lat].reshape(nb, grid_rows, GRID_W, d).transpose(0, 2, 1, 3).reshape(nb, n_lat, d)
        u_m2, = _in_proj(xa, x_cm, mod[l], norm1_g[l], w_in_b[l], (M2_PACK,), n_lat)
        o_gla = _gla_mixer(u_gla, gla_w2[l], gla_b2[l], gla_norm_g[l].reshape(1, GLA_DV), n_lat)
        o_ml = _mlstm_mixer(u_ml, ml_cw[l], ml_cb[l], ml_gb[l], ml_ng[l], n_lat)
        o_m2 = _ssd_mixer(u_m2, m2_cw[l], m2_cb[l], m2_dtb[l], m2_alog[l], m2_dsk[l], m2_ng[l], n_lat)
        o_m2_lat = (o_m2[:, :n_lat].reshape(nb, GRID_W, grid_rows, M2_WIDTH).transpose(0, 2, 1, 3)
                    .reshape(nb, n_lat, M2_WIDTH))
        xa, h2, logits = _out_proj(xa, o_gla, o_ml, o_m2_lat, o_m2, w_out_p[l], mod[l], norm2_g[l], rw[l], rb[l],
                                   n_lat)
        e_arr, gates, rank_arr, counts = _route(logits)
        dest, block_e, n_valid = _moe_plan(e_arr, rank_arr, counts, n_blocks)
        xb = _dispatch(dest, h2, xb)
        yb = _experts(l, block_e, n_valid, xb, moe_w_gu, moe_b_gu, moe_w_dn, moe_b_dn)
        xa = _combine(dest, yb, xa, gates, mod[l], n_lat)
    return _final_norm(xa, final_norm_g, n_lat)
```

```python
import functools
import math

import numpy as np
import jax
import jax.numpy as jnp
from jax import lax
from jax.experimental import pallas as pl
from jax.experimental.pallas import tpu as pltpu

F32 = jnp.float32
BF16 = jnp.bfloat16

D_MODEL = 1024
GRID_W = 64
GLA_HEADS, GLA_DK, GLA_DV, GLA_RANK = 4, 64, 128, 16
GLA_NORMALIZER = 16.0
ML_HEADS, ML_DQK, ML_DV = 4, 64, 128
M2_HEADS, M2_HEADDIM, M2_GROUPS, M2_DSTATE = 8, 64, 2, 128
CONV_W = 7
CONV_R = CONV_W // 2
N_EXPERTS, TOP_K, D_EXPERT = 32, 4, 1024
SWIGLU_LIMIT, SWIGLU_ALPHA = 7.0, 1.702
EPS = 1e-6
M_INIT = -1e30

GLA_WIDTH = GLA_HEADS * GLA_DV
ML_WIDTH = ML_HEADS * ML_DV
M2_WIDTH = M2_HEADS * M2_HEADDIM
MIX_WIDTH = GLA_WIDTH + ML_WIDTH + M2_WIDTH
GLA_IN = 2 * GLA_HEADS * GLA_DK + 2 * GLA_WIDTH + 2 * GLA_RANK
ML_IN = 2 * ML_HEADS * ML_DQK + 2 * ML_WIDTH + 4 * ML_HEADS
M2_CONV_DIM = M2_WIDTH + 2 * M2_GROUPS * M2_DSTATE
M2_IN = M2_WIDTH + M2_CONV_DIM + 2 * M2_HEADS
IN_WIDTH = GLA_IN + ML_IN + M2_IN

LANES = 128
SUBLANES = 8
CHUNK = 128
TOK_TILE = 256
MOE_TILE = 512
VMEM_LIMIT = 52 * 1024 * 1024

GLA_HEAD_COLS = 3 * LANES
GLA_PACK = GLA_HEADS * GLA_HEAD_COLS + LANES
ML_HEAD_COLS = 4 * LANES
ML_PACK = ML_HEADS * ML_HEAD_COLS
M2_GROUP_HEADS = M2_HEADS // M2_GROUPS
M2_GROUP_X = M2_GROUP_HEADS * M2_HEADDIM
M2_PAIRS = M2_GROUP_HEADS // 2
M2_DT_COLS = 2 * M2_PAIRS * LANES
M2_GROUP_COLS = 2 * M2_GROUP_X + 2 * M2_DSTATE + LANES
M2_PACK = M2_GROUPS * M2_GROUP_COLS
IN_PACK = GLA_PACK + ML_PACK + M2_PACK
ML_GATE_REP = LANES // 4


def _in_proj_column_map():
    cols = []
    qk = GLA_HEADS * GLA_DK
    for h in range(GLA_HEADS):
        cols += list(range(h * GLA_DK, (h + 1) * GLA_DK))
        cols += list(range(qk + h * GLA_DK, qk + (h + 1) * GLA_DK))
        cols += list(range(2 * qk + h * GLA_DV, 2 * qk + (h + 1) * GLA_DV))
        cols += list(range(2 * qk + GLA_WIDTH + h * GLA_DV, 2 * qk + GLA_WIDTH + (h + 1) * GLA_DV))
    lr0 = 2 * qk + 2 * GLA_WIDTH
    cols += list(range(lr0, lr0 + 2 * GLA_RANK)) + [-1] * (LANES - 2 * GLA_RANK)
    a0 = GLA_IN
    qk = ML_HEADS * ML_DQK
    g0 = a0 + 2 * qk + 2 * ML_WIDTH
    for h in range(ML_HEADS):
        cols += list(range(a0 + h * ML_DQK, a0 + (h + 1) * ML_DQK))
        cols += list(range(a0 + qk + h * ML_DQK, a0 + qk + (h + 1) * ML_DQK))
        cols += list(range(a0 + 2 * qk + h * ML_DV, a0 + 2 * qk + (h + 1) * ML_DV))
        cols += list(range(a0 + 2 * qk + ML_WIDTH + h * ML_DV, a0 + 2 * qk + ML_WIDTH + (h + 1) * ML_DV))
        for gate in range(4):
            cols += [g0 + gate * ML_HEADS + h] * ML_GATE_REP
    a1 = GLA_IN + ML_IN
    x0 = a1 + M2_WIDTH
    dt0 = a1 + M2_WIDTH + M2_CONV_DIM
    for g in range(M2_GROUPS):
        cols += list(range(a1 + g * M2_GROUP_X, a1 + (g + 1) * M2_GROUP_X))
        cols += list(range(x0 + g * M2_GROUP_X, x0 + (g + 1) * M2_GROUP_X))
        cols += list(range(x0 + M2_WIDTH + g * M2_DSTATE, x0 + M2_WIDTH + (g + 1) * M2_DSTATE))
        cols += list(range(x0 + M2_WIDTH + M2_GROUPS * M2_DSTATE + g * M2_DSTATE,
                           x0 + M2_WIDTH + M2_GROUPS * M2_DSTATE + (g + 1) * M2_DSTATE))
        for d in range(2):
            for h in range(M2_GROUP_HEADS):
                cols += [dt0 + d * M2_HEADS + g * M2_GROUP_HEADS + h]
        cols += [-1] * (LANES - 2 * M2_GROUP_HEADS)
    cols = np.asarray(cols, np.int32)
    assert cols.shape == (IN_PACK,)
    return cols


_IN_COLS = _in_proj_column_map()


def _bdot(a, b):
    return jnp.dot(a.astype(BF16), b.astype(BF16), preferred_element_type=F32)


def _bdot_nt(a, b):
    return lax.dot_general(a.astype(BF16), b.astype(BF16), (((1,), (1,)), ((), ())),
                           preferred_element_type=F32)


def _bdot_tn(a, b):
    return lax.dot_general(a.astype(BF16), b.astype(BF16), (((0,), (0,)), ((), ())),
                           preferred_element_type=F32)


def _split2(a):
    hi = a.astype(BF16)
    lo = (a - hi.astype(F32)).astype(BF16)
    return hi, lo


def _split3(a):
    hi = a.astype(BF16)
    r = a - hi.astype(F32)
    mid = r.astype(BF16)
    lo = (r - mid.astype(F32)).astype(BF16)
    return hi, mid, lo


def _hdot(a, b):
    ah, al = _split2(a)
    bh, bl = _split2(b)
    d = functools.partial(jnp.dot, preferred_element_type=F32)
    return d(ah, bh) + (d(ah, bl) + d(al, bh))


def _tri_dot(tri, a):
    hi, mid, lo = _split3(a)
    d = functools.partial(jnp.dot, preferred_element_type=F32)
    return d(tri, hi) + (d(tri, mid) + d(tri, lo))


def _rms(x):
    return x * lax.rsqrt(jnp.mean(x * x, axis=-1, keepdims=True) + EPS)


def _sigmoid(x):
    return 1.0 / (1.0 + jnp.exp(-x))


def _silu(x):
    return x * _sigmoid(x)


def _log_sigmoid(x):
    return jnp.minimum(x, 0.0) - jnp.log(1.0 + jnp.exp(-jnp.abs(x)))


def _softplus(x):
    return jnp.maximum(x, 0.0) + jnp.log(1.0 + jnp.exp(-jnp.abs(x)))


def _lane_iota(shape):
    return lax.broadcasted_iota(jnp.int32, shape, len(shape) - 1)


def _row_iota(shape):
    return lax.broadcasted_iota(jnp.int32, shape, len(shape) - 2)


def _lane_rep(x, lo, width):
    lane = _lane_iota(x.shape)
    y = jnp.where((lane >= lo) & (lane < lo + width), x, 0.0)
    w = width
    while w < LANES:
        y = y + pltpu.roll(y, w, axis=1)
        w *= 2
    return y


def _swap_halves(x):
    return pltpu.roll(x, LANES // 2, axis=1)


def _tri_incl(n):
    r = lax.broadcasted_iota(jnp.int32, (n, n), 0)
    c = lax.broadcasted_iota(jnp.int32, (n, n), 1)
    return (r >= c).astype(BF16)


def _mod_kernel(c_ref, w_ref, b_ref, o_ref):
    o_ref[...] = _hdot(_silu(c_ref[...]), w_ref[...]) + b_ref[...]


def _mod_table(c_rows, w_mod, b_mod):
    n_layers, d, n6 = w_mod.shape
    r = c_rows.shape[0]
    tn = 1536
    return pl.pallas_call(
        _mod_kernel,
        grid=(n_layers, n6 // tn),
        in_specs=[pl.BlockSpec((r, d), lambda l, j: (0, 0)),
                  pl.BlockSpec((None, d, tn), lambda l, j: (l, 0, j)),
                  pl.BlockSpec((None, 1, tn), lambda l, j: (l, 0, j))],
        out_specs=pl.BlockSpec((None, r, tn), lambda l, j: (l, 0, j)),
        out_shape=jax.ShapeDtypeStruct((n_layers, r, n6), F32),
        compiler_params=pltpu.CompilerParams(dimension_semantics=("arbitrary", "arbitrary"),
                                             vmem_limit_bytes=VMEM_LIMIT),
        name="adaln_mod",
    )(c_rows, w_mod, b_mod.reshape(n_layers, 1, n6))


def _inproj_kernel(*refs, n_lat_tiles, widths, two_sources):
    if two_sources:
        xl_ref, xc_ref, sh_ref, sc_ref, g_ref, w_ref = refs[:6]
        outs = refs[6:]
        x = jnp.where(pl.program_id(1) < n_lat_tiles, xl_ref[...], xc_ref[...])
    else:
        x_ref, sh_ref, sc_ref, g_ref, w_ref = refs[:5]
        outs = refs[5:]
        x = x_ref[...]
    h = _rms(x) * g_ref[...]
    h = (h * (1.0 + sc_ref[...]) + sh_ref[...]).astype(BF16)
    c0 = 0
    for o_ref, width in zip(outs, widths):
        o_ref[...] = jnp.dot(h, w_ref[:, c0:c0 + width], preferred_element_type=F32)
        c0 += width


def _mod_spec(which, n_lat_tiles, n_batch):
    def imap(b, t):
        row = jnp.where(t < n_lat_tiles, b, n_batch)
        return (row * 6 + which, 0, 0)
    return pl.BlockSpec((None, 1, D_MODEL), imap)


def _in_proj(xa, x_lat, mod3, norm_g, w_pack, widths, n_lat):
    nb, t, d = xa.shape
    n_lat_tiles = n_lat // TOK_TILE
    tok = lambda width: pl.BlockSpec((None, TOK_TILE, width), lambda b, i: (b, i, 0))
    if x_lat is None:
        x_specs, x_args = [tok(d)], (xa,)
    else:
        x_specs = [pl.BlockSpec((None, TOK_TILE, d), lambda b, i: (b, jnp.minimum(i, n_lat_tiles - 1), 0)),
                   pl.BlockSpec((None, TOK_TILE, d), lambda b, i: (b, jnp.maximum(i, n_lat_tiles), 0))]
        x_args = (x_lat, xa)
    return pl.pallas_call(
        functools.partial(_inproj_kernel, n_lat_tiles=n_lat_tiles, widths=widths, two_sources=x_lat is not None),
        grid=(nb, t // TOK_TILE),
        in_specs=x_specs + [_mod_spec(0, n_lat_tiles, nb), _mod_spec(1, n_lat_tiles, nb),
                            pl.BlockSpec((1, d), lambda b, i: (0, 0)),
                            pl.BlockSpec((d, sum(widths)), lambda b, i: (0, 0))],
        out_specs=[tok(w) for w in widths],
        out_shape=[jax.ShapeDtypeStruct((nb, t, w), F32) for w in widths],
        compiler_params=pltpu.CompilerParams(dimension_semantics=("arbitrary", "arbitrary"),
                                             vmem_limit_bytes=VMEM_LIMIT),
        name="in_proj",
    )(*x_args, mod3, mod3, norm_g.reshape(1, d), w_pack)


def _chunk_rows(c):
    return pl.ds(pl.multiple_of(c * CHUNK, CHUNK), CHUNK)


def _row_bcast(row):
    return jnp.broadcast_to(row, (SUBLANES, row.shape[-1]))


CHUNK_UNROLL = 6


def _for_chunks(n, body):
    assert n % CHUNK_UNROLL == 0

    def step(i, carry):
        for j in range(CHUNK_UNROLL):
            body(i * CHUNK_UNROLL + j)
        return carry

    lax.fori_loop(0, n // CHUNK_UNROLL, step, 0)


CUMSUM_GROUP = 6


def _chunk_cumsums(n, tri, load, emit):
    assert n % CUMSUM_GROUP == 0
    for c0 in range(0, n, CUMSUM_GROUP):
        xs = [load(c) for c in range(c0, c0 + CUMSUM_GROUP)]
        width = xs[0].shape[1]
        p = _tri_dot(tri, jnp.concatenate(xs, axis=1))
        for j in range(CUMSUM_GROUP):
            emit(c0 + j, xs[j], p[:, j * width:(j + 1) * width])


def _gla_kernel(u_ref, lr_ref, w2_ref, b2_ref, ng_ref, o_ref,
                cum_ref, attn_ref, q2_ref, ds_ref, a_ref, stf_ref, stb_ref, *, n_lat_chunks):
    t = u_ref.shape[0]
    n = t // CHUNK
    tri = _tri_incl(CHUNK)
    fwd = _lane_iota((CHUNK, LANES)) < LANES // 2
    row = _row_iota((CHUNK, CHUNK))
    col = _lane_iota((CHUNK, CHUNK))
    w2 = w2_ref[...]
    b2 = b2_ref[...]

    def log_decay(i, carry):
        rows = pl.ds(pl.multiple_of(i * TOK_TILE, TOK_TILE), TOK_TILE)
        cum_ref[rows, :] = _log_sigmoid(_hdot(lr_ref[rows, :], w2) + b2) * (1.0 / GLA_NORMALIZER)
        return carry

    lax.fori_loop(0, t // TOK_TILE, log_decay, 0)

    def emit_cum(c, la, p):
        tot = p[CHUNK - 1:CHUNK, :]
        cum_ref[c * CHUNK:(c + 1) * CHUNK, :] = jnp.where(fwd, p, tot - p + la)
        a_ref[c] = _row_bcast(jnp.exp(tot))

    _chunk_cumsums(n, tri, lambda c: cum_ref[c * CHUNK:(c + 1) * CHUNK, :], emit_cum)

    def local(c):
        rows = _chunk_rows(c)
        qk = u_ref[rows, 0:LANES]
        v = u_ref[rows, LANES:2 * LANES]
        cum = cum_ref[rows, :]
        tot = jnp.where(fwd[0:1], cum[CHUNK - 1:CHUNK, :], cum[0:1, :])
        mid = cum[CHUNK // 2:CHUNK // 2 + 1, :]
        sw = _swap_halves(qk)
        qq = jnp.where(fwd, qk, sw) * (GLA_DK ** -0.5)
        kk = jnp.where(fwd, sw, qk)
        qe = qq * jnp.exp(cum - mid)
        ke = kk * jnp.exp(mid - cum)
        af = _bdot_nt(jnp.where(fwd, qe, 0.0), ke)
        ab = _bdot_nt(jnp.where(fwd, 0.0, qe), ke)
        attn = jnp.where(row >= col, af, 0.0) + jnp.where(col >= row, ab, 0.0)
        attn_ref[c] = attn.astype(BF16)
        q2_ref[rows, :] = qq * jnp.exp(cum)
        ds_ref[c] = _bdot_tn(v, kk * jnp.exp(tot - cum))

    _for_chunks(n, local)

    def scan(s, st):
        f = lax.rem(s + n_lat_chunks, n)
        g = n - 1 - s
        stf_ref[f] = st
        stb_ref[g] = st
        a = jnp.where(fwd[0:1], a_ref[f][0:1], a_ref[g][0:1])
        return st * a + jnp.where(fwd, ds_ref[f], ds_ref[g])

    lax.fori_loop(0, n, scan, jnp.zeros((GLA_DV, LANES), F32))

    def finish(c):
        rows = _chunk_rows(c)
        st = jnp.where(fwd, stf_ref[c], stb_ref[c])
        o = _bdot(attn_ref[c], u_ref[rows, LANES:2 * LANES]) + _bdot_nt(q2_ref[rows, :], st)
        o = _rms(o) * ng_ref[...]
        o_ref[rows, :] = (o * _silu(u_ref[rows, 2 * LANES:3 * LANES])).astype(o_ref.dtype)

    _for_chunks(n, finish)


def _gla_mixer(u, w2, b2, ng, n_lat):
    nb, t, _ = u.shape
    n = t // CHUNK
    return pl.pallas_call(
        functools.partial(_gla_kernel, n_lat_chunks=n_lat // CHUNK),
        grid=(nb, GLA_HEADS),
        in_specs=[pl.BlockSpec((None, t, GLA_HEAD_COLS), lambda b, h: (b, 0, h)),
                  pl.BlockSpec((None, t, LANES), lambda b, h: (b, 0, GLA_HEADS * GLA_HEAD_COLS // LANES)),
                  pl.BlockSpec((None, LANES, LANES), lambda b, h: (h, 0, 0)),
                  pl.BlockSpec((None, 1, LANES), lambda b, h: (h, 0, 0)),
                  pl.BlockSpec((1, LANES), lambda b, h: (0, 0))],
        out_specs=pl.BlockSpec((None, t, GLA_DV), lambda b, h: (b, 0, h)),
        out_shape=jax.ShapeDtypeStruct((nb, t, GLA_WIDTH), BF16),
        scratch_shapes=[pltpu.VMEM((t, LANES), F32), pltpu.VMEM((n, CHUNK, CHUNK), BF16),
                        pltpu.VMEM((t, LANES), F32),
                        pltpu.VMEM((n, GLA_DV, LANES), F32), pltpu.VMEM((n, SUBLANES, LANES), F32),
                        pltpu.VMEM((n, GLA_DV, LANES), F32), pltpu.VMEM((n, GLA_DV, LANES), F32)],
        compiler_params=pltpu.CompilerParams(dimension_semantics=("arbitrary", "arbitrary"),
                                             vmem_limit_bytes=VMEM_LIMIT),
        name="gla_mixer",
    )(u, u, w2, b2, ng)


def _zero_pads(pad_ref, n_lat, t):
    z = jnp.zeros((SUBLANES, pad_ref.shape[1]), F32)
    pad_ref[0:SUBLANES, :] = z
    pad_ref[SUBLANES + n_lat:2 * SUBLANES + n_lat, :] = z
    pad_ref[2 * SUBLANES + t:3 * SUBLANES + t, :] = z


def _pad_base(c, n_lat_chunks):
    return c * CHUNK + (SUBLANES if c < n_lat_chunks else 2 * SUBLANES)


def _conv_silu(pad_ref, w_ref, b_ref, out_ref, n, n_lat_chunks):
    width = pad_ref.shape[1]
    for c in range(n):
        base = _pad_base(c, n_lat_chunks)
        for l0 in range(0, width, LANES):
            acc = None
            for j in range(CONV_W):
                term = w_ref[j:j + 1, l0:l0 + LANES] * pad_ref[base + j - CONV_R:base + j - CONV_R + CHUNK,
                                                               l0:l0 + LANES]
                acc = term if acc is None else acc + term
            out_ref[c * CHUNK:(c + 1) * CHUNK, l0:l0 + LANES] = _silu(acc + b_ref[:, l0:l0 + LANES])


def _mlstm_kernel(u_ref, cw_ref, cb_ref, gb_ref, ng_ref, o_ref,
                  pad_ref, qk_ref, fc_ref, rc_ref, dc_ref, tot_ref, mloc_ref, stm_ref, sqk_ref, *, n_lat_chunks):
    t = u_ref.shape[0]
    n = t // CHUNK
    n_lat = n_lat_chunks * CHUNK
    tri = _tri_incl(CHUNK)
    lane = _lane_iota((CHUNK, LANES))
    hi_half = lane >= LANES // 2
    row = _row_iota((CHUNK, CHUNK))
    col = _lane_iota((CHUNK, CHUNK))
    masks = (row >= col, col >= row)
    ones = jnp.ones((CHUNK, LANES), F32)

    _zero_pads(pad_ref, n_lat, t)
    pad_ref[SUBLANES:SUBLANES + n_lat, :] = u_ref[0:n_lat, 0:LANES]
    pad_ref[2 * SUBLANES + n_lat:2 * SUBLANES + t, :] = u_ref[n_lat:t, 0:LANES]
    _conv_silu(pad_ref, cw_ref, cb_ref, qk_ref, n, n_lat_chunks)

    def khat_of(qk):
        return jnp.where(hi_half, qk, 0.0) * (ML_DQK ** -0.5)

    sel_r = lax.broadcasted_iota(jnp.int32, (LANES, 4 * LANES), 0)
    sel_c = lax.broadcasted_iota(jnp.int32, (LANES, 4 * LANES), 1)
    spread = (sel_r == (sel_c // LANES) * ML_GATE_REP).astype(BF16)

    def load_gates(c):
        rows = slice(c * CHUNK, (c + 1) * CHUNK)
        g = u_ref[rows, 3 * LANES:4 * LANES] + gb_ref[...]
        g = jnp.where(hi_half, _log_sigmoid(g), g)
        hi, mid, lo = _split3(g)
        d = functools.partial(jnp.dot, preferred_element_type=F32)
        wide = d(hi, spread) + (d(mid, spread) + d(lo, spread))
        rc_ref[0, rows, :] = wide[:, 0:LANES]
        rc_ref[1, rows, :] = wide[:, LANES:2 * LANES]
        return wide[:, 2 * LANES:]

    def emit_gates(c, lf, p):
        rows = slice(c * CHUNK, (c + 1) * CHUNK)
        tot_b = p[CHUNK - 1:CHUNK, LANES:]
        f_dir = (p[:, 0:LANES], tot_b - p[:, LANES:] + lf[:, LANES:])
        for d in range(2):
            fc_ref[d, rows, :] = f_dir[d]
            rc_ref[d, rows, :] = rc_ref[d, rows, :] - f_dir[d]

    _chunk_cumsums(n, tri, load_gates, emit_gates)

    def local(c):
        rows = _chunk_rows(c)
        qk = qk_ref[rows, :]
        khat = khat_of(qk)
        sqk_ref[c] = _bdot_nt(jnp.where(hi_half, _swap_halves(qk), 0.0), khat)
        vaug = jnp.concatenate([u_ref[rows, LANES:2 * LANES], ones], axis=1)
        for d in range(2):
            fc = fc_ref[d, rows, :]
            tt = fc[CHUNK - 1:CHUNK, :] if d == 0 else fc[0:1, :]
            gend = tt + rc_ref[d, rows, :]
            mloc = jnp.max(gend, axis=0, keepdims=True)
            dc_ref[d, c] = _bdot_tn(khat * jnp.exp(gend - mloc), vaug)
            tot_ref[d, c] = _row_bcast(tt)
            mloc_ref[d, c] = _row_bcast(mloc)

    _for_chunks(n, local)

    def scan(s, carry):
        new = []
        for d, idx in ((0, lax.rem(s + n_lat_chunks, n)), (1, n - 1 - s)):
            cst, m = carry[d]
            inc = dc_ref[d, idx]
            dc_ref[d, idx] = cst
            stm_ref[d, idx] = _row_bcast(m)
            tt = tot_ref[d, idx][0:1]
            ml = mloc_ref[d, idx][0:1]
            m_new = jnp.maximum(tt + m, ml)
            a = jnp.exp(tt + m - m_new)[:, 0:1]
            sc = jnp.exp(ml - m_new)[:, 0:1]
            new.append((a * cst + sc * inc, m_new))
        return tuple(new)

    init = (jnp.zeros((LANES, 2 * LANES), F32), jnp.full((1, LANES), M_INIT, F32))
    lax.fori_loop(0, n, scan, (init, init))

    def finish(c):
        rows = _chunk_rows(c)
        qk = qk_ref[rows, :]
        qhat = jnp.where(hi_half, _swap_halves(qk), 0.0)
        s_qk = sqk_ref[c]
        vaug = jnp.concatenate([u_ref[rows, LANES:2 * LANES], ones], axis=1)
        rc_t = jnp.where(hi_half, rc_ref[1, rows, :], rc_ref[0, rows, :]).T
        h = None
        for d in range(2):
            fc = fc_ref[d, rows, :]
            rc_row = rc_t[d * (LANES // 2):d * (LANES // 2) + 1, :]
            dlog = jnp.where(masks[d], fc + rc_row, -jnp.inf)
            inter = fc + stm_ref[d, c][0:1]
            m_row = jnp.maximum(inter, jnp.max(dlog, axis=1, keepdims=True))
            w_inter = jnp.exp(inter - m_row)
            nd = (_bdot(s_qk * jnp.exp(dlog - m_row), vaug)
                  + jnp.concatenate([w_inter, w_inter], axis=1) * _bdot(qhat, dc_ref[d, c]))
            hd = nd[:, 0:LANES] / jnp.maximum(jnp.abs(nd[:, LANES:]), jnp.exp(-m_row))
            h = hd if h is None else h + hd
        h = _rms(h) * ng_ref[...]
        o_ref[rows, :] = (h * _sigmoid(u_ref[rows, 2 * LANES:3 * LANES])).astype(o_ref.dtype)

    _for_chunks(n, finish)


def _mlstm_mixer(u, cw, cb, gb, ng, n_lat):
    nb, t, _ = u.shape
    n = t // CHUNK
    head = lambda rows: pl.BlockSpec((None, rows, LANES), lambda b, h: (h, 0, 0))
    return pl.pallas_call(
        functools.partial(_mlstm_kernel, n_lat_chunks=n_lat // CHUNK),
        grid=(nb, ML_HEADS),
        in_specs=[pl.BlockSpec((None, t, ML_HEAD_COLS), lambda b, h: (b, 0, h)),
                  head(SUBLANES), head(1), head(1), head(1)],
        out_specs=pl.BlockSpec((None, t, ML_DV), lambda b, h: (b, 0, h)),
        out_shape=jax.ShapeDtypeStruct((nb, t, ML_WIDTH), BF16),
        scratch_shapes=[pltpu.VMEM((t + 3 * SUBLANES, LANES), F32), pltpu.VMEM((t, LANES), F32),
                        pltpu.VMEM((2, t, LANES), F32), pltpu.VMEM((2, t, LANES), F32),
                        pltpu.VMEM((2, n, LANES, 2 * LANES), F32),
                        pltpu.VMEM((2, n, SUBLANES, LANES), F32), pltpu.VMEM((2, n, SUBLANES, LANES), F32),
                        pltpu.VMEM((2, n, SUBLANES, LANES), F32), pltpu.VMEM((n, CHUNK, CHUNK), F32)],
        compiler_params=pltpu.CompilerParams(dimension_semantics=("arbitrary", "arbitrary"),
                                             vmem_limit_bytes=VMEM_LIMIT),
        name="mlstm_mixer",
    )(u, cw, cb, gb, ng)


M2_CONV_COLS = M2_GROUP_X + 2 * M2_DSTATE
M2_X0 = M2_GROUP_X
M2_DT0 = M2_X0 + M2_CONV_COLS


def _ssd_kernel(u_ref, cw_ref, cb_ref, dtb_ref, alog_ref, dsk_ref, ng_ref, o_ref,
                pad_ref, xc_ref, dt_ref, dh_ref, a_ref, y_ref, *, n_lat_chunks):
    t = u_ref.shape[0]
    n = t // CHUNK
    n_lat = n_lat_chunks * CHUNK
    n_state = 2 * M2_PAIRS
    tri = _tri_incl(CHUNK)
    lo_half = _lane_iota((CHUNK, LANES)) < LANES // 2
    row = _row_iota((CHUNK, CHUNK))
    col = _lane_iota((CHUNK, CHUNK))
    masks = (row >= col, col >= row)
    fwd_cols = _lane_iota((CHUNK, M2_DT_COLS)) < M2_DT_COLS // 2
    a_row = -jnp.exp(alog_ref[...])

    _zero_pads(pad_ref, n_lat, t)
    pad_ref[SUBLANES:SUBLANES + n_lat, :] = u_ref[0:n_lat, M2_X0:M2_DT0]
    pad_ref[2 * SUBLANES + n_lat:2 * SUBLANES + t, :] = u_ref[n_lat:t, M2_X0:M2_DT0]
    _conv_silu(pad_ref, cw_ref, cb_ref, xc_ref, n, n_lat_chunks)

    cum_ref = pad_ref

    spread = (lax.broadcasted_iota(jnp.int32, (LANES, M2_DT_COLS), 0)
              == lax.broadcasted_iota(jnp.int32, (LANES, M2_DT_COLS), 1) // M2_HEADDIM).astype(BF16)

    def load_decay(c):
        rows = slice(c * CHUNK, (c + 1) * CHUNK)
        hi, mid, lo = _split3(u_ref[rows, M2_DT0:])
        d = functools.partial(jnp.dot, preferred_element_type=F32)
        raw = d(hi, spread) + (d(mid, spread) + d(lo, spread))
        dt = _softplus(raw + dtb_ref[...])
        dt_ref[rows, :] = dt
        return dt * a_row

    def emit_decay(c, da, p):
        tot = p[CHUNK - 1:CHUNK, :]
        cum_ref[c * CHUNK:(c + 1) * CHUNK, :] = jnp.where(fwd_cols, p, tot - p + da)

    _chunk_cumsums(n, tri, load_decay, emit_decay)

    def local(c):
        rows = _chunk_rows(c)
        dt = dt_ref[rows, :]
        cum = cum_ref[rows, :]
        tot = jnp.where(fwd_cols[0:1], cum[CHUNK - 1:CHUNK, :], cum[0:1, :])
        x = xc_ref[rows, 0:M2_GROUP_X]
        bm = xc_ref[rows, M2_GROUP_X:M2_GROUP_X + M2_DSTATE]
        cm = xc_ref[rows, M2_GROUP_X + M2_DSTATE:]
        g = _bdot_nt(cm, bm)
        y = [None] * M2_PAIRS
        for d in range(2):
            for p in range(M2_PAIRS):
                k = d * M2_PAIRS + p
                sl = slice(k * LANES, (k + 1) * LANES)
                fp = cum[:, sl]
                tt = tot[:, sl]
                xdt = x[:, p * LANES:(p + 1) * LANES] * dt[:, sl]
                dh_ref[c, k] = _bdot_tn(bm, jnp.exp(tt - fp) * xdt)
                a_ref[c, k] = _row_bcast(jnp.exp(tt))
                sw = _swap_halves(fp)
                fpt = fp.T
                halves = []
                for hh, fh in enumerate((jnp.where(lo_half, fp, sw), jnp.where(lo_half, sw, fp))):
                    f_row = fpt[hh * M2_HEADDIM:hh * M2_HEADDIM + 1, :]
                    dec = jnp.exp(jnp.where(masks[d], fh - f_row, -jnp.inf))
                    halves.append(_bdot(g * dec, xdt))
                yp = jnp.where(lo_half, halves[0], halves[1])
                y[p] = yp if y[p] is None else y[p] + yp
        for p in range(M2_PAIRS):
            y_ref[p, rows, :] = y[p]

    _for_chunks(n, local)

    def scan(s, carry):
        f = lax.rem(s + n_lat_chunks, n)
        g = n - 1 - s
        new = []
        for k in range(n_state):
            idx = f if k < M2_PAIRS else g
            inc = dh_ref[idx, k]
            dh_ref[idx, k] = carry[k]
            new.append(carry[k] * a_ref[idx, k][0:1] + inc)
        return tuple(new)

    lax.fori_loop(0, n, scan, tuple(jnp.zeros((M2_DSTATE, LANES), F32) for _ in range(n_state)))

    def finish(c):
        rows = _chunk_rows(c)
        cum = cum_ref[rows, :]
        cm = xc_ref[rows, M2_GROUP_X + M2_DSTATE:]
        y = [y_ref[p, rows, :] for p in range(M2_PAIRS)]
        for k in range(n_state):
            p = k % M2_PAIRS
            y[p] = y[p] + jnp.exp(cum[:, k * LANES:(k + 1) * LANES]) * _bdot(cm, dh_ref[c, k])
        y = jnp.concatenate(y, axis=1) + dsk_ref[...] * xc_ref[rows, 0:M2_GROUP_X]
        y = _rms(y * _silu(u_ref[rows, 0:M2_X0])) * ng_ref[...]
        o_ref[rows, :] = y.astype(o_ref.dtype)

    _for_chunks(n, finish)


def _ssd_mixer(u, cw, cb, dtb, alog, dsk, ng, n_lat):
    nb, t, _ = u.shape
    n = t // CHUNK
    grp = lambda rows, width: pl.BlockSpec((None, rows, width), lambda b, g: (g, 0, 0))
    return pl.pallas_call(
        functools.partial(_ssd_kernel, n_lat_chunks=n_lat // CHUNK),
        grid=(nb, M2_GROUPS),
        in_specs=[pl.BlockSpec((None, t, M2_GROUP_COLS), lambda b, g: (b, 0, g)),
                  grp(SUBLANES, M2_CONV_COLS), grp(1, M2_CONV_COLS), grp(1, M2_DT_COLS), grp(1, M2_DT_COLS),
                  grp(1, M2_GROUP_X), grp(1, M2_GROUP_X)],
        out_specs=pl.BlockSpec((None, t, M2_GROUP_X), lambda b, g: (b, 0, g)),
        out_shape=jax.ShapeDtypeStruct((nb, t, M2_WIDTH), BF16),
        scratch_shapes=[pltpu.VMEM((t + 3 * SUBLANES, M2_CONV_COLS), F32), pltpu.VMEM((t, M2_CONV_COLS), F32),
                        pltpu.VMEM((t, M2_DT_COLS), F32),
                        pltpu.VMEM((n, 2 * M2_PAIRS, M2_DSTATE, LANES), F32),
                        pltpu.VMEM((n, 2 * M2_PAIRS, SUBLANES, LANES), F32),
                        pltpu.VMEM((M2_PAIRS, t, LANES), F32)],
        compiler_params=pltpu.CompilerParams(dimension_semantics=("arbitrary", "arbitrary"),
                                             vmem_limit_bytes=VMEM_LIMIT),
        name="ssd_mixer",
    )(u, cw, cb, dtb, alog, dsk, ng)


ROW_SLABS = D_MODEL // LANES


def _slab(c, n_rows):
    return pl.ds(c, n_rows, stride=ROW_SLABS)


def _outproj_kernel(x_ref, og_ref, om_ref, osl_ref, osc_ref, w_ref, gate_ref, sh_ref, sc_ref, g2_ref, rw_ref,
                    rb_ref, xo_ref, h2_ref, lg_ref, *, n_lat_tiles):
    o_ssd = jnp.where(pl.program_id(1) < n_lat_tiles, osl_ref[...], osc_ref[...])
    mix = (jnp.dot(og_ref[...], w_ref[0:GLA_WIDTH, :], preferred_element_type=F32)
           + jnp.dot(om_ref[...], w_ref[GLA_WIDTH:GLA_WIDTH + ML_WIDTH, :], preferred_element_type=F32)
           + jnp.dot(o_ssd, w_ref[GLA_WIDTH + ML_WIDTH:, :], preferred_element_type=F32))
    x = x_ref[...] + gate_ref[...] * mix
    xo_ref[...] = x
    h2 = (_rms(x) * g2_ref[...]) * (1.0 + sc_ref[...]) + sh_ref[...]
    for c in range(ROW_SLABS):
        h2_ref[_slab(c, TOK_TILE), :] = h2[:, c * LANES:(c + 1) * LANES]
    lg_ref[...] = _hdot(h2, rw_ref[...]) + rb_ref[...]


def _out_proj(xa, o_gla, o_ml, o_m2_lat, o_m2, w_out, mod3, norm_g, rw, rb, n_lat):
    nb, t, d = xa.shape
    nt = t // TOK_TILE
    n_lat_tiles = n_lat // TOK_TILE
    tok = lambda width: pl.BlockSpec((None, TOK_TILE, width), lambda b, i: (b, i, 0))
    const = lambda r, c: pl.BlockSpec((r, c), lambda b, i: (0, 0))
    mod = lambda which: _mod_spec(which, n_lat_tiles, nb)
    ssd_lat = pl.BlockSpec((None, TOK_TILE, M2_WIDTH), lambda b, i: (b, jnp.minimum(i, n_lat_tiles - 1), 0))
    ssd_ctx = pl.BlockSpec((None, TOK_TILE, M2_WIDTH), lambda b, i: (b, jnp.maximum(i, n_lat_tiles), 0))
    return pl.pallas_call(
        functools.partial(_outproj_kernel, n_lat_tiles=n_lat_tiles),
        grid=(nb, nt),
        in_specs=[tok(d), tok(GLA_WIDTH), tok(ML_WIDTH), ssd_lat, ssd_ctx, const(MIX_WIDTH, d),
                  mod(2), mod(3), mod(4), const(1, d), const(d, LANES), const(1, LANES)],
        out_specs=[tok(d),
                   pl.BlockSpec((TOK_TILE * ROW_SLABS, LANES), lambda b, i: (b * nt + i, 0)),
                   pl.BlockSpec((TOK_TILE, LANES), lambda b, i: (b * nt + i, 0))],
        out_shape=[jax.ShapeDtypeStruct((nb, t, d), F32),
                   jax.ShapeDtypeStruct((nb * t * ROW_SLABS, LANES), F32),
                   jax.ShapeDtypeStruct((nb * t, LANES), F32)],
        compiler_params=pltpu.CompilerParams(dimension_semantics=("arbitrary", "arbitrary"),
                                             vmem_limit_bytes=VMEM_LIMIT),
        name="out_proj",
    )(xa, o_gla, o_ml, o_m2_lat, o_m2, w_out, mod3, mod3, mod3, norm_g.reshape(1, d), rw, rb)


ROUTE_TILE = 256


def _route_kernel(lg_ref, e_ref, gt_ref, rk_ref, cnt_ref, base_ref):
    @pl.when(pl.program_id(0) == 0)
    def _():
        base_ref[...] = jnp.zeros_like(base_ref)

    lane = _lane_iota((ROUTE_TILE, LANES))
    work = lg_ref[...]
    vals, idxs = [], []
    for _ in range(TOP_K):
        m = jnp.max(work, axis=1, keepdims=True)
        idx = jnp.min(jnp.where(work == m, lane, LANES), axis=1, keepdims=True)
        vals.append(m)
        idxs.append(idx)
        work = jnp.where(lane == idx, -jnp.inf, work)
    ex = [jnp.exp(v - vals[0]) for v in vals]
    inv = 1.0 / (ex[0] + ex[1] + ex[2] + ex[3])
    r = _row_iota((ROUTE_TILE, ROUTE_TILE))
    c = _lane_iota((ROUTE_TILE, ROUTE_TILE))
    earlier = (r > c).astype(BF16)
    base = base_ref[0:1, :]
    e_out = jnp.zeros((ROUTE_TILE, LANES), jnp.int32)
    g_out = jnp.zeros((ROUTE_TILE, LANES), F32)
    r_out = jnp.zeros((ROUTE_TILE, LANES), F32)
    for k in range(TOP_K):
        onehot = (lane == idxs[k]).astype(F32)
        within = jnp.dot(earlier, onehot.astype(BF16), preferred_element_type=F32)
        rank = jnp.sum((base + within) * onehot, axis=1, keepdims=True)
        base = base + jnp.sum(onehot, axis=0, keepdims=True)
        e_out = jnp.where(lane == k, idxs[k], e_out)
        g_out = jnp.where(lane == k, ex[k] * inv, g_out)
        r_out = jnp.where(lane == k, rank, r_out)
    base_ref[...] = _row_bcast(base)
    cnt_ref[...] = _row_bcast(base)
    e_ref[...] = e_out
    gt_ref[...] = g_out
    rk_ref[...] = r_out


def _route(logits):
    n_tok = logits.shape[0]
    tile = pl.BlockSpec((ROUTE_TILE, LANES), lambda i: (i, 0))
    return pl.pallas_call(
        _route_kernel,
        grid=(n_tok // ROUTE_TILE,),
        in_specs=[tile],
        out_specs=[tile, tile, tile, pl.BlockSpec((SUBLANES, LANES), lambda i: (0, 0))],
        out_shape=[jax.ShapeDtypeStruct((n_tok, LANES), jnp.int32),
                   jax.ShapeDtypeStruct((n_tok, LANES), F32),
                   jax.ShapeDtypeStruct((n_tok, LANES), F32),
                   jax.ShapeDtypeStruct((SUBLANES, LANES), F32)],
        scratch_shapes=[pltpu.VMEM((SUBLANES, LANES), F32)],
        compiler_params=pltpu.CompilerParams(dimension_semantics=("arbitrary",)),
        name="moe_route",
    )(logits)


DISPATCH_TILE = 512


def _row_slab(r):
    return pl.ds(pl.multiple_of(r * ROW_SLABS, ROW_SLABS), ROW_SLABS)


WAIT_UNROLL = 32
ISSUE_UNROLL = 8


def _drain_rows(src_ref, dst_ref, sem, n_rows):
    assert n_rows % WAIT_UNROLL == 0

    def body(i, carry):
        for _ in range(WAIT_UNROLL):
            pltpu.make_async_copy(src_ref.at[_row_slab(0)], dst_ref.at[_row_slab(0)], sem).wait()
        return carry

    lax.fori_loop(0, n_rows // WAIT_UNROLL, body, 0)


def _dispatch_kernel(dest_ref, h_ref, xb_in, xb_hbm, sem):
    del xb_in

    def issue(i, carry):
        for j in range(ISSUE_UNROLL):
            t = i * ISSUE_UNROLL + j
            for k in range(TOP_K):
                pltpu.make_async_copy(h_ref.at[_row_slab(t)],
                                      xb_hbm.at[_row_slab(dest_ref[0, t * TOP_K + k])], sem).start(priority=k % 2)
        return carry

    lax.fori_loop(0, DISPATCH_TILE // ISSUE_UNROLL, issue, 0)
    _drain_rows(h_ref, xb_hbm, sem, DISPATCH_TILE * TOP_K)


def _dispatch(dest, h2, xb):
    n_tiles = dest.shape[0] // (DISPATCH_TILE * TOP_K)
    return pl.pallas_call(
        _dispatch_kernel,
        grid=(n_tiles,),
        in_specs=[pl.BlockSpec((None, 1, DISPATCH_TILE * TOP_K), lambda i: (i, 0, 0), memory_space=pltpu.SMEM),
                  pl.BlockSpec((DISPATCH_TILE * ROW_SLABS, LANES), lambda i: (i, 0)),
                  pl.BlockSpec(memory_space=pl.ANY)],
        out_specs=pl.BlockSpec(memory_space=pl.ANY),
        out_shape=jax.ShapeDtypeStruct(xb.shape, xb.dtype),
        scratch_shapes=[pltpu.SemaphoreType.DMA(())],
        input_output_aliases={2: 0},
        compiler_params=pltpu.CompilerParams(dimension_semantics=("arbitrary",)),
        name="moe_dispatch",
    )(dest.reshape(n_tiles, 1, DISPATCH_TILE * TOP_K), h2, xb)


def _expert_kernel(be_ref, nv_ref, x_ref, wgu_ref, bgu_ref, wdn_ref, bdn_ref, y_ref, wgu_bf, wdn_bf, act_ref):
    i = pl.program_id(0)
    valid = i < nv_ref[0]
    fresh = jnp.logical_or(i == 0, be_ref[i] != be_ref[jnp.maximum(i - 1, 0)])

    @pl.when(jnp.logical_not(valid))
    def _():
        y_ref[...] = jnp.zeros_like(y_ref)

    @pl.when(jnp.logical_and(valid, fresh))
    def _():
        for r0 in range(0, D_MODEL, LANES):
            wgu_bf[r0:r0 + LANES, :] = wgu_ref[r0:r0 + LANES, :].astype(BF16)
            wdn_bf[r0:r0 + LANES, :] = wdn_ref[r0:r0 + LANES, :].astype(BF16)

    @pl.when(valid)
    def _():
        x = jnp.concatenate([x_ref[_slab(c, MOE_TILE), :] for c in range(ROW_SLABS)], axis=1).astype(BF16)
        half = D_EXPERT // 2
        for c0 in range(0, D_EXPERT, half):
            glu = jnp.dot(x, wgu_bf[:, c0:c0 + half], preferred_element_type=F32) + bgu_ref[:, c0:c0 + half]
            lin = (jnp.dot(x, wgu_bf[:, D_EXPERT + c0:D_EXPERT + c0 + half], preferred_element_type=F32)
                   + bgu_ref[:, D_EXPERT + c0:D_EXPERT + c0 + half])
            glu = jnp.minimum(glu, SWIGLU_LIMIT)
            lin = jnp.clip(lin, -SWIGLU_LIMIT, SWIGLU_LIMIT)
            act_ref[:, c0:c0 + half] = (glu * _sigmoid(SWIGLU_ALPHA * glu) * (lin + 1.0)).astype(BF16)
        y = jnp.dot(act_ref[...], wdn_bf[...], preferred_element_type=F32) + bdn_ref[...]
        for c in range(ROW_SLABS):
            y_ref[_slab(c, MOE_TILE), :] = y[:, c * LANES:(c + 1) * LANES]


def _experts(layer, block_e, n_valid, xb, w_gu, b_gu, w_dn, b_dn):
    n_blocks = block_e.shape[0]
    blk = lambda i, be, nv: jnp.minimum(i, nv[0] - 1)
    rows = pl.BlockSpec((MOE_TILE * ROW_SLABS, LANES), lambda i, be, nv: (blk(i, be, nv), 0))
    per_e = lambda r, c: pl.BlockSpec((None, None, r, c), lambda i, be, nv: (layer, be[blk(i, be, nv)], 0, 0))
    return pl.pallas_call(
        _expert_kernel,
        grid_spec=pltpu.PrefetchScalarGridSpec(
            num_scalar_prefetch=2,
            grid=(n_blocks,),
            in_specs=[rows, per_e(D_MODEL, 2 * D_EXPERT), per_e(1, 2 * D_EXPERT),
                      per_e(D_EXPERT, D_MODEL), per_e(1, D_MODEL)],
            out_specs=pl.BlockSpec((MOE_TILE * ROW_SLABS, LANES), lambda i, be, nv: (i, 0)),
            scratch_shapes=[pltpu.VMEM((D_MODEL, 2 * D_EXPERT), BF16), pltpu.VMEM((D_EXPERT, D_MODEL), BF16),
                            pltpu.VMEM((MOE_TILE, D_EXPERT), BF16)]),
        out_shape=jax.ShapeDtypeStruct(xb.shape, F32),
        compiler_params=pltpu.CompilerParams(dimension_semantics=("arbitrary",), vmem_limit_bytes=VMEM_LIMIT),
        name="moe_experts",
    )(block_e, n_valid, xb, w_gu, b_gu[:, :, None, :], w_dn, b_dn[:, :, None, :])


COMBINE_TILE = 256


def _combine_kernel(dest_ref, dnext_ref, yb_hbm, x_ref, gt_ref, mg_ref, fg_ref, o_ref, buf_ref, sem, *, n_steps,
                    final):
    step = pl.program_id(0) * pl.num_programs(1) + pl.program_id(1)
    slot = lax.rem(step, 2)

    def gather(d_ref, into):
        def body(i, carry):
            for j in range(ISSUE_UNROLL):
                t = i * ISSUE_UNROLL + j
                for k in range(TOP_K):
                    pltpu.make_async_copy(yb_hbm.at[_row_slab(d_ref[0, t * TOP_K + k])],
                                          buf_ref.at[into, _row_slab(k * COMBINE_TILE + t)],
                                          sem.at[into]).start(priority=k % 2)
            return carry

        lax.fori_loop(0, COMBINE_TILE // ISSUE_UNROLL, body, 0)

    @pl.when(step == 0)
    def _():
        gather(dest_ref, 0)

    @pl.when(step + 1 < n_steps)
    def _():
        gather(dnext_ref, 1 - slot)

    _drain_rows(yb_hbm, buf_ref.at[slot], sem.at[slot], COMBINE_TILE * TOP_K)
    gates = gt_ref[...]
    acc = None
    for k in range(TOP_K):
        yk = jnp.concatenate(
            [buf_ref[slot, pl.ds(k * COMBINE_TILE * ROW_SLABS + c, COMBINE_TILE, stride=ROW_SLABS), :]
             for c in range(ROW_SLABS)], axis=1)
        term = gates[:, k:k + 1] * yk
        acc = term if acc is None else acc + term
    x = x_ref[...] + mg_ref[...] * acc
    o_ref[...] = _rms(x) * fg_ref[...] if final else x


def _combine(dest, yb, xa, gates, mod3, n_lat, final_g=None):
    nb, t, d = xa.shape
    nt = t // COMBINE_TILE
    n_lat_tiles = n_lat // COMBINE_TILE
    final = final_g is not None
    nt_run = n_lat_tiles if final else nt
    n_steps = nb * nt_run
    tok = pl.BlockSpec((None, COMBINE_TILE, d), lambda b, i: (b, i, 0))

    def tile_of(step):
        return (step // nt_run) * nt + step % nt_run

    rows_of = lambda step_of: pl.BlockSpec((None, 1, COMBINE_TILE * TOP_K),
                                           lambda b, i: (tile_of(step_of(b * nt_run + i)), 0, 0),
                                           memory_space=pltpu.SMEM)
    dest3 = dest.reshape(nb * nt, 1, COMBINE_TILE * TOP_K)
    fg = jnp.ones((1, d), F32) if final_g is None else final_g.reshape(1, d)
    return pl.pallas_call(
        functools.partial(_combine_kernel, n_steps=n_steps, final=final),
        grid=(nb, nt_run),
        in_specs=[rows_of(lambda s: s), rows_of(lambda s: jnp.minimum(s + 1, n_steps - 1)),
                  pl.BlockSpec(memory_space=pl.ANY), tok,
                  pl.BlockSpec((COMBINE_TILE, LANES), lambda b, i: (b * nt + i, 0)),
                  _mod_spec(5, n_lat_tiles, nb), pl.BlockSpec((1, d), lambda b, i: (0, 0))],
        out_specs=tok,
        out_shape=jax.ShapeDtypeStruct((nb, nt_run * COMBINE_TILE, d), F32),
        scratch_shapes=[pltpu.VMEM((2, TOP_K * COMBINE_TILE * ROW_SLABS, LANES), F32),
                        pltpu.SemaphoreType.DMA((2,))],
        compiler_params=pltpu.CompilerParams(dimension_semantics=("arbitrary", "arbitrary"),
                                             vmem_limit_bytes=VMEM_LIMIT),
        name="moe_combine",
    )(dest3, dest3, yb, xa, gates, mod3, fg)


def _pack_w_in(w_in, cols):
    lead = w_in.shape[:-1]
    parts = []
    i = 0
    while i < len(cols):
        c = int(cols[i])
        j = i + 1
        if c < 0:
            while j < len(cols) and cols[j] < 0:
                j += 1
            parts.append(jnp.zeros(lead + (j - i,), BF16))
        elif j < len(cols) and cols[j] == c:
            while j < len(cols) and cols[j] == c:
                j += 1
            parts.append(jnp.broadcast_to(w_in[..., c:c + 1].astype(BF16), lead + (j - i,)))
        else:
            while j < len(cols) and cols[j] == cols[j - 1] + 1:
                j += 1
            parts.append(w_in[..., c:c + j - i].astype(BF16))
        i = j
    return jnp.concatenate(parts, axis=-1)


def _pack_gla(w_gate2, b_gate):
    nl = w_gate2.shape[0]
    w = w_gate2.reshape(nl, 2, GLA_RANK, GLA_HEADS, GLA_DK).transpose(0, 3, 1, 2, 4)
    z = jnp.zeros((nl, GLA_HEADS, GLA_RANK, GLA_DK), F32)
    top = jnp.concatenate([w[:, :, 0], z], axis=-1)
    bot = jnp.concatenate([z, w[:, :, 1]], axis=-1)
    w2 = jnp.concatenate([top, bot, jnp.zeros((nl, GLA_HEADS, LANES - 2 * GLA_RANK, LANES), F32)], axis=2)
    b2 = b_gate.reshape(nl, 2, GLA_HEADS, GLA_DK).transpose(0, 2, 1, 3).reshape(nl, GLA_HEADS, 1, LANES)
    return w2, b2


def _pack_mlstm(conv_w, conv_b, b_i, b_f, norm_g):
    nl = conv_w.shape[0]
    cw = conv_w.reshape(nl, CONV_W, 2, ML_HEADS, ML_DQK).transpose(0, 3, 1, 2, 4).reshape(nl, ML_HEADS, CONV_W, LANES)
    cw = jnp.pad(cw, ((0, 0), (0, 0), (0, SUBLANES - CONV_W), (0, 0)))
    cb = conv_b.reshape(nl, 2, ML_HEADS, ML_DQK).transpose(0, 2, 1, 3).reshape(nl, ML_HEADS, 1, LANES)
    gates = jnp.stack([b_i[:, 0], b_i[:, 1], b_f[:, 0], b_f[:, 1]], axis=-1)
    gb = jnp.repeat(gates, ML_GATE_REP, axis=-1).reshape(nl, ML_HEADS, 1, LANES)
    return cw, cb, gb, norm_g.reshape(nl, ML_HEADS, 1, ML_DV)


def _pack_ssd(conv_w, conv_b, dt_bias, a_log, d_skip, norm_g):
    nl = conv_w.shape[0]
    bc = M2_GROUPS * M2_DSTATE

    def conv_cols(a):
        lead = a.shape[:-1]
        x = a[..., :M2_WIDTH].reshape(*lead, M2_GROUPS, M2_GROUP_X)
        b = a[..., M2_WIDTH:M2_WIDTH + bc].reshape(*lead, M2_GROUPS, M2_DSTATE)
        c = a[..., M2_WIDTH + bc:].reshape(*lead, M2_GROUPS, M2_DSTATE)
        return jnp.concatenate([x, b, c], axis=-1)

    cw = jnp.pad(conv_cols(conv_w).transpose(0, 2, 1, 3), ((0, 0), (0, 0), (0, SUBLANES - CONV_W), (0, 0)))
    cb = conv_cols(conv_b).reshape(nl, M2_GROUPS, 1, M2_CONV_COLS)

    def per_dir(a):
        a = a.reshape(nl, 2, M2_GROUPS, M2_GROUP_HEADS).transpose(0, 2, 1, 3)
        return jnp.repeat(a, M2_HEADDIM, axis=-1).reshape(nl, M2_GROUPS, 1, M2_DT_COLS)

    dsk = jnp.repeat(d_skip.reshape(nl, M2_GROUPS, M2_GROUP_HEADS), M2_HEADDIM, axis=-1)
    return (cw, cb, per_dir(dt_bias), per_dir(a_log), dsk.reshape(nl, M2_GROUPS, 1, M2_GROUP_X),
            norm_g.reshape(nl, M2_GROUPS, 1, M2_GROUP_X))


def _moe_plan(e_arr, rank_arr, counts_row, n_blocks):
    counts = counts_row[0, :N_EXPERTS].astype(jnp.int32)
    padded = (counts + MOE_TILE - 1) // MOE_TILE * MOE_TILE
    pad_end = jnp.cumsum(padded)
    pad_start = pad_end - padded
    e = e_arr[:, :TOP_K]
    dest = (jnp.take(pad_start, e) + rank_arr[:, :TOP_K].astype(jnp.int32)).reshape(-1)
    block_start = jnp.arange(n_blocks, dtype=jnp.int32) * MOE_TILE
    block_e = jnp.sum((pad_end[None, :] <= block_start[:, None]).astype(jnp.int32), axis=1)
    block_e = jnp.minimum(block_e, N_EXPERTS - 1)
    n_valid = (pad_end[-1:] // MOE_TILE).astype(jnp.int32)
    return dest, block_e, n_valid


def kernel(x, c, ctx, c_ctx, w_mod, b_mod, norm1_g, w_in, gla_w_gate2, gla_b_gate, gla_norm_g, ml_conv_w,
           ml_conv_b, ml_b_i, ml_b_f, ml_norm_g, m2_conv_w, m2_conv_b, m2_dt_bias, m2_A_log, m2_D, m2_norm_g,
           w_out, norm2_g, router_w, router_b, moe_w_gu, moe_b_gu, moe_w_dn, moe_b_dn, final_norm_g):
    nb, n_lat, d = x.shape
    n_ctx = ctx.shape[1]
    n_layers = w_mod.shape[0]
    t = n_lat + n_ctx
    assert d == D_MODEL and n_lat % (GRID_W * SUBLANES) == 0 and n_lat % TOK_TILE == 0 and n_ctx % TOK_TILE == 0
    assert (nb * t) % DISPATCH_TILE == 0

    mod_rows = -(-(nb + 1) // SUBLANES) * SUBLANES
    c_rows = jnp.concatenate([c, c_ctx[None], jnp.zeros((mod_rows - nb - 1, d), F32)], axis=0)
    mod = _mod_table(c_rows, w_mod, b_mod).reshape(n_layers, mod_rows * 6, 1, d)

    w_in_a = _pack_w_in(w_in, _IN_COLS[:GLA_PACK + ML_PACK])
    w_in_b = _pack_w_in(w_in, _IN_COLS[GLA_PACK + ML_PACK:])
    w_out_p = w_out.astype(BF16)
    grid_rows = n_lat // GRID_W
    gla_w2, gla_b2 = _pack_gla(gla_w_gate2, gla_b_gate)
    ml_cw, ml_cb, ml_gb, ml_ng = _pack_mlstm(ml_conv_w, ml_conv_b, ml_b_i, ml_b_f, ml_norm_g)
    m2_cw, m2_cb, m2_dtb, m2_alog, m2_dsk, m2_ng = _pack_ssd(m2_conv_w, m2_conv_b, m2_dt_bias, m2_A_log, m2_D,
                                                            m2_norm_g)
    rw = jnp.pad(router_w, ((0, 0), (0, 0), (0, LANES - N_EXPERTS)))
    rb = jnp.pad(router_b, ((0, 0), (0, LANES - N_EXPERTS)), constant_values=M_INIT).reshape(n_layers, 1, LANES)

    n_assign = nb * t * TOP_K
    n_blocks = n_assign // MOE_TILE + N_EXPERTS
    xb = jnp.zeros((n_blocks * MOE_TILE * ROW_SLABS, LANES), F32)

    xa = jnp.concatenate([x, ctx], axis=1)
    for l in range(n_layers):
        u_gla, u_ml = _in_proj(xa, None, mod[l], norm1_g[l], w_in_a[l], (GLA_PACK, ML_PACK), n_lat)
        x_cm = xa[:, :n_lat].reshape(nb, grid_rows, GRID_W, d).transpose(0, 2, 1, 3).reshape(nb, n_lat, d)
        u_m2, = _in_proj(xa, x_cm, mod[l], norm1_g[l], w_in_b[l], (M2_PACK,), n_lat)
        o_gla = _gla_mixer(u_gla, gla_w2[l], gla_b2[l], gla_norm_g[l].reshape(1, GLA_DV), n_lat)
        o_ml = _mlstm_mixer(u_ml, ml_cw[l], ml_cb[l], ml_gb[l], ml_ng[l], n_lat)
        o_m2 = _ssd_mixer(u_m2, m2_cw[l], m2_cb[l], m2_dtb[l], m2_alog[l], m2_dsk[l], m2_ng[l], n_lat)
        o_m2_lat = (o_m2[:, :n_lat].reshape(nb, GRID_W, grid_rows, M2_WIDTH).transpose(0, 2, 1, 3)
                    .reshape(nb, n_lat, M2_WIDTH))
        xa, h2, logits = _out_proj(xa, o_gla, o_ml, o_m2_lat, o_m2, w_out_p[l], mod[l], norm2_g[l], rw[l], rb[l],
                                   n_lat)
        e_arr, gates, rank_arr, counts = _route(logits)
        dest, block_e, n_valid = _moe_plan(e_arr, rank_arr, counts, n_blocks)
        xb = _dispatch(dest, h2, xb)
        yb = _experts(l, block_e, n_valid, xb, moe_w_gu, moe_b_gu, moe_w_dn, moe_b_dn)
        xa = _combine(dest, yb, xa, gates, mod[l], n_lat, final_norm_g if l == n_layers - 1 else None)
    return xa
```
